```python
import jax
import jax.numpy as jnp
from jax import lax
import numpy as np

D_MODEL = 1024
BATCH = 8
SEQ = 2048
DEPTH = 4

GRID_W = 64
CTX_LEN = 256

POOL_GROUPS = 4
POOL_GROUP_DIM = 64
POOL_WIDTH = POOL_GROUPS * POOL_GROUP_DIM
POOL_HALF_WINDOWS = (1, 2, 4, 8)
NA_HEADS = 4
NA_HEAD_DIM = 64
NA_WIDTH = NA_HEADS * NA_HEAD_DIM
NA_WIN_ROWS = 8
NA_WIN_COLS = 16
ML_HEADS = 4
ML_HEAD_DIM = 128
ML_WIDTH = ML_HEADS * ML_HEAD_DIM
ML_CHUNK = 128
ML_CONV = 3
ML_M_INIT = -1e30
ROPE_THETA = 10000.0
D_FF = 3584
N_EXPERTS = 8
TOP_K = 2
MOE_BLOCK = 256
N_BRANCHES = 3
ALPHA = (2 * DEPTH) ** 0.25
BETA = (8 * DEPTH) ** -0.25
LN_EPS = 1e-6
OFF_NA = POOL_WIDTH
OFF_ML = OFF_NA + 3 * NA_WIDTH
OFF_MLO = OFF_ML + 3 * ML_WIDTH
OFF_MLG = OFF_MLO + ML_WIDTH
OFF_GATE = OFF_MLG + 4 * ML_HEADS
N_IN = OFF_GATE + N_BRANCHES * D_MODEL
IN_OFFSETS = (OFF_NA, OFF_ML, OFF_MLO, OFF_MLG, OFF_GATE)
F32 = jnp.float32

kernel_name = 'hybrid_pool_natten_mlstm_moe_dit'


def ln_plain(x):
    x32 = x.astype(F32)
    xc = x32 - jnp.mean(x32, axis=-1, keepdims=True)
    return (xc * lax.rsqrt(jnp.mean(xc * xc, axis=-1, keepdims=True) + LN_EPS)).astype(x.dtype)


def ln_affine(x, g, b):
    return ln_plain(x) * g + b


def modulate(xn, shift, scale):
    return xn * (1 + scale) + shift


def swiglu(h, w1, w3, w2):
    return (jax.nn.silu(h @ w1) * (h @ w3)) @ w2


def multiscale_pool(a, pool_w, pool_scale):
    B, L, _ = a.shape
    a32 = a.astype(F32).reshape(B, L, POOL_GROUPS, POOL_GROUP_DIM)
    csum = jnp.concatenate([jnp.zeros_like(a32[:, :1]), jnp.cumsum(a32, axis=1)], axis=1)
    t = jnp.arange(L)[:, None]
    half = jnp.array(POOL_HALF_WINDOWS)[None, :]
    lo = jnp.clip(t - half, 0, L)
    hi = jnp.clip(t + half, 0, L)
    gidx = jnp.arange(POOL_GROUPS)[None, :]
    mean = (csum[:, hi, gidx, :] - csum[:, lo, gidx, :]) / (hi - lo).astype(F32)[None, :, :, None]
    out = jnp.einsum('blgc,gcd->blgd', mean - a32, pool_w.astype(F32)).reshape(B, L, POOL_WIDTH)
    return (out * pool_scale).astype(a.dtype)


def na_heads(qkv):
    B, L, _ = qkv.shape
    t = qkv.reshape(B, L, 3, NA_HEADS, NA_HEAD_DIM)
    return t[:, :, 0] * NA_HEAD_DIM ** -0.5, t[:, :, 1], t[:, :, 2]


def neighbourhood_attention(q, k, v, kc, vc, rpb):
    B, L, H, dh = q.shape
    rows = L // GRID_W
    wr = min(NA_WIN_ROWS, rows)
    r = jnp.arange(rows)
    row_idx = jnp.clip(r - wr // 2, 0, rows - wr)[:, None] + jnp.arange(wr)[None, :]
    col = jnp.arange(GRID_W)
    col_start = jnp.clip(col - NA_WIN_COLS // 2, 0, GRID_W - NA_WIN_COLS)
    in_window = (col[None, :] >= col_start[:, None]) & (col[None, :] < col_start[:, None] + NA_WIN_COLS)
    col_off = jnp.clip(col[None, :] - col[:, None], -(NA_WIN_COLS - 1), NA_WIN_COLS - 1) + NA_WIN_COLS - 1
    row_off = row_idx - r[:, None] + NA_WIN_ROWS - 1
    qg = q.reshape(B, rows, GRID_W, H, dh)
    kg = k.reshape(B, rows, GRID_W, H, dh)[:, row_idx]
    vg = v.reshape(B, rows, GRID_W, H, dh)[:, row_idx]
    bias = rpb[:, row_off[:, None, :, None], col_off[None, :, None, :]]
    s_loc = jnp.einsum('brqhd,brwkhd->bhrqwk', qg, kg).astype(F32) + bias.astype(F32)
    s_loc = jnp.where(in_window[:, None, :], s_loc, -jnp.inf)
    s_ctx = jnp.einsum('brqhd,bchd->bhrqc', qg, kc).astype(F32)
    n_loc = wr * GRID_W
    s = jnp.concatenate([s_loc.reshape(B, H, rows, GRID_W, n_loc), s_ctx], axis=-1)
    p = jax.nn.softmax(s, axis=-1).astype(v.dtype)
    p_loc = p[..., :n_loc].reshape(B, H, rows, GRID_W, wr, GRID_W)
    o = jnp.einsum('bhrqwk,brwkhd->brqhd', p_loc, vg) + jnp.einsum('bhrqc,bchd->brqhd', p[..., n_loc:], vc)
    return o.reshape(B, L, H * dh)


def context_attention(qc, kc, vc):
    B, C, H, dh = qc.shape
    p = jax.nn.softmax(jnp.einsum('bqhd,bkhd->bhqk', qc, kc).astype(F32), axis=-1).astype(vc.dtype)
    return jnp.einsum('bhqk,bkhd->bqhd', p, vc).reshape(B, C, H * dh)


def centred_dwconv(x, w):
    K = w.shape[0]
    return lax.conv_general_dilated(x, w[:, None, :].astype(x.dtype), window_strides=(1,),
                                    padding=[(K // 2, K - 1 - K // 2)],
                                    dimension_numbers=('NWC', 'WIO', 'NWC'),
                                    feature_group_count=x.shape[-1])


def axial_rope(t):
    L, dh = t.shape[2], t.shape[3]
    pos = jnp.arange(L)
    half = dh // 2
    nf = half // 2
    inv = ROPE_THETA ** (-jnp.arange(nf, dtype=F32) / nf)

    def rot(part, p):
        ang = p.astype(F32)[:, None] * inv[None, :]
        cos, sin = jnp.cos(ang), jnp.sin(ang)
        x1, x2 = part[..., :nf], part[..., nf:]
        return jnp.concatenate([x1 * cos - x2 * sin, x1 * sin + x2 * cos], axis=-1)

    return jnp.concatenate([rot(t[..., :half], pos // GRID_W), rot(t[..., half:], pos % GRID_W)], axis=-1)


def mlstm_inputs(qkv, gate_pre, conv_w, gate_bias, rotary):
    B, L, _ = qkv.shape
    qk = jax.nn.silu(centred_dwconv(qkv[..., :2 * ML_WIDTH], conv_w))

    def heads(t):
        return t.astype(F32).reshape(B, L, ML_HEADS, ML_HEAD_DIM).transpose(0, 2, 1, 3)

    q = heads(qk[..., :ML_WIDTH])
    k = heads(qk[..., ML_WIDTH:]) * ML_HEAD_DIM ** -0.5
    v = heads(qkv[..., 2 * ML_WIDTH:])
    if rotary:
        q, k = axial_rope(q), axial_rope(k)
    g = (gate_pre.astype(F32) + gate_bias).reshape(B, L, 4, ML_HEADS).transpose(2, 0, 3, 1)
    return (q, k, v, g[0], jax.nn.log_sigmoid(g[1]), g[2], jax.nn.log_sigmoid(g[3]))


def mlstm_zero_state(B):
    return (jnp.zeros((B, ML_HEADS, ML_HEAD_DIM, ML_HEAD_DIM), F32),
            jnp.zeros((B, ML_HEADS, ML_HEAD_DIM), F32),
            jnp.full((B, ML_HEADS), ML_M_INIT, F32))


def mlstm_chunkwise(q, k, v, ig, lf, state):
    B, H, L, d = q.shape
    nc = L // ML_CHUNK

    def to_chunks(a):
        return jnp.moveaxis(a.reshape(B, H, nc, ML_CHUNK, *a.shape[3:]), 2, 0)

    tril = jnp.tril(jnp.ones((ML_CHUNK, ML_CHUNK), bool))

    def step(carry, xs):
        C, n, m = carry
        qb, kb, vb, ib, fb = xs
        b = jnp.cumsum(fb, axis=-1)
        log_d = jnp.where(tril, b[..., :, None] - b[..., None, :] + ib[..., None, :], -jnp.inf)
        inter = b + m[..., None]
        m_t = jnp.maximum(jnp.max(log_d, axis=-1), inter)
        s = jnp.einsum('bhtd,bhsd->bhts', qb, kb) * jnp.exp(log_d - m_t[..., None])
        w_inter = jnp.exp(inter - m_t)
        num = jnp.einsum('bhts,bhse->bhte', s, vb) + w_inter[..., None] * jnp.einsum('bhtd,bhde->bhte', qb, C)
        den = jnp.sum(s, axis=-1) + w_inter * jnp.einsum('bhtd,bhd->bht', qb, n)
        h = num / jnp.maximum(jnp.abs(den), jnp.exp(-m_t))[..., None]
        b_end = b[..., -1]
        log_w = b_end[..., None] - b + ib
        m_new = jnp.maximum(b_end + m, jnp.max(log_w, axis=-1))
        w = jnp.exp(log_w - m_new[..., None])
        decay = jnp.exp(b_end + m - m_new)
        C_new = decay[..., None, None] * C + jnp.einsum('bhs,bhsd,bhse->bhde', w, kb, vb)
        n_new = decay[..., None] * n + jnp.einsum('bhs,bhsd->bhd', w, kb)
        return (C_new, n_new, m_new), h

    state, h = lax.scan(step, state, (to_chunks(q), to_chunks(k), to_chunks(v), to_chunks(ig), to_chunks(lf)))
    return jnp.moveaxis(h, 0, 2).reshape(B, H, L, d), state


def mlstm_bidirectional(lat, ctx_in):
    q, k, v, igf, lff, igb, lfb = lat
    qc, kc, vc, igcf, lfcf, igcb, lfcb = ctx_in
    zero = mlstm_zero_state(q.shape[0])

    def flip(t):
        return jnp.flip(t, axis=2)

    hc_f, st_f = mlstm_chunkwise(qc, kc, vc, igcf, lfcf, zero)
    hc_b, st_b = mlstm_chunkwise(flip(qc), flip(kc), flip(vc), flip(igcb), flip(lfcb), zero)
    h_f, _ = mlstm_chunkwise(q, k, v, igf, lff, st_f)
    h_b, _ = mlstm_chunkwise(flip(q), flip(k), flip(v), flip(igb), flip(lfb), st_b)
    return h_f + flip(h_b), hc_f + flip(hc_b)


def mlstm_output(h, o_pre, norm_w):
    B, H, L, d = h.shape
    hc = h - jnp.mean(h, axis=-1, keepdims=True)
    hn = (hc * lax.rsqrt(jnp.mean(hc * hc, axis=-1, keepdims=True) + LN_EPS)).transpose(0, 2, 1, 3).reshape(B, L, H * d)
    return (jax.nn.sigmoid(o_pre.astype(F32)) * hn * norm_w).astype(o_pre.dtype)


def merge_branches(a, n, m, gate_pre, w_bp, w_bn, w_bm, w_out):
    ga, gn, gm = jnp.split(jax.nn.sigmoid(gate_pre), N_BRANCHES, axis=-1)
    return (ga * (a @ w_bp) + gn * (n @ w_bn) + gm * (m @ w_bm)) @ w_out


def hybrid_mixer(u, uc, w_in, ml_gate_bias, ml_conv_w, ml_norm_w, pool_w, pool_scale, na_rpb,
                 w_bp, w_bn, w_bm, w_out, with_ctx_out):
    pool_in, na_qkv, ml_qkv, ml_o, ml_g, gates = jnp.split(u @ w_in, IN_OFFSETS, axis=-1)
    pool_c, na_qkv_c, ml_qkv_c, ml_o_c, ml_g_c, gates_c = jnp.split(uc @ w_in, IN_OFFSETS, axis=-1)
    a = multiscale_pool(pool_in, pool_w, pool_scale)
    q, k, v = na_heads(na_qkv)
    qc, kc, vc = na_heads(na_qkv_c)
    n = neighbourhood_attention(q, k, v, kc, vc, na_rpb)
    h, hc = mlstm_bidirectional(mlstm_inputs(ml_qkv, ml_g, ml_conv_w, ml_gate_bias, True),
                                mlstm_inputs(ml_qkv_c, ml_g_c, ml_conv_w, ml_gate_bias, False))
    y = merge_branches(a, n, mlstm_output(h, ml_o, ml_norm_w), gates, w_bp, w_bn, w_bm, w_out)
    if not with_ctx_out:
        return y, None
    yc = merge_branches(multiscale_pool(pool_c, pool_w, pool_scale), context_attention(qc, kc, vc),
                        mlstm_output(hc, ml_o_c, ml_norm_w), gates_c, w_bp, w_bn, w_bm, w_out)
    return y, yc


def moe_swiglu(h, router_w, router_b, w1, w3, w2):
    T, D = h.shape
    probs = jax.nn.softmax((h @ router_w).astype(F32) + router_b, axis=-1)
    top_p, top_e = lax.top_k(probs, TOP_K)
    top_p = top_p / jnp.sum(top_p, axis=-1, keepdims=True)
    n_assign = T * TOP_K
    flat_e = top_e.reshape(-1)
    flat_tok = jnp.repeat(jnp.arange(T, dtype=jnp.int32), TOP_K)
    flat_p = top_p.reshape(-1)
    order = jnp.argsort(flat_e)
    se = flat_e[order]
    counts = jnp.bincount(flat_e, length=N_EXPERTS)
    padded = (counts + MOE_BLOCK - 1) // MOE_BLOCK * MOE_BLOCK
    start = jnp.cumsum(counts) - counts
    pstart = jnp.cumsum(padded) - padded
    dest = pstart[se] + jnp.arange(n_assign) - start[se]
    n_blocks = -(-n_assign // MOE_BLOCK) + N_EXPERTS
    n_rows = n_blocks * MOE_BLOCK
    row_tok = jnp.zeros((n_rows,), jnp.int32).at[dest].set(flat_tok[order])
    row_p = jnp.zeros((n_rows,), F32).at[dest].set(flat_p[order])
    block_e = jnp.minimum(jnp.searchsorted(jnp.cumsum(padded), jnp.arange(n_blocks) * MOE_BLOCK, side='right'),
                          N_EXPERTS - 1)
    xb = h[row_tok].reshape(n_blocks, MOE_BLOCK, D)

    def expert_block(args):
        xe, e = args
        return swiglu(xe, w1[e], w3[e], w2[e])

    yb = lax.map(expert_block, (xb, block_e)).reshape(n_rows, D)
    y = jnp.zeros((T, D), F32).at[row_tok].add(yb.astype(F32) * row_p[:, None])
    return y.astype(h.dtype)


def channel_mixer(tokens, layer, ffn_w1, ffn_w3, ffn_w2, moe_router_w, moe_router_b, moe_w1, moe_w3, moe_w2):
    i = layer // 2
    if layer % 2 == 0:
        return swiglu(tokens, ffn_w1[i], ffn_w3[i], ffn_w2[i])
    return moe_swiglu(tokens, moe_router_w[i], moe_router_b[i], moe_w1[i], moe_w3[i], moe_w2[i])


def setup_inputs(seed: int = 0) -> dict:
    key = jax.random.key(seed)
    ks = jax.random.split(key, 32)
    D = D_MODEL
    n_dense = (DEPTH + 1) // 2
    n_moe = DEPTH // 2

    def nrm(k, shape, s):
        return jax.random.normal(k, shape, F32) * s

    gate_base = jnp.concatenate([jnp.zeros(ML_HEADS), jnp.linspace(3.0, 6.0, ML_HEADS)] * 2)
    return {
        'x': nrm(ks[0], (BATCH, SEQ, D), 1.0),
        'c': nrm(ks[1], (BATCH, D), 1.0),
        'ctx': nrm(ks[2], (BATCH, CTX_LEN, D), 1.0),
        'c_ctx': nrm(ks[3], (D,), 1.0),
        'w_mod': nrm(ks[4], (DEPTH, D, 6 * D), D ** -0.5),
        'b_mod': nrm(ks[5], (DEPTH, 6 * D), 0.02),
        'w_in': nrm(ks[6], (DEPTH, D, N_IN), D ** -0.5),
        'ml_gate_bias': gate_base + nrm(ks[7], (DEPTH, 4 * ML_HEADS), 0.1),
        'ml_conv_w': nrm(ks[8], (DEPTH, ML_CONV, 2 * ML_WIDTH), ML_CONV ** -0.5),
        'ml_norm_w': 1.0 + nrm(ks[9], (DEPTH, ML_WIDTH), 0.1),
        'pool_w': nrm(ks[10], (DEPTH, POOL_GROUPS, POOL_GROUP_DIM, POOL_GROUP_DIM), POOL_GROUP_DIM ** -0.5),
        'pool_scale': 1.0 + nrm(ks[11], (DEPTH, POOL_WIDTH), 0.1),
        'na_rpb': nrm(ks[12], (DEPTH, NA_HEADS, 2 * NA_WIN_ROWS - 1, 2 * NA_WIN_COLS - 1), 0.1),
        'w_branch_pool': nrm(ks[13], (DEPTH, POOL_WIDTH, D), POOL_WIDTH ** -0.5),
        'w_branch_na': nrm(ks[14], (DEPTH, NA_WIDTH, D), NA_WIDTH ** -0.5),
        'w_branch_ml': nrm(ks[15], (DEPTH, ML_WIDTH, D), ML_WIDTH ** -0.5),
        'w_out': nrm(ks[16], (DEPTH, D, D), BETA * D ** -0.5),
        'ln1_g': 1.0 + nrm(ks[17], (DEPTH, D), 0.1),
        'ln1_b': nrm(ks[18], (DEPTH, D), 0.02),
        'ln2_g': 1.0 + nrm(ks[19], (DEPTH, D), 0.1),
        'ln2_b': nrm(ks[20], (DEPTH, D), 0.02),
        'ffn_w1': nrm(ks[21], (n_dense, D, D_FF), D ** -0.5),
        'ffn_w3': nrm(ks[22], (n_dense, D, D_FF), D ** -0.5),
        'ffn_w2': nrm(ks[23], (n_dense, D_FF, D), BETA * D_FF ** -0.5),
        'moe_router_w': nrm(ks[24], (n_moe, D, N_EXPERTS), D ** -0.5),
        'moe_router_b': nrm(ks[25], (n_moe, N_EXPERTS), 0.01),
        'moe_w1': nrm(ks[26], (n_moe, N_EXPERTS, D, D_FF), D ** -0.5),
        'moe_w3': nrm(ks[27], (n_moe, N_EXPERTS, D, D_FF), D ** -0.5),
        'moe_w2': nrm(ks[28], (n_moe, N_EXPERTS, D_FF, D), BETA * D_FF ** -0.5),
    }


def reference(x, c, ctx, c_ctx, w_mod, b_mod, w_in, ml_gate_bias, ml_conv_w, ml_norm_w, pool_w, pool_scale,
              na_rpb, w_branch_pool, w_branch_na, w_branch_ml, w_out, ln1_g, ln1_b, ln2_g, ln2_b,
              ffn_w1, ffn_w3, ffn_w2, moe_router_w, moe_router_b, moe_w1, moe_w3, moe_w2):
    B, L, D = x.shape
    s_c = jax.nn.silu(c)
    s_cc = jax.nn.silu(c_ctx)
    xc = ctx
    for layer in range(DEPTH):
        last = layer == DEPTH - 1
        sh1, sc1, g1, sh2, sc2, g2 = jnp.split((s_c @ w_mod[layer] + b_mod[layer])[:, None, :], 6, axis=-1)
        csh1, csc1, cg1, csh2, csc2, cg2 = jnp.split(s_cc @ w_mod[layer] + b_mod[layer], 6, axis=-1)
        u = modulate(ln_plain(x), sh1, sc1)
        uc = modulate(ln_plain(xc), csh1, csc1)
        y, yc = hybrid_mixer(u, uc, w_in[layer], ml_gate_bias[layer], ml_conv_w[layer], ml_norm_w[layer],
                             pool_w[layer], pool_scale[layer], na_rpb[layer], w_branch_pool[layer],
                             w_branch_na[layer], w_branch_ml[layer], w_out[layer], not last)
        x = ln_affine(ALPHA * x + g1 * y, ln1_g[layer], ln1_b[layer])
        tokens = modulate(ln_plain(x), sh2, sc2).reshape(B * L, D)
        if not last:
            xc = ln_affine(ALPHA * xc + cg1 * yc, ln1_g[layer], ln1_b[layer])
            tokens = jnp.concatenate([tokens, modulate(ln_plain(xc), csh2, csc2).reshape(-1, D)], axis=0)
        f = channel_mixer(tokens, layer, ffn_w1, ffn_w3, ffn_w2, moe_router_w, moe_router_b, moe_w1, moe_w3, moe_w2)
        x = ln_affine(ALPHA * x + g2 * f[:B * L].reshape(B, L, D), ln2_g[layer], ln2_b[layer])
        if not last:
            xc = ln_affine(ALPHA * xc + cg2 * f[B * L:].reshape(xc.shape), ln2_g[layer], ln2_b[layer])
    return x
```

```python
import functools

import numpy as np
import jax
import jax.numpy as jnp
from jax import lax
from jax.experimental import pallas as pl
from jax.experimental.pallas import tpu as pltpu

F32 = jnp.float32
BF16 = jnp.bfloat16
HIGHEST = lax.Precision.HIGHEST

GRID_W = 64
POOL_GROUPS = 4
POOL_GROUP_DIM = 64
POOL_WIDTH = POOL_GROUPS * POOL_GROUP_DIM
POOL_HALF_WINDOWS = (1, 2, 4, 8)
NA_HEADS = 4
NA_HEAD_DIM = 64
NA_WIDTH = NA_HEADS * NA_HEAD_DIM
NA_WIN_ROWS = 8
NA_WIN_COLS = 16
ML_HEADS = 4
ML_HEAD_DIM = 128
ML_WIDTH = ML_HEADS * ML_HEAD_DIM
ML_CHUNK = 128
ML_M_INIT = -1e30
ROPE_THETA = 10000.0
N_EXPERTS = 8
TOP_K = 2
N_BRANCHES = 3
LN_EPS = 1e-6

LANES = 128
MOD_ROWS = 16
VMEM_LIMIT = 56 * 1024 * 1024

TM_TOKEN = 768
TM_MERGE = 384
TF_FFN = 512
MOE_ROWS = 512


def _cparams(sem):
    return pltpu.CompilerParams(dimension_semantics=sem, vmem_limit_bytes=VMEM_LIMIT)


def _sigmoid(x):
    return 1.0 / (1.0 + jnp.exp(-x))


def _silu(x):
    return x * _sigmoid(x)


def _ln_plain(x):
    xc = x - jnp.mean(x, axis=-1, keepdims=True)
    return xc * lax.rsqrt(jnp.mean(xc * xc, axis=-1, keepdims=True) + LN_EPS)


def _is_ctx_rows(tile_idx, tm, n_lat):
    row = tile_idx * tm + lax.broadcasted_iota(jnp.int32, (tm, 1), 0)
    return row >= n_lat


def _tile_mod(lat_ref, ctx_ref, is_ctx):
    return jnp.where(is_ctx, ctx_ref[0], lat_ref[0])


def _mod_body(s_ref, w_ref, b_ref, o_ref):
    s = _silu(s_ref[...])
    o_ref[0] = jnp.dot(s, w_ref[0], precision=HIGHEST, preferred_element_type=F32) + b_ref[0]


def _mod_vectors(c, c_ctx, w_mod, b_mod):
    depth, d, d6 = w_mod.shape
    b = c.shape[0]
    s = jnp.zeros((MOD_ROWS, d), F32).at[:b].set(c).at[b].set(c_ctx)
    tn = 1536
    return pl.pallas_call(
        _mod_body,
        grid=(depth, d6 // tn),
        in_specs=[pl.BlockSpec((MOD_ROWS, d), lambda l, j: (0, 0)),
                  pl.BlockSpec((1, d, tn), lambda l, j: (l, 0, j)),
                  pl.BlockSpec((1, 1, tn), lambda l, j: (l, 0, j))],
        out_specs=pl.BlockSpec((1, MOD_ROWS, tn), lambda l, j: (l, 0, j)),
        out_shape=jax.ShapeDtypeStruct((depth, MOD_ROWS, d6), F32),
        compiler_params=_cparams(("arbitrary", "arbitrary")),
        name="mod_vectors",
    )(s, w_mod, b_mod.reshape(depth, 1, d6))


def _ln_mod_body(x_ref, shl_ref, shc_ref, scl_ref, scc_ref, u_ref, *, n_lat):
    tm = x_ref.shape[1]
    is_ctx = _is_ctx_rows(pl.program_id(1), tm, n_lat)
    sh = _tile_mod(shl_ref, shc_ref, is_ctx)
    sc = _tile_mod(scl_ref, scc_ref, is_ctx)
    u_ref[0] = (_ln_plain(x_ref[0]) * (1.0 + sc) + sh).astype(u_ref.dtype)


def _row_specs(d, nb):
    return [pl.BlockSpec((1, 1, d), lambda b, j: (b, 0, 0)),
            pl.BlockSpec((1, 1, d), lambda b, j: (nb, 0, 0))]


def _ln_mod(x_all, shift, scale, n_lat):
    b, s, d = x_all.shape
    tm = TM_TOKEN
    return pl.pallas_call(
        functools.partial(_ln_mod_body, n_lat=n_lat),
        grid=(b, s // tm),
        in_specs=[pl.BlockSpec((1, tm, d), lambda i, j: (i, j, 0))] + _row_specs(d, b) + _row_specs(d, b),
        out_specs=pl.BlockSpec((1, tm, d), lambda i, j: (i, j, 0)),
        out_shape=jax.ShapeDtypeStruct((b, s, d), BF16),
        compiler_params=_cparams(("arbitrary", "arbitrary")),
        name="ln_mod",
    )(x_all, shift, shift, scale, scale)


def _proj_main_body(u_ref, w_ref, wg_ref, pool_ref, na_ref, mlqk_ref, mlv_ref, mlo_ref, mlg_ref):
    u = u_ref[0]

    def seg(lo, hi):
        return jnp.dot(u, w_ref[:, lo:hi], preferred_element_type=F32)

    off_na = POOL_WIDTH
    off_ml = off_na + 3 * NA_WIDTH
    off_mlv = off_ml + 2 * ML_WIDTH
    off_mlo = off_ml + 3 * ML_WIDTH
    pool_ref[0] = seg(0, off_na)
    na_ref[0, :, :NA_WIDTH] = (seg(off_na, off_na + NA_WIDTH) * NA_HEAD_DIM ** -0.5).astype(na_ref.dtype)
    na_ref[0, :, NA_WIDTH:] = seg(off_na + NA_WIDTH, off_ml).astype(na_ref.dtype)
    mlqk_ref[0] = seg(off_ml, off_mlv)
    mlv_ref[0] = seg(off_mlv, off_mlo).astype(mlv_ref.dtype)
    mlo_ref[0] = seg(off_mlo, off_mlo + ML_WIDTH)
    mlg_ref[0] = jnp.dot(u, wg_ref[...], preferred_element_type=F32)


def _proj_main(u, w_main, w_g):
    b, s, d = u.shape
    tm = TM_TOKEN
    widths = (POOL_WIDTH, 3 * NA_WIDTH, 2 * ML_WIDTH, ML_WIDTH, ML_WIDTH, LANES)
    dtypes = (F32, BF16, F32, BF16, F32, F32)
    return pl.pallas_call(
        _proj_main_body,
        grid=(b, s // tm),
        in_specs=[pl.BlockSpec((1, tm, d), lambda i, j: (i, j, 0)),
                  pl.BlockSpec(w_main.shape, lambda i, j: (0, 0)),
                  pl.BlockSpec(w_g.shape, lambda i, j: (0, 0))],
        out_specs=[pl.BlockSpec((1, tm, w), lambda i, j: (i, j, 0)) for w in widths],
        out_shape=[jax.ShapeDtypeStruct((b, s, w), dt) for w, dt in zip(widths, dtypes)],
        compiler_params=_cparams(("arbitrary", "arbitrary")),
        name="proj_main",
    )(u, w_main, w_g)


def _proj_gate_body(u_ref, w_ref, g_ref):
    u = u_ref[0]
    n = w_ref.shape[1]
    step = 1024
    for lo in range(0, n, step):
        g_ref[0, :, lo:lo + step] = jnp.dot(u, w_ref[:, lo:lo + step], preferred_element_type=F32)


def _proj_gate(u, w_gate):
    b, s, d = u.shape
    tm = TM_TOKEN
    n = w_gate.shape[1]
    return pl.pallas_call(
        _proj_gate_body,
        grid=(b, s // tm),
        in_specs=[pl.BlockSpec((1, tm, d), lambda i, j: (i, j, 0)),
                  pl.BlockSpec(w_gate.shape, lambda i, j: (0, 0))],
        out_specs=pl.BlockSpec((1, tm, n), lambda i, j: (i, j, 0)),
        out_shape=jax.ShapeDtypeStruct((b, s, n), F32),
        compiler_params=_cparams(("arbitrary", "arbitrary")),
        name="proj_gate",
    )(u, w_gate)


def _pool_segment(a, w_bd, scale):
    n = a.shape[0]
    t = lax.broadcasted_iota(jnp.int32, a.shape, 0)
    grp = lax.broadcasted_iota(jnp.int32, a.shape, 1) // POOL_GROUP_DIM

    def up(x, k):
        return jnp.where(t < n - k, pltpu.roll(x, n - k, 0), 0.0)

    def down(x, k):
        return jnp.where(t >= k, pltpu.roll(x, k, 0), 0.0)

    ahead = a
    behind = down(a, 1)
    wsum = jnp.zeros_like(a)
    half = jnp.zeros_like(t)
    for g, h in enumerate(POOL_HALF_WINDOWS):
        if g > 0:
            ahead = ahead + up(ahead, h // 2)
            behind = behind + down(behind, h // 2)
        wsum = jnp.where(grp == g, ahead + behind, wsum)
        half = jnp.where(grp == g, h, half)
    cnt = jnp.minimum(t + half, n) - jnp.maximum(t - half, 0)
    mean = wsum / cnt.astype(F32)
    out = jnp.dot((mean - a).astype(BF16), w_bd, preferred_element_type=F32)
    return out * scale


def _pool_body(a_ref, w_ref, sc_ref, o_ref, *, n_lat):
    w_bd = w_ref[...]
    scale = sc_ref[...]
    s = a_ref.shape[1]
    o_ref[0, :n_lat] = _pool_segment(a_ref[0, :n_lat], w_bd, scale).astype(o_ref.dtype)
    o_ref[0, n_lat:] = _pool_segment(a_ref[0, n_lat:s], w_bd, scale).astype(o_ref.dtype)


def _pool(pool_in, w_bd, scale, n_lat):
    b, s, w = pool_in.shape
    return pl.pallas_call(
        functools.partial(_pool_body, n_lat=n_lat),
        grid=(b,),
        in_specs=[pl.BlockSpec((1, s, w), lambda i: (i, 0, 0)),
                  pl.BlockSpec((w, w), lambda i: (0, 0)),
                  pl.BlockSpec((1, w), lambda i: (0, 0))],
        out_specs=pl.BlockSpec((1, s, w), lambda i: (i, 0, 0)),
        out_shape=jax.ShapeDtypeStruct((b, s, w), BF16),
        compiler_params=_cparams(("arbitrary",)),
        name="pool",
    )(pool_in, w_bd, scale)


def _na_body(q_ref, k_ref, v_ref, bias_ref, o_ref, *, n_lat, n_rows):
    r = pl.program_id(1)
    q = q_ref[0]
    head = lax.broadcasted_iota(jnp.int32, q.shape, 1) // NA_HEAD_DIM
    zero = jnp.zeros_like(q)
    q4 = jnp.concatenate([jnp.where(head == h, q, zero) for h in range(NA_HEADS)], axis=0)
    s_tot = k_ref.shape[1]
    kc = k_ref[0, n_lat:s_tot, :]
    vc = v_ref[0, n_lat:s_tot, :]
    nt = (((1,), (1,)), ((), ()))
    s_ctx = lax.dot_general(q4, kc, nt, preferred_element_type=F32)

    def finish(o4, denom):
        o4 = o4 / denom
        out = jnp.zeros((GRID_W, NA_WIDTH), F32)
        for h in range(NA_HEADS):
            out = out + jnp.where(head == h, o4[h * GRID_W:(h + 1) * GRID_W], 0.0)
        o_ref[0] = out.astype(o_ref.dtype)

    @pl.when(r < n_rows)
    def _():
        wr = NA_WIN_ROWS
        rs = jnp.clip(r - wr // 2, 0, n_rows - wr)
        start = pl.multiple_of(rs * GRID_W, GRID_W)
        kw = k_ref[0, pl.ds(start, wr * GRID_W), :]
        vw = v_ref[0, pl.ds(start, wr * GRID_W), :]
        s_loc = lax.dot_general(q4, kw, nt, preferred_element_type=F32) + bias_ref[0]
        m = jnp.maximum(jnp.max(s_loc, axis=-1, keepdims=True), jnp.max(s_ctx, axis=-1, keepdims=True))
        p_loc = jnp.exp(s_loc - m)
        p_ctx = jnp.exp(s_ctx - m)
        denom = jnp.sum(p_loc, axis=-1, keepdims=True) + jnp.sum(p_ctx, axis=-1, keepdims=True)
        o4 = (jnp.dot(p_loc.astype(BF16), vw, preferred_element_type=F32)
              + jnp.dot(p_ctx.astype(BF16), vc, preferred_element_type=F32))
        finish(o4, denom)

    @pl.when(r >= n_rows)
    def _():
        m = jnp.max(s_ctx, axis=-1, keepdims=True)
        p_ctx = jnp.exp(s_ctx - m)
        denom = jnp.sum(p_ctx, axis=-1, keepdims=True)
        finish(jnp.dot(p_ctx.astype(BF16), vc, preferred_element_type=F32), denom)


def _na_bias_classes(rpb):
    col = np.arange(GRID_W)
    col_start = np.clip(col - NA_WIN_COLS // 2, 0, GRID_W - NA_WIN_COLS)
    in_window = (col[None, :] >= col_start[:, None]) & (col[None, :] < col_start[:, None] + NA_WIN_COLS)
    col_off = np.clip(col[None, :] - col[:, None], -(NA_WIN_COLS - 1), NA_WIN_COLS - 1) + NA_WIN_COLS - 1
    row_off = np.arange(NA_WIN_ROWS)[:, None] + np.arange(NA_WIN_ROWS)[None, :]
    b = rpb[:, row_off[:, :, None, None], col_off[None, None, :, :]]
    b = jnp.where(in_window[None, None, None], b, -jnp.inf)
    b = b.transpose(1, 0, 3, 2, 4)
    return b.reshape(NA_WIN_ROWS, NA_HEADS * GRID_W, NA_WIN_ROWS * GRID_W).astype(F32)


def _na(na_qkv, bias_cls, n_lat):
    b, s, _ = na_qkv.shape
    n_rows = n_lat // GRID_W
    n_q = s // GRID_W
    wr = NA_WIN_ROWS

    def bias_idx(i, r):
        cls = jnp.where(r < wr // 2, wr - 1 - r,
                        jnp.where(r > n_rows - wr // 2, n_rows - 1 - r, wr // 2 - 1))
        return (jnp.clip(cls, 0, wr - 1), 0, 0)

    return pl.pallas_call(
        functools.partial(_na_body, n_lat=n_lat, n_rows=n_rows),
        grid=(b, n_q),
        in_specs=[pl.BlockSpec((1, GRID_W, NA_WIDTH), lambda i, r: (i, r, 0)),
                  pl.BlockSpec((1, s, NA_WIDTH), lambda i, r: (i, 0, 1)),
                  pl.BlockSpec((1, s, NA_WIDTH), lambda i, r: (i, 0, 2)),
                  pl.BlockSpec((1,) + bias_cls.shape[1:], bias_idx)],
        out_specs=pl.BlockSpec((1, GRID_W, NA_WIDTH), lambda i, r: (i, r, 0)),
        out_shape=jax.ShapeDtypeStruct((b, s, NA_WIDTH), BF16),
        compiler_params=_cparams(("arbitrary", "arbitrary")),
        name="nbr_attention",
    )(na_qkv, na_qkv, na_qkv, bias_cls)


def _rope_tables(n_lat):
    half = ML_HEAD_DIM // 2
    nf = half // 2
    inv = ROPE_THETA ** (-np.arange(nf, dtype=np.float64) / nf)
    pos = np.arange(n_lat)
    ang = np.concatenate([(pos // GRID_W)[:, None] * inv[None, :]] * 2
                         + [(pos % GRID_W)[:, None] * inv[None, :]] * 2, axis=1)
    lane = np.arange(ML_HEAD_DIM)
    second = (lane % half) >= nf
    cos = np.cos(ang)
    sin = np.sin(ang)
    sin_from_lo = np.where(second[None, :], sin, 0.0)
    sin_from_hi = np.where(second[None, :], 0.0, -sin)
    return (jnp.asarray(cos, F32), jnp.asarray(sin_from_lo, F32), jnp.asarray(sin_from_hi, F32))


def _conv_silu(x, w):
    n = x.shape[0]
    t = lax.broadcasted_iota(jnp.int32, x.shape, 0)
    prev = jnp.where(t >= 1, pltpu.roll(x, 1, 0), 0.0)
    nxt = jnp.where(t < n - 1, pltpu.roll(x, n - 1, 0), 0.0)
    return _silu(w[0:1] * prev + w[1:2] * x + w[2:3] * nxt)


def _rope(x, cos, sin_lo, sin_hi):
    nf = ML_HEAD_DIM // 4
    return x * cos + pltpu.roll(x, nf, 1) * sin_lo + pltpu.roll(x, ML_HEAD_DIM - nf, 1) * sin_hi


def _ml_prep_body(xq_ref, xk_ref, wq_ref, wk_ref, cos_ref, slo_ref, shi_ref, q_ref, kt_ref, *, n_lat):
    s = xq_ref.shape[1]
    wq = wq_ref[...]
    wk = wk_ref[...]
    cos, slo, shi = cos_ref[...], slo_ref[...], shi_ref[...]
    kscale = ML_HEAD_DIM ** -0.5
    for lo, hi, rotary in ((0, n_lat, True), (n_lat, s, False)):
        q = _conv_silu(xq_ref[0, lo:hi], wq)
        k = _conv_silu(xk_ref[0, lo:hi], wk) * kscale
        if rotary:
            q = _rope(q, cos, slo, shi)
            k = _rope(k, cos, slo, shi)
        q_ref[0, lo:hi] = q.astype(q_ref.dtype)
        for j in range((hi - lo) // ML_CHUNK):
            kt_ref[0, 0, lo // ML_CHUNK + j] = k[j * ML_CHUNK:(j + 1) * ML_CHUNK].T.astype(kt_ref.dtype)


def _ml_prep(mlqk, conv_w, rope, n_lat):
    b, s, _ = mlqk.shape
    hd = ML_HEAD_DIM
    nc = s // ML_CHUNK
    tab = pl.BlockSpec((n_lat, hd), lambda i, h: (0, 0))
    return pl.pallas_call(
        functools.partial(_ml_prep_body, n_lat=n_lat),
        grid=(b, ML_HEADS),
        in_specs=[pl.BlockSpec((1, s, hd), lambda i, h: (i, 0, h)),
                  pl.BlockSpec((1, s, hd), lambda i, h: (i, 0, ML_HEADS + h)),
                  pl.BlockSpec((3, hd), lambda i, h: (0, h)),
                  pl.BlockSpec((3, hd), lambda i, h: (0, ML_HEADS + h)),
                  tab, tab, tab],
        out_specs=[pl.BlockSpec((1, s, hd), lambda i, h: (i, 0, h)),
                   pl.BlockSpec((1, 1, nc, hd, ML_CHUNK), lambda i, h: (i, h, 0, 0, 0))],
        out_shape=[jax.ShapeDtypeStruct((b, s, ML_WIDTH), BF16),
                   jax.ShapeDtypeStruct((b, ML_HEADS, nc, hd, ML_CHUNK), BF16)],
        compiler_params=_cparams(("arbitrary", "arbitrary")),
        name="mlstm_prep",
    )(mlqk, mlqk, conv_w, conv_w, *rope)


def _log_sigmoid(x):
    return jnp.minimum(x, 0.0) - jnp.log1p(jnp.exp(-jnp.abs(x)))


def _ml_gates_body(gc_ref, gr_ref, bc_ref, br_ref, oc_ref, or_ref):
    nc = gr_ref.shape[1]
    c = ML_CHUNK
    ii = lax.broadcasted_iota(jnp.int32, (c, c), 0)
    jj = lax.broadcasted_iota(jnp.int32, (c, c), 1)
    lower = (ii >= jj).astype(F32)
    upper = (ii <= jj).astype(F32)
    lane_kind = lax.broadcasted_iota(jnp.int32, (c, LANES), 1) // ML_HEADS
    row_kind = lax.broadcasted_iota(jnp.int32, (4 * ML_HEADS, c), 0) // ML_HEADS
    for j in range(nc):
        g = gc_ref[0, j * c:(j + 1) * c, :] + bc_ref[...]
        lf = _log_sigmoid(g)
        pre = jnp.dot(lower, lf, precision=HIGHEST, preferred_element_type=F32)
        suf = jnp.dot(upper, lf, precision=HIGHEST, preferred_element_type=F32)
        oc_ref[0, j * c:(j + 1) * c, :] = jnp.where(lane_kind == 1, pre, jnp.where(lane_kind == 3, suf, g))
        gr = gr_ref[0, j] + br_ref[...]
        lfr = _log_sigmoid(gr)
        pre_r = jnp.dot(lfr, upper, precision=HIGHEST, preferred_element_type=F32)
        suf_r = jnp.dot(lfr, lower, precision=HIGHEST, preferred_element_type=F32)
        or_ref[0, j] = jnp.where(row_kind == 1, pre_r, jnp.where(row_kind == 3, suf_r, gr))


def _ml_gates(mlg, gate_bias):
    b, s, _ = mlg.shape
    nc = s // ML_CHUNK
    ng = 4 * ML_HEADS
    g_rows = mlg[:, :, :ng].reshape(b, nc, ML_CHUNK, ng).transpose(0, 1, 3, 2)
    bias_c = jnp.zeros((1, LANES), F32).at[0, :ng].set(gate_bias)
    bias_r = gate_bias.reshape(ng, 1)
    return pl.pallas_call(
        _ml_gates_body,
        grid=(b,),
        in_specs=[pl.BlockSpec((1, s, LANES), lambda i: (i, 0, 0)),
                  pl.BlockSpec((1, nc, ng, ML_CHUNK), lambda i: (i, 0, 0, 0)),
                  pl.BlockSpec((1, LANES), lambda i: (0, 0)),
                  pl.BlockSpec((ng, 1), lambda i: (0, 0))],
        out_specs=[pl.BlockSpec((1, s, LANES), lambda i: (i, 0, 0)),
                   pl.BlockSpec((1, nc, ng, ML_CHUNK), lambda i: (i, 0, 0, 0))],
        out_shape=[jax.ShapeDtypeStruct((b, s, LANES), F32),
                   jax.ShapeDtypeStruct((b, nc, ng, ML_CHUNK), F32)],
        compiler_params=_cparams(("arbitrary",)),
        name="mlstm_gates",
    )(mlg, g_rows, bias_c, bias_r)


def _ml_step(q, kt, v_aug, ig_r, b_c, b_r, state, m, *, forward):
    c = ML_CHUNK
    d = ML_HEAD_DIM
    ti = lax.broadcasted_iota(jnp.int32, (c, c), 0)
    si = lax.broadcasted_iota(jnp.int32, (c, c), 1)
    allowed = (si <= ti) if forward else (si >= ti)
    a_r = ig_r - b_r
    log_d = jnp.where(allowed, b_c + a_r, -jnp.inf)
    inter = b_c + m
    m_t = jnp.maximum(jnp.max(log_d, axis=-1, keepdims=True), inter)
    qk = jnp.dot(q, kt, preferred_element_type=F32)
    s_mat = (qk * jnp.exp(log_d - m_t)).astype(BF16)
    w_inter = jnp.exp(inter - m_t)
    tot = (jnp.dot(s_mat, v_aug, preferred_element_type=F32)
           + w_inter * jnp.dot(q, state.astype(BF16), preferred_element_type=F32))
    num = tot[:, :d]
    den = tot[:, d:d + 1]
    h = num / jnp.maximum(jnp.abs(den), jnp.exp(-m_t))
    b_end = b_r[:, c - 1:c] if forward else b_r[:, 0:1]
    log_w = b_end + a_r
    m_new = jnp.maximum(b_end + m, jnp.max(log_w, axis=-1, keepdims=True))
    w_r = jnp.exp(log_w - m_new)
    decay = jnp.exp(b_end + m - m_new)
    kw = (kt.astype(F32) * w_r).astype(BF16)
    state_new = decay * state + jnp.dot(kw, v_aug, preferred_element_type=F32)
    return h, state_new, m_new


def _ml_scan_body(q_ref, kt_ref, v_ref, gc_ref, gr_ref, o_ref, nw_ref, out_ref, hsum_ref, *, n_lat):
    c = ML_CHUNK
    d = ML_HEAD_DIM
    s = q_ref.shape[1]
    nc_lat = n_lat // c
    nc_all = s // c
    ones_col = (lax.broadcasted_iota(jnp.int32, (c, d), 1) == 0).astype(BF16)
    nw = nw_ref[...]

    def run(j, state, m, forward):
        row0 = j * c if isinstance(j, int) else pl.multiple_of(j * c, c)
        q = q_ref[0, pl.ds(row0, c), :]
        v_aug = jnp.concatenate([v_ref[0, pl.ds(row0, c), :], ones_col], axis=1)
        gc = gc_ref[0, 0, pl.ds(row0, c), :]
        gr = gr_ref[0, 0, j]
        k0 = 0 if forward else 2
        return _ml_step(q, kt_ref[0, 0, j], v_aug, gr[k0:k0 + 1], gc[:, k0 + 1:k0 + 2], gr[k0 + 1:k0 + 2],
                        state, m, forward=forward)

    def finalize(j, h):
        row0 = j * c if isinstance(j, int) else pl.multiple_of(j * c, c)
        hc = h - jnp.mean(h, axis=-1, keepdims=True)
        hn = hc * lax.rsqrt(jnp.mean(hc * hc, axis=-1, keepdims=True) + LN_EPS)
        o = o_ref[0, pl.ds(row0, c), :]
        out_ref[0, pl.ds(row0, c), :] = (_sigmoid(o) * hn * nw).astype(out_ref.dtype)

    zero_state = jnp.zeros((d, 2 * d), F32)
    m0 = jnp.full((1, 1), ML_M_INIT, F32)
    sf, mf = zero_state, m0
    hc_f = []
    for j in range(nc_lat, nc_all):
        h, sf, mf = run(j, sf, mf, True)
        hc_f.append(h)
    sb, mb = zero_state, m0
    hc_b = {}
    for j in range(nc_all - 1, nc_lat - 1, -1):
        h, sb, mb = run(j, sb, mb, False)
        hc_b[j] = h
    for idx, j in enumerate(range(nc_lat, nc_all)):
        finalize(j, hc_f[idx] + hc_b[j])

    half = nc_lat // 2

    def first_half(i, carry):
        sf, mf, sb, mb = carry
        jb = nc_lat - 1 - i
        hf, sf, mf = run(i, sf, mf, True)
        hb, sb, mb = run(jb, sb, mb, False)
        hsum_ref[pl.ds(pl.multiple_of(i * c, c), c), :] = hf
        hsum_ref[pl.ds(pl.multiple_of(jb * c, c), c), :] = hb
        return sf, mf, sb, mb

    carry = lax.fori_loop(0, half, first_half, (sf, mf, sb, mb))

    def second_half(i, carry):
        sf, mf, sb, mb = carry
        jb = nc_lat - 1 - i
        hf, sf, mf = run(i, sf, mf, True)
        hb, sb, mb = run(jb, sb, mb, False)
        finalize(i, hf + hsum_ref[pl.ds(pl.multiple_of(i * c, c), c), :])
        finalize(jb, hb + hsum_ref[pl.ds(pl.multiple_of(jb * c, c), c), :])
        return sf, mf, sb, mb

    lax.fori_loop(half, nc_lat, second_half, carry)


def _ml_scan(q, kt, mlv, gcol, grow, mlo, norm_w, n_lat):
    b, s, _ = q.shape
    hd = ML_HEAD_DIM
    nc = s // ML_CHUNK
    ng = 4 * ML_HEADS
    gc_h = gcol[:, :, :ng].reshape(b, s, 4, ML_HEADS).transpose(0, 3, 1, 2)
    gr_h = grow.reshape(b, nc, 4, ML_HEADS, ML_CHUNK).transpose(0, 3, 1, 2, 4)
    tok = pl.BlockSpec((1, s, hd), lambda i, h: (i, 0, h))
    return pl.pallas_call(
        functools.partial(_ml_scan_body, n_lat=n_lat),
        grid=(b, ML_HEADS),
        in_specs=[tok,
                  pl.BlockSpec((1, 1, nc, hd, ML_CHUNK), lambda i, h: (i, h, 0, 0, 0)),
                  tok,
                  pl.BlockSpec((1, 1, s, 4), lambda i, h: (i, h, 0, 0)),
                  pl.BlockSpec((1, 1, nc, 4, ML_CHUNK), lambda i, h: (i, h, 0, 0, 0)),
                  tok,
                  pl.BlockSpec((1, hd), lambda i, h: (0, h))],
        out_specs=tok,
        out_shape=jax.ShapeDtypeStruct((b, s, ML_WIDTH), BF16),
        scratch_shapes=[pltpu.VMEM((n_lat, hd), F32)],
        compiler_params=_cparams(("arbitrary", "arbitrary")),
        name="mlstm_scan",
    )(q, kt, mlv, gc_h, gr_h, mlo, norm_w)


def _top2(logits, n_valid):
    lane = lax.broadcasted_iota(jnp.int32, logits.shape, 1).astype(F32)
    neg = jnp.float32(-jnp.inf)
    l = jnp.where(lane < n_valid, logits, neg)
    m1 = jnp.max(l, axis=-1, keepdims=True)
    e1 = jnp.min(jnp.where(l == m1, lane, float(LANES)), axis=-1, keepdims=True)
    l2 = jnp.where(lane == e1, neg, l)
    m2 = jnp.max(l2, axis=-1, keepdims=True)
    e2 = jnp.min(jnp.where(l2 == m2, lane, float(LANES)), axis=-1, keepdims=True)
    x2 = jnp.exp(m2 - m1)
    p1 = 1.0 / (1.0 + x2)
    p2 = x2 / (1.0 + x2)
    return jnp.where(lane == 0, e1, jnp.where(lane == 1, e2, jnp.where(lane == 2, p1, jnp.where(lane == 3, p2, 0.0))))


def _merge_body(*refs, n_lat, alpha, moe):
    (a_ref, n_ref, m_ref, g_ref, x_ref, wbp_ref, wbn_ref, wbm_ref, wo_ref,
     g1l_ref, g1c_ref, shl_ref, shc_ref, scl_ref, scc_ref, lng_ref, lnb_ref) = refs[:17]
    if moe:
        rw_ref, rb_ref, x_out, tok_out, route_out = refs[17:]
    else:
        x_out, tok_out = refs[17:]
    tm = x_ref.shape[1]
    d = x_ref.shape[2]
    is_ctx = _is_ctx_rows(pl.program_id(1), tm, n_lat)
    g = g_ref[0]
    y = (_sigmoid(g[:, :d]) * jnp.dot(a_ref[0], wbp_ref[...], preferred_element_type=F32)
         + _sigmoid(g[:, d:2 * d]) * jnp.dot(n_ref[0], wbn_ref[...], preferred_element_type=F32)
         + _sigmoid(g[:, 2 * d:]) * jnp.dot(m_ref[0], wbm_ref[...], preferred_element_type=F32))
    y = jnp.dot(y.astype(BF16), wo_ref[...], preferred_element_type=F32)
    z = alpha * x_ref[0] + _tile_mod(g1l_ref, g1c_ref, is_ctx) * y
    xn = _ln_plain(z) * lng_ref[...] + lnb_ref[...]
    x_out[0] = xn
    tok = _ln_plain(xn) * (1.0 + _tile_mod(scl_ref, scc_ref, is_ctx)) + _tile_mod(shl_ref, shc_ref, is_ctx)
    tok_out[0] = tok.astype(tok_out.dtype)
    if moe:
        logits = jnp.dot(tok, rw_ref[...], precision=HIGHEST, preferred_element_type=F32) + rb_ref[...]
        route_out[0] = _top2(logits, N_EXPERTS)


def _merge(a, n, m, gates, x_all, w_bp, w_bn, w_bm, w_o, g1, sh2, sc2, ln_g, ln_b, n_lat, alpha, router=None):
    b, s, d = x_all.shape
    tm = TM_MERGE
    moe = router is not None

    def tok(w):
        return pl.BlockSpec((1, tm, w), lambda i, j: (i, j, 0))

    def full(arr):
        return pl.BlockSpec(arr.shape, lambda i, j: (0,) * arr.ndim)

    in_specs = ([tok(a.shape[2]), tok(n.shape[2]), tok(m.shape[2]), tok(gates.shape[2]), tok(d),
                 full(w_bp), full(w_bn), full(w_bm), full(w_o)]
                + _row_specs(d, b) * 3 + [full(ln_g), full(ln_b)])
    args = [a, n, m, gates, x_all, w_bp, w_bn, w_bm, w_o, g1, g1, sh2, sh2, sc2, sc2, ln_g, ln_b]
    out_specs = [tok(d), tok(d)]
    out_shape = [jax.ShapeDtypeStruct((b, s, d), F32), jax.ShapeDtypeStruct((b, s, d), F32 if moe else BF16)]
    if moe:
        in_specs += [full(router[0]), full(router[1])]
        args += list(router)
        out_specs.append(tok(LANES))
        out_shape.append(jax.ShapeDtypeStruct((b, s, LANES), F32))
    return pl.pallas_call(
        functools.partial(_merge_body, n_lat=n_lat, alpha=alpha, moe=moe),
        grid=(b, s // tm),
        in_specs=in_specs,
        out_specs=out_specs,
        out_shape=out_shape,
        compiler_params=_cparams(("arbitrary", "arbitrary")),
        name="merge",
    )(*args)


def _finish_rows(f, x, g2, lng, lnb, alpha):
    return _ln_plain(alpha * x + g2 * f) * lng + lnb


def _ffn_body(*refs, n_lat, alpha, with_next):
    (t_ref, x_ref, w1_ref, w3_ref, w2_ref, g2l_ref, g2c_ref, lng_ref, lnb_ref) = refs[:9]
    if with_next:
        shl_ref, shc_ref, scl_ref, scc_ref, x_out, u_out, acc_ref = refs[9:]
    else:
        x_out, acc_ref = refs[9:]
    f = pl.program_id(2)
    t = t_ref[0]
    h1 = jnp.dot(t, w1_ref[...], preferred_element_type=F32)
    h3 = jnp.dot(t, w3_ref[...], preferred_element_type=F32)
    contrib = jnp.dot((_silu(h1) * h3).astype(BF16), w2_ref[...], preferred_element_type=F32)

    @pl.when(f == 0)
    def _():
        acc_ref[...] = contrib

    @pl.when(f > 0)
    def _():
        acc_ref[...] += contrib

    @pl.when(f == pl.num_programs(2) - 1)
    def _():
        tm = x_ref.shape[1]
        is_ctx = _is_ctx_rows(pl.program_id(1), tm, n_lat)
        xn = _finish_rows(acc_ref[...], x_ref[0], _tile_mod(g2l_ref, g2c_ref, is_ctx), lng_ref[...], lnb_ref[...], alpha)
        x_out[0] = xn
        if with_next:
            u = _ln_plain(xn) * (1.0 + _tile_mod(scl_ref, scc_ref, is_ctx)) + _tile_mod(shl_ref, shc_ref, is_ctx)
            u_out[0] = u.astype(u_out.dtype)


def _ffn(tok, x_all, w1, w3, w2, g2, ln_g, ln_b, n_lat, alpha, nxt=None):
    b, s, d = x_all.shape
    dff = w1.shape[1]
    tm, tf = TM_TOKEN, TF_FFN
    with_next = nxt is not None

    def tokspec():
        return pl.BlockSpec((1, tm, d), lambda i, j, f: (i, j, 0))

    def rows():
        return [pl.BlockSpec((1, 1, d), lambda i, j, f: (i, 0, 0)),
                pl.BlockSpec((1, 1, d), lambda i, j, f: (b, 0, 0))]

    vec = pl.BlockSpec((1, d), lambda i, j, f: (0, 0))
    in_specs = [tokspec(), tokspec(),
                pl.BlockSpec((d, tf), lambda i, j, f: (0, f)),
                pl.BlockSpec((d, tf), lambda i, j, f: (0, f)),
                pl.BlockSpec((tf, d), lambda i, j, f: (f, 0))] + rows() + [vec, vec]
    args = [tok, x_all, w1, w3, w2, g2, g2, ln_g, ln_b]
    out_specs = [tokspec()]
    out_shape = [jax.ShapeDtypeStruct((b, s, d), F32)]
    if with_next:
        in_specs += rows() + rows()
        args += [nxt[0], nxt[0], nxt[1], nxt[1]]
        out_specs.append(tokspec())
        out_shape.append(jax.ShapeDtypeStruct((b, s, d), BF16))
    res = pl.pallas_call(
        functools.partial(_ffn_body, n_lat=n_lat, alpha=alpha, with_next=with_next),
        grid=(b, s // tm, dff // tf),
        in_specs=in_specs,
        out_specs=out_specs,
        out_shape=out_shape,
        scratch_shapes=[pltpu.VMEM((tm, d), F32)],
        compiler_params=_cparams(("arbitrary", "arbitrary", "arbitrary")),
        name="ffn_dense",
    )(*args)
    return res if with_next else (res[0], None)


def _moe_plan(top_e, tok_id, n_slots_per_tok):
    r = MOE_ROWS
    flat_e = top_e.reshape(-1)
    n_assign = flat_e.shape[0]
    onehot = (flat_e[:, None] == jnp.arange(N_EXPERTS, dtype=jnp.int32)[None, :]).astype(jnp.int32)
    cum = jnp.cumsum(onehot, axis=0)
    rank = jnp.take_along_axis(cum, flat_e[:, None], axis=1)[:, 0] - 1
    counts = cum[-1]
    nblk = (counts + r - 1) // r
    bend = jnp.cumsum(nblk)
    bstart = bend - nblk
    dest = bstart[flat_e] * r + rank
    n_blocks = -(-n_assign // r) + N_EXPERTS
    n_rows = n_blocks * r
    row_tok = jnp.zeros((n_rows,), jnp.int32).at[dest].set(jnp.repeat(tok_id, n_slots_per_tok))
    row_slot = jnp.zeros((n_rows,), jnp.int32).at[dest].set(jnp.arange(n_assign, dtype=jnp.int32))
    blk = jnp.arange(n_blocks, dtype=jnp.int32)
    total = bend[-1]
    e_of = jnp.searchsorted(bend, blk, side='right').astype(jnp.int32)
    e_last = jnp.searchsorted(bend, total - 1, side='right').astype(jnp.int32)
    block_e = jnp.minimum(e_of, e_last)
    nvalid = jnp.where(blk < total, jnp.clip(counts[block_e] - (blk - bstart[block_e]) * r, 0, r), 0).astype(jnp.int32)
    return (block_e, nvalid, row_tok.reshape(n_blocks, 1, r), row_slot.reshape(n_blocks, 1, r))


def _moe_body(be_ref, nv_ref, rtok_ref, rslot_ref, tok_hbm, w1_ref, w3_ref, w2_ref, y_hbm,
              xbuf, acc_ref, gsem, ssem):
    i = pl.program_id(0)
    f = pl.program_id(1)
    nf = pl.num_programs(1)
    nv = nv_ref[i]

    @pl.when((i == 0) & (f == 0))
    def _():
        xbuf[...] = jnp.zeros_like(xbuf)

    def gather_copy(r):
        return pltpu.make_async_copy(tok_hbm.at[pl.ds(rtok_ref[0, 0, r], 1)], xbuf.at[pl.ds(r, 1)], gsem)

    def scatter_copy(r):
        return pltpu.make_async_copy(acc_ref.at[pl.ds(r, 1)], y_hbm.at[pl.ds(rslot_ref[0, 0, r], 1)], ssem)

    @pl.when((f == 0) & (nv > 0))
    def _():
        def issue(r, c):
            gather_copy(r).start()
            return c

        lax.fori_loop(0, nv, issue, 0)

        def wait(r, c):
            gather_copy(r).wait()
            return c

        lax.fori_loop(0, nv, wait, 0)

    @pl.when(nv > 0)
    def _():
        x = xbuf[...].astype(BF16)
        h1 = jnp.dot(x, w1_ref[0], preferred_element_type=F32)
        h3 = jnp.dot(x, w3_ref[0], preferred_element_type=F32)
        contrib = jnp.dot((_silu(h1) * h3).astype(BF16), w2_ref[0], preferred_element_type=F32)

        @pl.when(f == 0)
        def _():
            acc_ref[...] = contrib

        @pl.when(f > 0)
        def _():
            acc_ref[...] += contrib

    @pl.when((f == nf - 1) & (nv > 0))
    def _():
        def issue(r, c):
            scatter_copy(r).start()
            return c

        lax.fori_loop(0, nv, issue, 0)

        def wait(r, c):
            scatter_copy(r).wait()
            return c

        lax.fori_loop(0, nv, wait, 0)


def _moe_experts(tok_flat, plan, w1, w3, w2, n_slots):
    block_e, nvalid, row_tok, row_slot = plan
    n_blocks = block_e.shape[0]
    d = tok_flat.shape[1]
    dff = w1.shape[2]
    r, tf = MOE_ROWS, TF_FFN
    nf = dff // tf

    def f_eff(i, f, nv):
        return jnp.where(nv[i] > 0, f, nf - 1)

    grid_spec = pltpu.PrefetchScalarGridSpec(
        num_scalar_prefetch=2,
        grid=(n_blocks, nf),
        in_specs=[pl.BlockSpec((1, 1, r), lambda i, f, be, nv: (i, 0, 0), memory_space=pltpu.SMEM),
                  pl.BlockSpec((1, 1, r), lambda i, f, be, nv: (i, 0, 0), memory_space=pltpu.SMEM),
                  pl.BlockSpec(memory_space=pl.ANY),
                  pl.BlockSpec((1, d, tf), lambda i, f, be, nv: (be[i], 0, f_eff(i, f, nv))),
                  pl.BlockSpec((1, d, tf), lambda i, f, be, nv: (be[i], 0, f_eff(i, f, nv))),
                  pl.BlockSpec((1, tf, d), lambda i, f, be, nv: (be[i], f_eff(i, f, nv), 0))],
        out_specs=pl.BlockSpec(memory_space=pl.ANY),
        scratch_shapes=[pltpu.VMEM((r, d), F32), pltpu.VMEM((r, d), F32),
                        pltpu.SemaphoreType.DMA(()), pltpu.SemaphoreType.DMA(())],
    )
    return pl.pallas_call(
        _moe_body,
        grid_spec=grid_spec,
        out_shape=jax.ShapeDtypeStruct((n_slots, d), F32),
        compiler_params=_cparams(("arbitrary", "arbitrary")),
        name="moe_experts",
    )(block_e, nvalid, row_tok, row_slot, tok_flat, w1, w3, w2)


def _combine_body(*refs, n_lat, alpha, with_next):
    (y_ref, p_ref, x_ref, g2l_ref, g2c_ref, lng_ref, lnb_ref) = refs[:7]
    if with_next:
        shl_ref, shc_ref, scl_ref, scc_ref, x_out, u_out = refs[7:]
    else:
        (x_out,) = refs[7:]
    tm = x_ref.shape[1]
    d = x_ref.shape[2]
    is_ctx = _is_ctx_rows(pl.program_id(1), tm, n_lat)
    p = p_ref[0]
    y = y_ref[0]
    f = p[:, 2:3] * y[:, :d] + p[:, 3:4] * y[:, d:]
    xn = _finish_rows(f, x_ref[0], _tile_mod(g2l_ref, g2c_ref, is_ctx), lng_ref[...], lnb_ref[...], alpha)
    x_out[0] = xn
    if with_next:
        u = _ln_plain(xn) * (1.0 + _tile_mod(scl_ref, scc_ref, is_ctx)) + _tile_mod(shl_ref, shc_ref, is_ctx)
        u_out[0] = u.astype(u_out.dtype)


def _combine(y2, route, x_all, g2, ln_g, ln_b, n_lat, alpha, n_tok_rows, nxt=None):
    b, s, d = x_all.shape
    tm = 512 if n_tok_rows % 512 == 0 else TM_MERGE
    with_next = nxt is not None

    def tok(w):
        return pl.BlockSpec((1, tm, w), lambda i, j: (i, j, 0))

    vec = pl.BlockSpec((1, d), lambda i, j: (0, 0))
    in_specs = [tok(2 * d), tok(LANES), tok(d)] + _row_specs(d, b) + [vec, vec]
    args = [y2, route, x_all, g2, g2, ln_g, ln_b]
    out_specs = [tok(d)]
    out_shape = [jax.ShapeDtypeStruct((b, n_tok_rows, d), F32)]
    if with_next:
        in_specs += _row_specs(d, b) * 2
        args += [nxt[0], nxt[0], nxt[1], nxt[1]]
        out_specs.append(tok(d))
        out_shape.append(jax.ShapeDtypeStruct((b, n_tok_rows, d), BF16))
    res = pl.pallas_call(
        functools.partial(_combine_body, n_lat=n_lat, alpha=alpha, with_next=with_next),
        grid=(b, n_tok_rows // tm),
        in_specs=in_specs,
        out_specs=out_specs,
        out_shape=out_shape,
        compiler_params=_cparams(("arbitrary", "arbitrary")),
        name="moe_combine",
    )(*args)
    return res if with_next else (res[0], None)


def _block_diag(pool_w):
    g, c, _ = pool_w.shape
    out = jnp.zeros((g * c, g * c), pool_w.dtype)
    for i in range(g):
        out = out.at[i * c:(i + 1) * c, i * c:(i + 1) * c].set(pool_w[i])
    return out


def kernel(x, c, ctx, c_ctx, w_mod, b_mod, w_in, ml_gate_bias, ml_conv_w, ml_norm_w, pool_w, pool_scale,
           na_rpb, w_branch_pool, w_branch_na, w_branch_ml, w_out, ln1_g, ln1_b, ln2_g, ln2_b,
           ffn_w1, ffn_w3, ffn_w2, moe_router_w, moe_router_b, moe_w1, moe_w3, moe_w2):
    b, n_lat, d = x.shape
    n_ctx = ctx.shape[1]
    s = n_lat + n_ctx
    depth = w_in.shape[0]
    alpha = (2 * depth) ** 0.25
    off_g = POOL_WIDTH + 3 * NA_WIDTH + 4 * ML_WIDTH
    n_gate_cols = 4 * ML_HEADS

    x_all = jnp.concatenate([x, ctx], axis=1)
    mod = _mod_vectors(c, c_ctx, w_mod, b_mod)

    def mod_part(layer, k):
        return mod[layer, :, k * d:(k + 1) * d].reshape(MOD_ROWS, 1, d)

    rope = _rope_tables(n_lat)
    u = _ln_mod(x_all, mod_part(0, 0), mod_part(0, 1), n_lat)

    for layer in range(depth):
        last = layer == depth - 1
        w_l = w_in[layer]
        w_main = w_l[:, :off_g].astype(BF16)
        w_g = jnp.zeros((d, LANES), BF16).at[:, :n_gate_cols].set(w_l[:, off_g:off_g + n_gate_cols].astype(BF16))
        w_gate = w_l[:, off_g + n_gate_cols:].astype(BF16)
        pool_in, na_qkv, mlqk, mlv, mlo, mlg = _proj_main(u, w_main, w_g)
        gates = _proj_gate(u, w_gate)

        a = _pool(pool_in, _block_diag(pool_w[layer]).astype(BF16), pool_scale[layer].reshape(1, -1), n_lat)
        n = _na(na_qkv, _na_bias_classes(na_rpb[layer]), n_lat)
        q_ml, kt_ml = _ml_prep(mlqk, ml_conv_w[layer], rope, n_lat)
        gcol, grow = _ml_gates(mlg, ml_gate_bias[layer])
        m = _ml_scan(q_ml, kt_ml, mlv, gcol, grow, mlo, ml_norm_w[layer].reshape(1, -1), n_lat)

        is_moe = layer % 2 == 1
        i = layer // 2
        router = None
        if is_moe:
            rw = jnp.zeros((d, LANES), F32).at[:, :N_EXPERTS].set(moe_router_w[i])
            rb = jnp.zeros((1, LANES), F32).at[0, :N_EXPERTS].set(moe_router_b[i])
            router = (rw, rb)
        merged = _merge(a, n, m, gates, x_all,
                        w_branch_pool[layer].astype(BF16), w_branch_na[layer].astype(BF16),
                        w_branch_ml[layer].astype(BF16), w_out[layer].astype(BF16),
                        mod_part(layer, 2), mod_part(layer, 3), mod_part(layer, 4),
                        ln1_g[layer].reshape(1, d), ln1_b[layer].reshape(1, d), n_lat, alpha, router)
        nxt = None if last else (mod_part(layer + 1, 0), mod_part(layer + 1, 1))
        g2 = mod_part(layer, 5)
        lng, lnb = ln2_g[layer].reshape(1, d), ln2_b[layer].reshape(1, d)
        if not is_moe:
            x_mid, tok = merged
            x_all, u = _ffn(tok, x_mid, ffn_w1[i].astype(BF16), ffn_w3[i].astype(BF16), ffn_w2[i].astype(BF16),
                            g2, lng, lnb, n_lat, alpha, nxt)
        else:
            x_mid, tok, route = merged
            n_rows_tok = n_lat if last else s
            top_e = route[:, :n_rows_tok, :TOP_K].astype(jnp.int32).reshape(b * n_rows_tok, TOP_K)
            tok_id = (jnp.arange(b, dtype=jnp.int32)[:, None] * s
                      + jnp.arange(n_rows_tok, dtype=jnp.int32)[None, :]).reshape(-1)
            plan = _moe_plan(top_e, tok_id, TOP_K)
            y2 = _moe_experts(tok.reshape(b * s, d), plan, moe_w1[i].astype(BF16), moe_w3[i].astype(BF16),
                              moe_w2[i].astype(BF16), b * n_rows_tok * TOP_K)
            x_all, u = _combine(y2.reshape(b, n_rows_tok, TOP_K * d), route, x_mid, g2, lng, lnb,
                                n_lat, alpha, n_rows_tok, nxt)
    return x_all[:, :n_lat]
```

```python
import functools

import numpy as np
import jax
import jax.numpy as jnp
from jax import lax
from jax.experimental import pallas as pl
from jax.experimental.pallas import tpu as pltpu

F32 = jnp.float32
BF16 = jnp.bfloat16
HIGHEST = lax.Precision.HIGHEST

GRID_W = 64
POOL_GROUPS = 4
POOL_GROUP_DIM = 64
POOL_WIDTH = POOL_GROUPS * POOL_GROUP_DIM
POOL_HALF_WINDOWS = (1, 2, 4, 8)
NA_HEADS = 4
NA_HEAD_DIM = 64
NA_WIDTH = NA_HEADS * NA_HEAD_DIM
NA_WIN_ROWS = 8
NA_WIN_COLS = 16
ML_HEADS = 4
ML_HEAD_DIM = 128
ML_WIDTH = ML_HEADS * ML_HEAD_DIM
ML_CHUNK = 128
ML_M_INIT = -1e30
ROPE_THETA = 10000.0
N_EXPERTS = 8
TOP_K = 2
N_BRANCHES = 3
LN_EPS = 1e-6

LANES = 128
MOD_ROWS = 16
VMEM_LIMIT = 56 * 1024 * 1024

TM_TOKEN = 768
TM_MERGE = 384
TF_FFN = 512
MOE_ROWS = 512


def _cparams(sem):
    return pltpu.CompilerParams(dimension_semantics=sem, vmem_limit_bytes=VMEM_LIMIT)


def _sigmoid(x):
    return 1.0 / (1.0 + jnp.exp(-x))


def _silu(x):
    return x * _sigmoid(x)


def _ln_plain(x):
    xc = x - jnp.mean(x, axis=-1, keepdims=True)
    return xc * lax.rsqrt(jnp.mean(xc * xc, axis=-1, keepdims=True) + LN_EPS)


def _is_ctx_rows(tile_idx, tm, n_lat):
    row = tile_idx * tm + lax.broadcasted_iota(jnp.int32, (tm, 1), 0)
    return row >= n_lat


def _tile_mod(lat_ref, ctx_ref, is_ctx):
    return jnp.where(is_ctx, ctx_ref[0], lat_ref[0])


def _mod_body(s_ref, w_ref, b_ref, o_ref):
    s = _silu(s_ref[...])
    o_ref[0] = jnp.dot(s, w_ref[0], precision=HIGHEST, preferred_element_type=F32) + b_ref[0]


def _mod_vectors(c, c_ctx, w_mod, b_mod):
    depth, d, d6 = w_mod.shape
    b = c.shape[0]
    s = jnp.zeros((MOD_ROWS, d), F32).at[:b].set(c).at[b].set(c_ctx)
    tn = 1536
    return pl.pallas_call(
        _mod_body,
        grid=(depth, d6 // tn),
        in_specs=[pl.BlockSpec((MOD_ROWS, d), lambda l, j: (0, 0)),
                  pl.BlockSpec((1, d, tn), lambda l, j: (l, 0, j)),
                  pl.BlockSpec((1, 1, tn), lambda l, j: (l, 0, j))],
        out_specs=pl.BlockSpec((1, MOD_ROWS, tn), lambda l, j: (l, 0, j)),
        out_shape=jax.ShapeDtypeStruct((depth, MOD_ROWS, d6), F32),
        compiler_params=_cparams(("arbitrary", "arbitrary")),
        name="mod_vectors",
    )(s, w_mod, b_mod.reshape(depth, 1, d6))


def _ln_mod_body(x_ref, shl_ref, shc_ref, scl_ref, scc_ref, u_ref, *, n_lat):
    tm = x_ref.shape[1]
    is_ctx = _is_ctx_rows(pl.program_id(1), tm, n_lat)
    sh = _tile_mod(shl_ref, shc_ref, is_ctx)
    sc = _tile_mod(scl_ref, scc_ref, is_ctx)
    u_ref[0] = (_ln_plain(x_ref[0]) * (1.0 + sc) + sh).astype(u_ref.dtype)


def _row_specs(d, nb):
    return [pl.BlockSpec((1, 1, d), lambda b, j: (b, 0, 0)),
            pl.BlockSpec((1, 1, d), lambda b, j: (nb, 0, 0))]


def _ln_mod(x_all, shift, scale, n_lat):
    b, s, d = x_all.shape
    tm = TM_TOKEN
    return pl.pallas_call(
        functools.partial(_ln_mod_body, n_lat=n_lat),
        grid=(b, s // tm),
        in_specs=[pl.BlockSpec((1, tm, d), lambda i, j: (i, j, 0))] + _row_specs(d, b) + _row_specs(d, b),
        out_specs=pl.BlockSpec((1, tm, d), lambda i, j: (i, j, 0)),
        out_shape=jax.ShapeDtypeStruct((b, s, d), BF16),
        compiler_params=_cparams(("arbitrary", "arbitrary")),
        name="ln_mod",
    )(x_all, shift, shift, scale, scale)


def _proj_main_body(u_ref, w_ref, wg_ref, pool_ref, na_ref, mlqk_ref, mlv_ref, mlo_ref, mlg_ref):
    u = u_ref[0]

    def seg(lo, hi):
        return jnp.dot(u, w_ref[:, lo:hi], preferred_element_type=F32)

    off_na = POOL_WIDTH
    off_ml = off_na + 3 * NA_WIDTH
    off_mlv = off_ml + 2 * ML_WIDTH
    off_mlo = off_ml + 3 * ML_WIDTH
    pool_ref[0] = seg(0, off_na)
    na_ref[0, :, :NA_WIDTH] = (seg(off_na, off_na + NA_WIDTH) * NA_HEAD_DIM ** -0.5).astype(na_ref.dtype)
    na_ref[0, :, NA_WIDTH:] = seg(off_na + NA_WIDTH, off_ml).astype(na_ref.dtype)
    mlqk_ref[0] = seg(off_ml, off_mlv)
    mlv_ref[0] = seg(off_mlv, off_mlo).astype(mlv_ref.dtype)
    mlo_ref[0] = seg(off_mlo, off_mlo + ML_WIDTH)
    mlg_ref[0] = jnp.dot(u, wg_ref[...], preferred_element_type=F32)


def _proj_main(u, w_main, w_g):
    b, s, d = u.shape
    tm = TM_TOKEN
    widths = (POOL_WIDTH, 3 * NA_WIDTH, 2 * ML_WIDTH, ML_WIDTH, ML_WIDTH, LANES)
    dtypes = (F32, BF16, F32, BF16, F32, F32)
    return pl.pallas_call(
        _proj_main_body,
        grid=(b, s // tm),
        in_specs=[pl.BlockSpec((1, tm, d), lambda i, j: (i, j, 0)),
                  pl.BlockSpec(w_main.shape, lambda i, j: (0, 0)),
                  pl.BlockSpec(w_g.shape, lambda i, j: (0, 0))],
        out_specs=[pl.BlockSpec((1, tm, w), lambda i, j: (i, j, 0)) for w in widths],
        out_shape=[jax.ShapeDtypeStruct((b, s, w), dt) for w, dt in zip(widths, dtypes)],
        compiler_params=_cparams(("arbitrary", "arbitrary")),
        name="proj_main",
    )(u, w_main, w_g)


def _proj_gate_body(u_ref, w_ref, g_ref):
    u = u_ref[0]
    n = w_ref.shape[1]
    step = 1024
    for lo in range(0, n, step):
        g_ref[0, :, lo:lo + step] = jnp.dot(u, w_ref[:, lo:lo + step], preferred_element_type=F32)


def _proj_gate(u, w_gate):
    b, s, d = u.shape
    tm = TM_TOKEN
    n = w_gate.shape[1]
    return pl.pallas_call(
        _proj_gate_body,
        grid=(b, s // tm),
        in_specs=[pl.BlockSpec((1, tm, d), lambda i, j: (i, j, 0)),
                  pl.BlockSpec(w_gate.shape, lambda i, j: (0, 0))],
        out_specs=pl.BlockSpec((1, tm, n), lambda i, j: (i, j, 0)),
        out_shape=jax.ShapeDtypeStruct((b, s, n), F32),
        compiler_params=_cparams(("arbitrary", "arbitrary")),
        name="proj_gate",
    )(u, w_gate)


def _pool_segment(a, w_bd, scale):
    n = a.shape[0]
    t = lax.broadcasted_iota(jnp.int32, a.shape, 0)
    grp = lax.broadcasted_iota(jnp.int32, a.shape, 1) // POOL_GROUP_DIM

    def up(x, k):
        return jnp.where(t < n - k, pltpu.roll(x, n - k, 0), 0.0)

    def down(x, k):
        return jnp.where(t >= k, pltpu.roll(x, k, 0), 0.0)

    ahead = a
    behind = down(a, 1)
    wsum = jnp.zeros_like(a)
    half = jnp.zeros_like(t)
    for g, h in enumerate(POOL_HALF_WINDOWS):
        if g > 0:
            ahead = ahead + up(ahead, h // 2)
            behind = behind + down(behind, h // 2)
        wsum = jnp.where(grp == g, ahead + behind, wsum)
        half = jnp.where(grp == g, h, half)
    cnt = jnp.minimum(t + half, n) - jnp.maximum(t - half, 0)
    mean = wsum / cnt.astype(F32)
    out = jnp.dot((mean - a).astype(BF16), w_bd, preferred_element_type=F32)
    return out * scale


def _pool_body(a_ref, w_ref, sc_ref, o_ref, *, n_lat):
    w_bd = w_ref[...]
    scale = sc_ref[...]
    s = a_ref.shape[1]
    o_ref[0, :n_lat] = _pool_segment(a_ref[0, :n_lat], w_bd, scale).astype(o_ref.dtype)
    o_ref[0, n_lat:] = _pool_segment(a_ref[0, n_lat:s], w_bd, scale).astype(o_ref.dtype)


def _pool(pool_in, w_bd, scale, n_lat):
    b, s, w = pool_in.shape
    return pl.pallas_call(
        functools.partial(_pool_body, n_lat=n_lat),
        grid=(b,),
        in_specs=[pl.BlockSpec((1, s, w), lambda i: (i, 0, 0)),
                  pl.BlockSpec((w, w), lambda i: (0, 0)),
                  pl.BlockSpec((1, w), lambda i: (0, 0))],
        out_specs=pl.BlockSpec((1, s, w), lambda i: (i, 0, 0)),
        out_shape=jax.ShapeDtypeStruct((b, s, w), BF16),
        compiler_params=_cparams(("arbitrary",)),
        name="pool",
    )(pool_in, w_bd, scale)


def _na_body(q_ref, k_ref, v_ref, bias_ref, o_ref, *, n_lat, n_rows):
    r = pl.program_id(1)
    q = q_ref[0]
    head = lax.broadcasted_iota(jnp.int32, q.shape, 1) // NA_HEAD_DIM
    zero = jnp.zeros_like(q)
    q4 = jnp.concatenate([jnp.where(head == h, q, zero) for h in range(NA_HEADS)], axis=0)
    s_tot = k_ref.shape[1]
    kc = k_ref[0, n_lat:s_tot, :]
    vc = v_ref[0, n_lat:s_tot, :]
    nt = (((1,), (1,)), ((), ()))
    s_ctx = lax.dot_general(q4, kc, nt, preferred_element_type=F32)

    def finish(o4, denom):
        o4 = o4 / denom
        out = jnp.zeros((GRID_W, NA_WIDTH), F32)
        for h in range(NA_HEADS):
            out = out + jnp.where(head == h, o4[h * GRID_W:(h + 1) * GRID_W], 0.0)
        o_ref[0] = out.astype(o_ref.dtype)

    @pl.when(r < n_rows)
    def _():
        wr = NA_WIN_ROWS
        rs = jnp.clip(r - wr // 2, 0, n_rows - wr)
        start = pl.multiple_of(rs * GRID_W, GRID_W)
        kw = k_ref[0, pl.ds(start, wr * GRID_W), :]
        vw = v_ref[0, pl.ds(start, wr * GRID_W), :]
        s_loc = lax.dot_general(q4, kw, nt, preferred_element_type=F32) + bias_ref[0]
        m = jnp.maximum(jnp.max(s_loc, axis=-1, keepdims=True), jnp.max(s_ctx, axis=-1, keepdims=True))
        p_loc = jnp.exp(s_loc - m)
        p_ctx = jnp.exp(s_ctx - m)
        denom = jnp.sum(p_loc, axis=-1, keepdims=True) + jnp.sum(p_ctx, axis=-1, keepdims=True)
        o4 = (jnp.dot(p_loc.astype(BF16), vw, preferred_element_type=F32)
              + jnp.dot(p_ctx.astype(BF16), vc, preferred_element_type=F32))
        finish(o4, denom)

    @pl.when(r >= n_rows)
    def _():
        m = jnp.max(s_ctx, axis=-1, keepdims=True)
        p_ctx = jnp.exp(s_ctx - m)
        denom = jnp.sum(p_ctx, axis=-1, keepdims=True)
        finish(jnp.dot(p_ctx.astype(BF16), vc, preferred_element_type=F32), denom)


def _na_bias_classes(rpb):
    col = np.arange(GRID_W)
    col_start = np.clip(col - NA_WIN_COLS // 2, 0, GRID_W - NA_WIN_COLS)
    in_window = (col[None, :] >= col_start[:, None]) & (col[None, :] < col_start[:, None] + NA_WIN_COLS)
    col_off = np.clip(col[None, :] - col[:, None], -(NA_WIN_COLS - 1), NA_WIN_COLS - 1) + NA_WIN_COLS - 1
    n_off = 2 * NA_WIN_COLS - 1
    pick = (col_off[None, :, :] == np.arange(n_off)[:, None, None]).astype(np.float32)
    by_col = jnp.einsum('hro,oqk->hrqk', rpb.astype(F32), jnp.asarray(pick), precision=HIGHEST)
    by_col = jnp.where(in_window[None, None], by_col, -jnp.inf)
    b = jnp.stack([by_col[:, o:o + NA_WIN_ROWS] for o in range(NA_WIN_ROWS)], axis=0)
    b = b.transpose(0, 1, 3, 2, 4)
    return b.reshape(NA_WIN_ROWS, NA_HEADS * GRID_W, NA_WIN_ROWS * GRID_W)


def _na(na_qkv, bias_cls, n_lat):
    b, s, _ = na_qkv.shape
    n_rows = n_lat // GRID_W
    n_q = s // GRID_W
    wr = NA_WIN_ROWS

    def bias_idx(i, r):
        cls = jnp.where(r < wr // 2, wr - 1 - r,
                        jnp.where(r > n_rows - wr // 2, n_rows - 1 - r, wr // 2 - 1))
        return (jnp.clip(cls, 0, wr - 1), 0, 0)

    return pl.pallas_call(
        functools.partial(_na_body, n_lat=n_lat, n_rows=n_rows),
        grid=(b, n_q),
        in_specs=[pl.BlockSpec((1, GRID_W, NA_WIDTH), lambda i, r: (i, r, 0)),
                  pl.BlockSpec((1, s, NA_WIDTH), lambda i, r: (i, 0, 1)),
                  pl.BlockSpec((1, s, NA_WIDTH), lambda i, r: (i, 0, 2)),
                  pl.BlockSpec((1,) + bias_cls.shape[1:], bias_idx)],
        out_specs=pl.BlockSpec((1, GRID_W, NA_WIDTH), lambda i, r: (i, r, 0)),
        out_shape=jax.ShapeDtypeStruct((b, s, NA_WIDTH), BF16),
        compiler_params=_cparams(("arbitrary", "arbitrary")),
        name="nbr_attention",
    )(na_qkv, na_qkv, na_qkv, bias_cls)


def _rope_tables(n_lat):
    half = ML_HEAD_DIM // 2
    nf = half // 2
    inv = ROPE_THETA ** (-np.arange(nf, dtype=np.float64) / nf)
    pos = np.arange(n_lat)
    ang = np.concatenate([(pos // GRID_W)[:, None] * inv[None, :]] * 2
                         + [(pos % GRID_W)[:, None] * inv[None, :]] * 2, axis=1)
    lane = np.arange(ML_HEAD_DIM)
    second = (lane % half) >= nf
    cos = np.cos(ang)
    sin = np.sin(ang)
    sin_from_lo = np.where(second[None, :], sin, 0.0)
    sin_from_hi = np.where(second[None, :], 0.0, -sin)
    return (jnp.asarray(cos, F32), jnp.asarray(sin_from_lo, F32), jnp.asarray(sin_from_hi, F32))


def _conv_silu(x, w):
    n = x.shape[0]
    t = lax.broadcasted_iota(jnp.int32, x.shape, 0)
    prev = jnp.where(t >= 1, pltpu.roll(x, 1, 0), 0.0)
    nxt = jnp.where(t < n - 1, pltpu.roll(x, n - 1, 0), 0.0)
    return _silu(w[0:1] * prev + w[1:2] * x + w[2:3] * nxt)


def _rope(x, cos, sin_lo, sin_hi):
    nf = ML_HEAD_DIM // 4
    return x * cos + pltpu.roll(x, nf, 1) * sin_lo + pltpu.roll(x, ML_HEAD_DIM - nf, 1) * sin_hi


def _ml_prep_body(xq_ref, xk_ref, wq_ref, wk_ref, cos_ref, slo_ref, shi_ref, q_ref, kt_ref, *, n_lat):
    s = xq_ref.shape[1]
    wq = wq_ref[...]
    wk = wk_ref[...]
    cos, slo, shi = cos_ref[...], slo_ref[...], shi_ref[...]
    kscale = ML_HEAD_DIM ** -0.5
    for lo, hi, rotary in ((0, n_lat, True), (n_lat, s, False)):
        q = _conv_silu(xq_ref[0, lo:hi], wq)
        k = _conv_silu(xk_ref[0, lo:hi], wk) * kscale
        if rotary:
            q = _rope(q, cos, slo, shi)
            k = _rope(k, cos, slo, shi)
        q_ref[0, lo:hi] = q.astype(q_ref.dtype)
        for j in range((hi - lo) // ML_CHUNK):
            kt_ref[0, 0, lo // ML_CHUNK + j] = k[j * ML_CHUNK:(j + 1) * ML_CHUNK].T.astype(kt_ref.dtype)


def _ml_prep(mlqk, conv_w, rope, n_lat):
    b, s, _ = mlqk.shape
    hd = ML_HEAD_DIM
    nc = s // ML_CHUNK
    tab = pl.BlockSpec((n_lat, hd), lambda i, h: (0, 0))
    return pl.pallas_call(
        functools.partial(_ml_prep_body, n_lat=n_lat),
        grid=(b, ML_HEADS),
        in_specs=[pl.BlockSpec((1, s, hd), lambda i, h: (i, 0, h)),
                  pl.BlockSpec((1, s, hd), lambda i, h: (i, 0, ML_HEADS + h)),
                  pl.BlockSpec((3, hd), lambda i, h: (0, h)),
                  pl.BlockSpec((3, hd), lambda i, h: (0, ML_HEADS + h)),
                  tab, tab, tab],
        out_specs=[pl.BlockSpec((1, s, hd), lambda i, h: (i, 0, h)),
                   pl.BlockSpec((1, 1, nc, hd, ML_CHUNK), lambda i, h: (i, h, 0, 0, 0))],
        out_shape=[jax.ShapeDtypeStruct((b, s, ML_WIDTH), BF16),
                   jax.ShapeDtypeStruct((b, ML_HEADS, nc, hd, ML_CHUNK), BF16)],
        compiler_params=_cparams(("arbitrary", "arbitrary")),
        name="mlstm_prep",
    )(mlqk, mlqk, conv_w, conv_w, *rope)


def _log_sigmoid(x):
    return jnp.minimum(x, 0.0) - jnp.log1p(jnp.exp(-jnp.abs(x)))


def _ml_gates_body(gc_ref, gr_ref, bc_ref, br_ref, oc_ref, or_ref):
    nc = gr_ref.shape[1]
    c = ML_CHUNK
    ii = lax.broadcasted_iota(jnp.int32, (c, c), 0)
    jj = lax.broadcasted_iota(jnp.int32, (c, c), 1)
    lower = (ii >= jj).astype(F32)
    upper = (ii <= jj).astype(F32)
    lane_kind = lax.broadcasted_iota(jnp.int32, (c, LANES), 1) // ML_HEADS
    row_kind = lax.broadcasted_iota(jnp.int32, (4 * ML_HEADS, c), 0) // ML_HEADS
    for j in range(nc):
        g = gc_ref[0, j * c:(j + 1) * c, :] + bc_ref[...]
        lf = _log_sigmoid(g)
        pre = jnp.dot(lower, lf, precision=HIGHEST, preferred_element_type=F32)
        suf = jnp.dot(upper, lf, precision=HIGHEST, preferred_element_type=F32)
        oc_ref[0, j * c:(j + 1) * c, :] = jnp.where(lane_kind == 1, pre, jnp.where(lane_kind == 3, suf, g))
        gr = gr_ref[0, j] + br_ref[...]
        lfr = _log_sigmoid(gr)
        pre_r = jnp.dot(lfr, upper, precision=HIGHEST, preferred_element_type=F32)
        suf_r = jnp.dot(lfr, lower, precision=HIGHEST, preferred_element_type=F32)
        or_ref[0, j] = jnp.where(row_kind == 1, pre_r, jnp.where(row_kind == 3, suf_r, gr))


def _ml_gates(mlg, gate_bias):
    b, s, _ = mlg.shape
    nc = s // ML_CHUNK
    ng = 4 * ML_HEADS
    g_rows = mlg[:, :, :ng].reshape(b, nc, ML_CHUNK, ng).transpose(0, 1, 3, 2)
    bias_c = jnp.zeros((1, LANES), F32).at[0, :ng].set(gate_bias)
    bias_r = gate_bias.reshape(ng, 1)
    return pl.pallas_call(
        _ml_gates_body,
        grid=(b,),
        in_specs=[pl.BlockSpec((1, s, LANES), lambda i: (i, 0, 0)),
                  pl.BlockSpec((1, nc, ng, ML_CHUNK), lambda i: (i, 0, 0, 0)),
                  pl.BlockSpec((1, LANES), lambda i: (0, 0)),
                  pl.BlockSpec((ng, 1), lambda i: (0, 0))],
        out_specs=[pl.BlockSpec((1, s, LANES), lambda i: (i, 0, 0)),
                   pl.BlockSpec((1, nc, ng, ML_CHUNK), lambda i: (i, 0, 0, 0))],
        out_shape=[jax.ShapeDtypeStruct((b, s, LANES), F32),
                   jax.ShapeDtypeStruct((b, nc, ng, ML_CHUNK), F32)],
        compiler_params=_cparams(("arbitrary",)),
        name="mlstm_gates",
    )(mlg, g_rows, bias_c, bias_r)


def _ml_step(q, kt, v_aug, ig_r, b_c, b_r, state, m, *, forward):
    c = ML_CHUNK
    d = ML_HEAD_DIM
    ti = lax.broadcasted_iota(jnp.int32, (c, c), 0)
    si = lax.broadcasted_iota(jnp.int32, (c, c), 1)
    allowed = (si <= ti) if forward else (si >= ti)
    a_r = ig_r - b_r
    log_d = jnp.where(allowed, b_c + a_r, -jnp.inf)
    inter = b_c + m
    m_t = jnp.maximum(jnp.max(log_d, axis=-1, keepdims=True), inter)
    qk = jnp.dot(q, kt, preferred_element_type=F32)
    s_mat = (qk * jnp.exp(log_d - m_t)).astype(BF16)
    w_inter = jnp.exp(inter - m_t)
    tot = (jnp.dot(s_mat, v_aug, preferred_element_type=F32)
           + w_inter * jnp.dot(q, state.astype(BF16), preferred_element_type=F32))
    num = tot[:, :d]
    den = tot[:, d:d + 1]
    h = num / jnp.maximum(jnp.abs(den), jnp.exp(-m_t))
    b_end = b_r[:, c - 1:c] if forward else b_r[:, 0:1]
    log_w = b_end + a_r
    m_new = jnp.maximum(b_end + m, jnp.max(log_w, axis=-1, keepdims=True))
    w_r = jnp.exp(log_w - m_new)
    decay = jnp.exp(b_end + m - m_new)
    kw = (kt.astype(F32) * w_r).astype(BF16)
    state_new = decay * state + jnp.dot(kw, v_aug, preferred_element_type=F32)
    return h, state_new, m_new


def _ml_scan_body(q_ref, kt_ref, v_ref, gc_ref, gr_ref, o_ref, nw_ref, out_ref, hsum_ref, *, n_lat):
    c = ML_CHUNK
    d = ML_HEAD_DIM
    s = q_ref.shape[1]
    nc_lat = n_lat // c
    nc_all = s // c
    ones_col = (lax.broadcasted_iota(jnp.int32, (c, d), 1) == 0).astype(BF16)
    nw = nw_ref[...]

    def run(j, state, m, forward):
        row0 = j * c if isinstance(j, int) else pl.multiple_of(j * c, c)
        q = q_ref[0, pl.ds(row0, c), :]
        v_aug = jnp.concatenate([v_ref[0, pl.ds(row0, c), :], ones_col], axis=1)
        gc = gc_ref[0, 0, pl.ds(row0, c), :]
        gr = gr_ref[0, 0, j]
        k0 = 0 if forward else 2
        return _ml_step(q, kt_ref[0, 0, j], v_aug, gr[k0:k0 + 1], gc[:, k0 + 1:k0 + 2], gr[k0 + 1:k0 + 2],
                        state, m, forward=forward)

    def finalize(j, h):
        row0 = j * c if isinstance(j, int) else pl.multiple_of(j * c, c)
        hc = h - jnp.mean(h, axis=-1, keepdims=True)
        hn = hc * lax.rsqrt(jnp.mean(hc * hc, axis=-1, keepdims=True) + LN_EPS)
        o = o_ref[0, pl.ds(row0, c), :]
        out_ref[0, pl.ds(row0, c), :] = (_sigmoid(o) * hn * nw).astype(out_ref.dtype)

    zero_state = jnp.zeros((d, 2 * d), F32)
    m0 = jnp.full((1, 1), ML_M_INIT, F32)
    sf, mf = zero_state, m0
    hc_f = []
    for j in range(nc_lat, nc_all):
        h, sf, mf = run(j, sf, mf, True)
        hc_f.append(h)
    sb, mb = zero_state, m0
    hc_b = {}
    for j in range(nc_all - 1, nc_lat - 1, -1):
        h, sb, mb = run(j, sb, mb, False)
        hc_b[j] = h
    for idx, j in enumerate(range(nc_lat, nc_all)):
        finalize(j, hc_f[idx] + hc_b[j])

    half = nc_lat // 2

    def first_half(i, carry):
        sf, mf, sb, mb = carry
        jb = nc_lat - 1 - i
        hf, sf, mf = run(i, sf, mf, True)
        hb, sb, mb = run(jb, sb, mb, False)
        hsum_ref[pl.ds(pl.multiple_of(i * c, c), c), :] = hf
        hsum_ref[pl.ds(pl.multiple_of(jb * c, c), c), :] = hb
        return sf, mf, sb, mb

    carry = lax.fori_loop(0, half, first_half, (sf, mf, sb, mb))

    def second_half(i, carry):
        sf, mf, sb, mb = carry
        jb = nc_lat - 1 - i
        hf, sf, mf = run(i, sf, mf, True)
        hb, sb, mb = run(jb, sb, mb, False)
        finalize(i, hf + hsum_ref[pl.ds(pl.multiple_of(i * c, c), c), :])
        finalize(jb, hb + hsum_ref[pl.ds(pl.multiple_of(jb * c, c), c), :])
        return sf, mf, sb, mb

    lax.fori_loop(half, nc_lat, second_half, carry)


def _ml_scan(q, kt, mlv, gcol, grow, mlo, norm_w, n_lat):
    b, s, _ = q.shape
    hd = ML_HEAD_DIM
    nc = s // ML_CHUNK
    ng = 4 * ML_HEADS
    gc_h = gcol[:, :, :ng].reshape(b, s, 4, ML_HEADS).transpose(0, 3, 1, 2)
    gr_h = grow.reshape(b, nc, 4, ML_HEADS, ML_CHUNK).transpose(0, 3, 1, 2, 4)
    tok = pl.BlockSpec((1, s, hd), lambda i, h: (i, 0, h))
    return pl.pallas_call(
        functools.partial(_ml_scan_body, n_lat=n_lat),
        grid=(b, ML_HEADS),
        in_specs=[tok,
                  pl.BlockSpec((1, 1, nc, hd, ML_CHUNK), lambda i, h: (i, h, 0, 0, 0)),
                  tok,
                  pl.BlockSpec((1, 1, s, 4), lambda i, h: (i, h, 0, 0)),
                  pl.BlockSpec((1, 1, nc, 4, ML_CHUNK), lambda i, h: (i, h, 0, 0, 0)),
                  tok,
                  pl.BlockSpec((1, hd), lambda i, h: (0, h))],
        out_specs=tok,
        out_shape=jax.ShapeDtypeStruct((b, s, ML_WIDTH), BF16),
        scratch_shapes=[pltpu.VMEM((n_lat, hd), F32)],
        compiler_params=_cparams(("arbitrary", "arbitrary")),
        name="mlstm_scan",
    )(q, kt, mlv, gc_h, gr_h, mlo, norm_w)


def _top2(logits, n_valid):
    lane = lax.broadcasted_iota(jnp.int32, logits.shape, 1).astype(F32)
    neg = jnp.float32(-jnp.inf)
    l = jnp.where(lane < n_valid, logits, neg)
    m1 = jnp.max(l, axis=-1, keepdims=True)
    e1 = jnp.min(jnp.where(l == m1, lane, float(LANES)), axis=-1, keepdims=True)
    l2 = jnp.where(lane == e1, neg, l)
    m2 = jnp.max(l2, axis=-1, keepdims=True)
    e2 = jnp.min(jnp.where(l2 == m2, lane, float(LANES)), axis=-1, keepdims=True)
    x2 = jnp.exp(m2 - m1)
    p1 = 1.0 / (1.0 + x2)
    p2 = x2 / (1.0 + x2)
    return jnp.where(lane == 0, e1, jnp.where(lane == 1, e2, jnp.where(lane == 2, p1, jnp.where(lane == 3, p2, 0.0))))


def _merge_body(*refs, n_lat, alpha, moe):
    (a_ref, n_ref, m_ref, g_ref, x_ref, wbp_ref, wbn_ref, wbm_ref, wo_ref,
     g1l_ref, g1c_ref, shl_ref, shc_ref, scl_ref, scc_ref, lng_ref, lnb_ref) = refs[:17]
    if moe:
        rw_ref, rb_ref, x_out, tok_out, route_out = refs[17:]
    else:
        x_out, tok_out = refs[17:]
    tm = x_ref.shape[1]
    d = x_ref.shape[2]
    is_ctx = _is_ctx_rows(pl.program_id(1), tm, n_lat)
    g = g_ref[0]
    y = (_sigmoid(g[:, :d]) * jnp.dot(a_ref[0], wbp_ref[...], preferred_element_type=F32)
         + _sigmoid(g[:, d:2 * d]) * jnp.dot(n_ref[0], wbn_ref[...], preferred_element_type=F32)
         + _sigmoid(g[:, 2 * d:]) * jnp.dot(m_ref[0], wbm_ref[...], preferred_element_type=F32))
    y = jnp.dot(y.astype(BF16), wo_ref[...], preferred_element_type=F32)
    z = alpha * x_ref[0] + _tile_mod(g1l_ref, g1c_ref, is_ctx) * y
    xn = _ln_plain(z) * lng_ref[...] + lnb_ref[...]
    x_out[0] = xn
    tok = _ln_plain(xn) * (1.0 + _tile_mod(scl_ref, scc_ref, is_ctx)) + _tile_mod(shl_ref, shc_ref, is_ctx)
    tok_out[0] = tok.astype(tok_out.dtype)
    if moe:
        lane = lax.broadcasted_iota(jnp.int32, (tm, LANES), 1)
        logits = jnp.zeros((tm, LANES), F32)
        for e in range(N_EXPERTS):
            col = jnp.sum(tok * rw_ref[e:e + 1, :], axis=-1, keepdims=True)
            logits = jnp.where(lane == e, col, logits)
        route_out[0] = _top2(logits + rb_ref[...], N_EXPERTS)


def _merge(a, n, m, gates, x_all, w_bp, w_bn, w_bm, w_o, g1, sh2, sc2, ln_g, ln_b, n_lat, alpha, router=None):
    b, s, d = x_all.shape
    tm = TM_MERGE
    moe = router is not None

    def tok(w):
        return pl.BlockSpec((1, tm, w), lambda i, j: (i, j, 0))

    def full(arr):
        return pl.BlockSpec(arr.shape, lambda i, j: (0,) * arr.ndim)

    in_specs = ([tok(a.shape[2]), tok(n.shape[2]), tok(m.shape[2]), tok(gates.shape[2]), tok(d),
                 full(w_bp), full(w_bn), full(w_bm), full(w_o)]
                + _row_specs(d, b) * 3 + [full(ln_g), full(ln_b)])
    args = [a, n, m, gates, x_all, w_bp, w_bn, w_bm, w_o, g1, g1, sh2, sh2, sc2, sc2, ln_g, ln_b]
    out_specs = [tok(d), tok(d)]
    out_shape = [jax.ShapeDtypeStruct((b, s, d), F32), jax.ShapeDtypeStruct((b, s, d), F32 if moe else BF16)]
    if moe:
        in_specs += [full(router[0]), full(router[1])]
        args += list(router)
        out_specs.append(tok(LANES))
        out_shape.append(jax.ShapeDtypeStruct((b, s, LANES), F32))
    return pl.pallas_call(
        functools.partial(_merge_body, n_lat=n_lat, alpha=alpha, moe=moe),
        grid=(b, s // tm),
        in_specs=in_specs,
        out_specs=out_specs,
        out_shape=out_shape,
        compiler_params=_cparams(("arbitrary", "arbitrary")),
        name="merge",
    )(*args)


def _finish_rows(f, x, g2, lng, lnb, alpha):
    return _ln_plain(alpha * x + g2 * f) * lng + lnb


def _ffn_body(*refs, n_lat, alpha, with_next):
    (t_ref, x_ref, w1_ref, w3_ref, w2_ref, g2l_ref, g2c_ref, lng_ref, lnb_ref) = refs[:9]
    if with_next:
        shl_ref, shc_ref, scl_ref, scc_ref, x_out, u_out, acc_ref = refs[9:]
    else:
        x_out, acc_ref = refs[9:]
    f = pl.program_id(2)
    t = t_ref[0]
    h1 = jnp.dot(t, w1_ref[...], preferred_element_type=F32)
    h3 = jnp.dot(t, w3_ref[...], preferred_element_type=F32)
    contrib = jnp.dot((_silu(h1) * h3).astype(BF16), w2_ref[...], preferred_element_type=F32)

    @pl.when(f == 0)
    def _():
        acc_ref[...] = contrib

    @pl.when(f > 0)
    def _():
        acc_ref[...] += contrib

    @pl.when(f == pl.num_programs(2) - 1)
    def _():
        tm = x_ref.shape[1]
        is_ctx = _is_ctx_rows(pl.program_id(1), tm, n_lat)
        xn = _finish_rows(acc_ref[...], x_ref[0], _tile_mod(g2l_ref, g2c_ref, is_ctx), lng_ref[...], lnb_ref[...], alpha)
        x_out[0] = xn
        if with_next:
            u = _ln_plain(xn) * (1.0 + _tile_mod(scl_ref, scc_ref, is_ctx)) + _tile_mod(shl_ref, shc_ref, is_ctx)
            u_out[0] = u.astype(u_out.dtype)


def _ffn(tok, x_all, w1, w3, w2, g2, ln_g, ln_b, n_lat, alpha, nxt=None):
    b, s, d = x_all.shape
    dff = w1.shape[1]
    tm, tf = TM_TOKEN, TF_FFN
    with_next = nxt is not None

    def tokspec():
        return pl.BlockSpec((1, tm, d), lambda i, j, f: (i, j, 0))

    def rows():
        return [pl.BlockSpec((1, 1, d), lambda i, j, f: (i, 0, 0)),
                pl.BlockSpec((1, 1, d), lambda i, j, f: (b, 0, 0))]

    vec = pl.BlockSpec((1, d), lambda i, j, f: (0, 0))
    in_specs = [tokspec(), tokspec(),
                pl.BlockSpec((d, tf), lambda i, j, f: (0, f)),
                pl.BlockSpec((d, tf), lambda i, j, f: (0, f)),
                pl.BlockSpec((tf, d), lambda i, j, f: (f, 0))] + rows() + [vec, vec]
    args = [tok, x_all, w1, w3, w2, g2, g2, ln_g, ln_b]
    out_specs = [tokspec()]
    out_shape = [jax.ShapeDtypeStruct((b, s, d), F32)]
    if with_next:
        in_specs += rows() + rows()
        args += [nxt[0], nxt[0], nxt[1], nxt[1]]
        out_specs.append(tokspec())
        out_shape.append(jax.ShapeDtypeStruct((b, s, d), BF16))
    res = pl.pallas_call(
        functools.partial(_ffn_body, n_lat=n_lat, alpha=alpha, with_next=with_next),
        grid=(b, s // tm, dff // tf),
        in_specs=in_specs,
        out_specs=out_specs,
        out_shape=out_shape,
        scratch_shapes=[pltpu.VMEM((tm, d), F32)],
        compiler_params=_cparams(("arbitrary", "arbitrary", "arbitrary")),
        name="ffn_dense",
    )(*args)
    return res if with_next else (res[0], None)


def _moe_plan(top_e, tok_id, n_slots_per_tok):
    r = MOE_ROWS
    flat_e = top_e.reshape(-1)
    n_assign = flat_e.shape[0]
    onehot = (flat_e[:, None] == jnp.arange(N_EXPERTS, dtype=jnp.int32)[None, :]).astype(jnp.int32)
    cum = jnp.cumsum(onehot, axis=0)
    rank = jnp.take_along_axis(cum, flat_e[:, None], axis=1)[:, 0] - 1
    counts = cum[-1]
    nblk = (counts + r - 1) // r
    bend = jnp.cumsum(nblk)
    bstart = bend - nblk
    dest = bstart[flat_e] * r + rank
    n_blocks = -(-n_assign // r) + N_EXPERTS
    n_rows = n_blocks * r
    row_tok = jnp.zeros((n_rows,), jnp.int32).at[dest].set(jnp.repeat(tok_id, n_slots_per_tok))
    assign = jnp.arange(n_assign, dtype=jnp.int32)
    slot = (assign % n_slots_per_tok) * (n_assign // n_slots_per_tok) + assign // n_slots_per_tok
    row_slot = jnp.zeros((n_rows,), jnp.int32).at[dest].set(slot)
    blk = jnp.arange(n_blocks, dtype=jnp.int32)
    total = bend[-1]
    e_of = jnp.sum((blk[:, None] >= bend[None, :]).astype(jnp.int32), axis=1)
    e_last = jnp.sum((total - 1 >= bend).astype(jnp.int32))
    block_e = jnp.minimum(e_of, e_last)
    nvalid = jnp.where(blk < total, jnp.clip(counts[block_e] - (blk - bstart[block_e]) * r, 0, r), 0).astype(jnp.int32)
    return (block_e, nvalid, row_tok.reshape(n_blocks, 1, r), row_slot.reshape(n_blocks, 1, r))


def _moe_body(be_ref, nv_ref, rtok_ref, rslot_ref, tok_hbm, w1_ref, w3_ref, w2_ref, y_hbm,
              xbuf, acc_ref, gsem, ssem):
    i = pl.program_id(0)
    f = pl.program_id(1)
    nf = pl.num_programs(1)
    nv = nv_ref[i]

    @pl.when((i == 0) & (f == 0))
    def _():
        xbuf[...] = jnp.zeros_like(xbuf)

    def gather_copy(r):
        return pltpu.make_async_copy(tok_hbm.at[pl.ds(rtok_ref[0, 0, r], 1)], xbuf.at[pl.ds(r, 1)], gsem)

    def scatter_copy(r):
        return pltpu.make_async_copy(acc_ref.at[pl.ds(r, 1)], y_hbm.at[pl.ds(rslot_ref[0, 0, r], 1)], ssem)

    @pl.when((f == 0) & (nv > 0))
    def _():
        def issue(r, c):
            gather_copy(r).start()
            return c

        lax.fori_loop(0, nv, issue, 0)

        def wait(r, c):
            gather_copy(r).wait()
            return c

        lax.fori_loop(0, nv, wait, 0)

    @pl.when(nv > 0)
    def _():
        x = xbuf[...].astype(BF16)
        h1 = jnp.dot(x, w1_ref[0], preferred_element_type=F32)
        h3 = jnp.dot(x, w3_ref[0], preferred_element_type=F32)
        contrib = jnp.dot((_silu(h1) * h3).astype(BF16), w2_ref[0], preferred_element_type=F32)

        @pl.when(f == 0)
        def _():
            acc_ref[...] = contrib

        @pl.when(f > 0)
        def _():
            acc_ref[...] += contrib

    @pl.when((f == nf - 1) & (nv > 0))
    def _():
        def issue(r, c):
            scatter_copy(r).start()
            return c

        lax.fori_loop(0, nv, issue, 0)

        def wait(r, c):
            scatter_copy(r).wait()
            return c

        lax.fori_loop(0, nv, wait, 0)


def _moe_experts(tok_flat, plan, w1, w3, w2, n_slots):
    block_e, nvalid, row_tok, row_slot = plan
    n_blocks = block_e.shape[0]
    d = tok_flat.shape[1]
    dff = w1.shape[2]
    r, tf = MOE_ROWS, TF_FFN
    nf = dff // tf

    def f_eff(i, f, nv):
        return jnp.where(nv[i] > 0, f, nf - 1)

    grid_spec = pltpu.PrefetchScalarGridSpec(
        num_scalar_prefetch=2,
        grid=(n_blocks, nf),
        in_specs=[pl.BlockSpec((1, 1, r), lambda i, f, be, nv: (i, 0, 0), memory_space=pltpu.SMEM),
                  pl.BlockSpec((1, 1, r), lambda i, f, be, nv: (i, 0, 0), memory_space=pltpu.SMEM),
                  pl.BlockSpec(memory_space=pl.ANY),
                  pl.BlockSpec((1, d, tf), lambda i, f, be, nv: (be[i], 0, f_eff(i, f, nv))),
                  pl.BlockSpec((1, d, tf), lambda i, f, be, nv: (be[i], 0, f_eff(i, f, nv))),
                  pl.BlockSpec((1, tf, d), lambda i, f, be, nv: (be[i], f_eff(i, f, nv), 0))],
        out_specs=pl.BlockSpec(memory_space=pl.ANY),
        scratch_shapes=[pltpu.VMEM((r, d), F32), pltpu.VMEM((r, d), F32),
                        pltpu.SemaphoreType.DMA(()), pltpu.SemaphoreType.DMA(())],
    )
    return pl.pallas_call(
        _moe_body,
        grid_spec=grid_spec,
        out_shape=jax.ShapeDtypeStruct((n_slots, d), F32),
        compiler_params=_cparams(("arbitrary", "arbitrary")),
        name="moe_experts",
    )(block_e, nvalid, row_tok, row_slot, tok_flat, w1, w3, w2)


def _combine_body(*refs, n_lat, alpha, with_next):
    (y0_ref, y1_ref, p_ref, x_ref, g2l_ref, g2c_ref, lng_ref, lnb_ref) = refs[:8]
    if with_next:
        shl_ref, shc_ref, scl_ref, scc_ref, x_out, u_out = refs[8:]
    else:
        (x_out,) = refs[8:]
    tm = x_ref.shape[1]
    is_ctx = _is_ctx_rows(pl.program_id(1), tm, n_lat)
    p = p_ref[0]
    f = p[:, 2:3] * y0_ref[0, 0] + p[:, 3:4] * y1_ref[0, 0]
    xn = _finish_rows(f, x_ref[0], _tile_mod(g2l_ref, g2c_ref, is_ctx), lng_ref[...], lnb_ref[...], alpha)
    x_out[0] = xn
    if with_next:
        u = _ln_plain(xn) * (1.0 + _tile_mod(scl_ref, scc_ref, is_ctx)) + _tile_mod(shl_ref, shc_ref, is_ctx)
        u_out[0] = u.astype(u_out.dtype)


def _combine(y2, route, x_all, g2, ln_g, ln_b, n_lat, alpha, n_tok_rows, nxt=None):
    b, s, d = x_all.shape
    tm = 512 if n_tok_rows % 512 == 0 else TM_MERGE
    with_next = nxt is not None

    def tok(w):
        return pl.BlockSpec((1, tm, w), lambda i, j: (i, j, 0))

    def plane(k):
        return pl.BlockSpec((1, 1, tm, d), lambda i, j: (k, i, j, 0))

    vec = pl.BlockSpec((1, d), lambda i, j: (0, 0))
    in_specs = [plane(0), plane(1), tok(LANES), tok(d)] + _row_specs(d, b) + [vec, vec]
    args = [y2, y2, route, x_all, g2, g2, ln_g, ln_b]
    out_specs = [tok(d)]
    out_shape = [jax.ShapeDtypeStruct((b, n_tok_rows, d), F32)]
    if with_next:
        in_specs += _row_specs(d, b) * 2
        args += [nxt[0], nxt[0], nxt[1], nxt[1]]
        out_specs.append(tok(d))
        out_shape.append(jax.ShapeDtypeStruct((b, n_tok_rows, d), BF16))
    res = pl.pallas_call(
        functools.partial(_combine_body, n_lat=n_lat, alpha=alpha, with_next=with_next),
        grid=(b, n_tok_rows // tm),
        in_specs=in_specs,
        out_specs=out_specs,
        out_shape=out_shape,
        compiler_params=_cparams(("arbitrary", "arbitrary")),
        name="moe_combine",
    )(*args)
    return res if with_next else (res[0], None)


def _block_diag(pool_w):
    g, c, _ = pool_w.shape
    out = jnp.zeros((g * c, g * c), pool_w.dtype)
    for i in range(g):
        out = out.at[i * c:(i + 1) * c, i * c:(i + 1) * c].set(pool_w[i])
    return out


def kernel(x, c, ctx, c_ctx, w_mod, b_mod, w_in, ml_gate_bias, ml_conv_w, ml_norm_w, pool_w, pool_scale,
           na_rpb, w_branch_pool, w_branch_na, w_branch_ml, w_out, ln1_g, ln1_b, ln2_g, ln2_b,
           ffn_w1, ffn_w3, ffn_w2, moe_router_w, moe_router_b, moe_w1, moe_w3, moe_w2):
    b, n_lat, d = x.shape
    n_ctx = ctx.shape[1]
    s = n_lat + n_ctx
    depth = w_in.shape[0]
    alpha = (2 * depth) ** 0.25
    off_g = POOL_WIDTH + 3 * NA_WIDTH + 4 * ML_WIDTH
    n_gate_cols = 4 * ML_HEADS

    x_all = jnp.concatenate([x, ctx], axis=1)
    mod = _mod_vectors(c, c_ctx, w_mod, b_mod)

    def mod_part(layer, k):
        return mod[layer, :, k * d:(k + 1) * d].reshape(MOD_ROWS, 1, d)

    rope = _rope_tables(n_lat)
    u = _ln_mod(x_all, mod_part(0, 0), mod_part(0, 1), n_lat)

    for layer in range(depth):
        last = layer == depth - 1
        w_l = w_in[layer]
        w_main = w_l[:, :off_g].astype(BF16)
        w_g = jnp.zeros((d, LANES), BF16).at[:, :n_gate_cols].set(w_l[:, off_g:off_g + n_gate_cols].astype(BF16))
        w_gate = w_l[:, off_g + n_gate_cols:].astype(BF16)
        pool_in, na_qkv, mlqk, mlv, mlo, mlg = _proj_main(u, w_main, w_g)
        gates = _proj_gate(u, w_gate)

        a = _pool(pool_in, _block_diag(pool_w[layer]).astype(BF16), pool_scale[layer].reshape(1, -1), n_lat)
        n = _na(na_qkv, _na_bias_classes(na_rpb[layer]), n_lat)
        q_ml, kt_ml = _ml_prep(mlqk, ml_conv_w[layer], rope, n_lat)
        gcol, grow = _ml_gates(mlg, ml_gate_bias[layer])
        m = _ml_scan(q_ml, kt_ml, mlv, gcol, grow, mlo, ml_norm_w[layer].reshape(1, -1), n_lat)

        is_moe = layer % 2 == 1
        i = layer // 2
        router = None
        if is_moe:
            rw = moe_router_w[i].T
            rb = jnp.zeros((1, LANES), F32).at[0, :N_EXPERTS].set(moe_router_b[i])
            router = (rw, rb)
        merged = _merge(a, n, m, gates, x_all,
                        w_branch_pool[layer].astype(BF16), w_branch_na[layer].astype(BF16),
                        w_branch_ml[layer].astype(BF16), w_out[layer].astype(BF16),
                        mod_part(layer, 2), mod_part(layer, 3), mod_part(layer, 4),
                        ln1_g[layer].reshape(1, d), ln1_b[layer].reshape(1, d), n_lat, alpha, router)
        nxt = None if last else (mod_part(layer + 1, 0), mod_part(layer + 1, 1))
        g2 = mod_part(layer, 5)
        lng, lnb = ln2_g[layer].reshape(1, d), ln2_b[layer].reshape(1, d)
        if not is_moe:
            x_mid, tok = merged
            x_all, u = _ffn(tok, x_mid, ffn_w1[i].astype(BF16), ffn_w3[i].astype(BF16), ffn_w2[i].astype(BF16),
                            g2, lng, lnb, n_lat, alpha, nxt)
        else:
            x_mid, tok, route = merged
            n_rows_tok = n_lat if last else s
            x_all, u = _moe_layer(tok, route, x_mid, moe_w1[i], moe_w3[i], moe_w2[i], g2, lng, lnb,
                                  n_lat, alpha, n_rows_tok, nxt)
    return x_all[:, :n_lat]


def _moe_layer(tok, route, x_mid, w1, w3, w2, g2, lng, lnb, n_lat, alpha, n_rows_tok, nxt):
    b, s, d = tok.shape
    top_e = route[:, :n_rows_tok, :TOP_K].astype(jnp.int32).reshape(b * n_rows_tok, TOP_K)
    tok_id = (jnp.arange(b, dtype=jnp.int32)[:, None] * s
              + jnp.arange(n_rows_tok, dtype=jnp.int32)[None, :]).reshape(-1)
    plan = _moe_plan(top_e, tok_id, TOP_K)
    y2 = _moe_experts(tok.reshape(b * s, d), plan, w1.astype(BF16), w3.astype(BF16), w2.astype(BF16),
                      b * n_rows_tok * TOP_K)
    return _combine(y2.reshape(TOP_K, b, n_rows_tok, d), route, x_mid, g2, lng, lnb, n_lat, alpha, n_rows_tok, nxt)
```

```python
import functools

import numpy as np
import jax
import jax.numpy as jnp
from jax import lax
from jax.experimental import pallas as pl
from jax.experimental.pallas import tpu as pltpu

F32 = jnp.float32
BF16 = jnp.bfloat16
HIGHEST = lax.Precision.HIGHEST

GRID_W = 64
POOL_GROUPS = 4
POOL_GROUP_DIM = 64
POOL_WIDTH = POOL_GROUPS * POOL_GROUP_DIM
POOL_HALF_WINDOWS = (1, 2, 4, 8)
NA_HEADS = 4
NA_HEAD_DIM = 64
NA_WIDTH = NA_HEADS * NA_HEAD_DIM
NA_WIN_ROWS = 8
NA_WIN_COLS = 16
ML_HEADS = 4
ML_HEAD_DIM = 128
ML_WIDTH = ML_HEADS * ML_HEAD_DIM
ML_CHUNK = 128
ML_M_INIT = -1e30
ROPE_THETA = 10000.0
N_EXPERTS = 8
TOP_K = 2
N_BRANCHES = 3
LN_EPS = 1e-6

LANES = 128
MOD_ROWS = 16
VMEM_LIMIT = 56 * 1024 * 1024

TM_TOKEN = 768
TM_MERGE = 384
TF_FFN = 512
MOE_ROWS = 1024
TM_MOE = 256


def _cparams(sem):
    return pltpu.CompilerParams(dimension_semantics=sem, vmem_limit_bytes=VMEM_LIMIT)


def _sigmoid(x):
    return 1.0 / (1.0 + jnp.exp(-x))


def _silu(x):
    return x * _sigmoid(x)


def _ln_plain(x):
    xc = x - jnp.mean(x, axis=-1, keepdims=True)
    return xc * lax.rsqrt(jnp.mean(xc * xc, axis=-1, keepdims=True) + LN_EPS)


def _is_ctx_rows(tile_idx, tm, n_lat):
    row = tile_idx * tm + lax.broadcasted_iota(jnp.int32, (tm, 1), 0)
    return row >= n_lat


def _tile_mod(lat_ref, ctx_ref, is_ctx):
    return jnp.where(is_ctx, ctx_ref[0], lat_ref[0])


def _mod_body(s_ref, w_ref, b_ref, o_ref):
    s = _silu(s_ref[...])
    o_ref[0] = jnp.dot(s, w_ref[0], precision=HIGHEST, preferred_element_type=F32) + b_ref[0]


def _mod_vectors(c, c_ctx, w_mod, b_mod):
    depth, d, d6 = w_mod.shape
    b = c.shape[0]
    s = jnp.zeros((MOD_ROWS, d), F32).at[:b].set(c).at[b].set(c_ctx)
    tn = 1536
    return pl.pallas_call(
        _mod_body,
        grid=(depth, d6 // tn),
        in_specs=[pl.BlockSpec((MOD_ROWS, d), lambda l, j: (0, 0)),
                  pl.BlockSpec((1, d, tn), lambda l, j: (l, 0, j)),
                  pl.BlockSpec((1, 1, tn), lambda l, j: (l, 0, j))],
        out_specs=pl.BlockSpec((1, MOD_ROWS, tn), lambda l, j: (l, 0, j)),
        out_shape=jax.ShapeDtypeStruct((depth, MOD_ROWS, d6), F32),
        compiler_params=_cparams(("arbitrary", "arbitrary")),
        name="mod_vectors",
    )(s, w_mod, b_mod.reshape(depth, 1, d6))


def _ln_mod_body(x_ref, shl_ref, shc_ref, scl_ref, scc_ref, u_ref, *, n_lat):
    tm = x_ref.shape[1]
    is_ctx = _is_ctx_rows(pl.program_id(1), tm, n_lat)
    sh = _tile_mod(shl_ref, shc_ref, is_ctx)
    sc = _tile_mod(scl_ref, scc_ref, is_ctx)
    u_ref[0] = (_ln_plain(x_ref[0]) * (1.0 + sc) + sh).astype(u_ref.dtype)


def _row_specs(d, nb):
    return [pl.BlockSpec((1, 1, d), lambda b, j: (b, 0, 0)),
            pl.BlockSpec((1, 1, d), lambda b, j: (nb, 0, 0))]


def _ln_mod(x_all, shift, scale, n_lat):
    b, s, d = x_all.shape
    tm = TM_TOKEN
    return pl.pallas_call(
        functools.partial(_ln_mod_body, n_lat=n_lat),
        grid=(b, s // tm),
        in_specs=[pl.BlockSpec((1, tm, d), lambda i, j: (i, j, 0))] + _row_specs(d, b) + _row_specs(d, b),
        out_specs=pl.BlockSpec((1, tm, d), lambda i, j: (i, j, 0)),
        out_shape=jax.ShapeDtypeStruct((b, s, d), BF16),
        compiler_params=_cparams(("arbitrary", "arbitrary")),
        name="ln_mod",
    )(x_all, shift, shift, scale, scale)


def _proj_main_body(u_ref, w_ref, wg_ref, pool_ref, na_ref, mlqk_ref, mlv_ref, mlo_ref, mlg_ref):
    u = u_ref[0]

    def seg(lo, hi):
        return jnp.dot(u, w_ref[:, lo:hi], preferred_element_type=F32)

    off_na = POOL_WIDTH
    off_ml = off_na + 3 * NA_WIDTH
    off_mlv = off_ml + 2 * ML_WIDTH
    off_mlo = off_ml + 3 * ML_WIDTH
    pool_ref[0] = seg(0, off_na)
    na_ref[0, :, :NA_WIDTH] = (seg(off_na, off_na + NA_WIDTH) * NA_HEAD_DIM ** -0.5).astype(na_ref.dtype)
    na_ref[0, :, NA_WIDTH:] = seg(off_na + NA_WIDTH, off_ml).astype(na_ref.dtype)
    mlqk_ref[0] = seg(off_ml, off_mlv)
    mlv_ref[0] = seg(off_mlv, off_mlo).astype(mlv_ref.dtype)
    mlo_ref[0] = seg(off_mlo, off_mlo + ML_WIDTH)
    mlg_ref[0] = jnp.dot(u, wg_ref[...], preferred_element_type=F32)


def _proj_main(u, w_main, w_g):
    b, s, d = u.shape
    tm = TM_TOKEN
    widths = (POOL_WIDTH, 3 * NA_WIDTH, 2 * ML_WIDTH, ML_WIDTH, ML_WIDTH, LANES)
    dtypes = (F32, BF16, F32, BF16, F32, F32)
    return pl.pallas_call(
        _proj_main_body,
        grid=(b, s // tm),
        in_specs=[pl.BlockSpec((1, tm, d), lambda i, j: (i, j, 0)),
                  pl.BlockSpec(w_main.shape, lambda i, j: (0, 0)),
                  pl.BlockSpec(w_g.shape, lambda i, j: (0, 0))],
        out_specs=[pl.BlockSpec((1, tm, w), lambda i, j: (i, j, 0)) for w in widths],
        out_shape=[jax.ShapeDtypeStruct((b, s, w), dt) for w, dt in zip(widths, dtypes)],
        compiler_params=_cparams(("arbitrary", "arbitrary")),
        name="proj_main",
    )(u, w_main, w_g)


def _proj_gate_body(u_ref, w_ref, g_ref):
    u = u_ref[0]
    n = w_ref.shape[1]
    step = 1024
    for lo in range(0, n, step):
        g_ref[0, :, lo:lo + step] = jnp.dot(u, w_ref[:, lo:lo + step], preferred_element_type=F32)


def _proj_gate(u, w_gate):
    b, s, d = u.shape
    tm = TM_TOKEN
    n = w_gate.shape[1]
    return pl.pallas_call(
        _proj_gate_body,
        grid=(b, s // tm),
        in_specs=[pl.BlockSpec((1, tm, d), lambda i, j: (i, j, 0)),
                  pl.BlockSpec(w_gate.shape, lambda i, j: (0, 0))],
        out_specs=pl.BlockSpec((1, tm, n), lambda i, j: (i, j, 0)),
        out_shape=jax.ShapeDtypeStruct((b, s, n), F32),
        compiler_params=_cparams(("arbitrary", "arbitrary")),
        name="proj_gate",
    )(u, w_gate)


def _pool_segment(a, w_bd, scale):
    n = a.shape[0]
    t = lax.broadcasted_iota(jnp.int32, a.shape, 0)
    grp = lax.broadcasted_iota(jnp.int32, a.shape, 1) // POOL_GROUP_DIM

    def up(x, k):
        return jnp.where(t < n - k, pltpu.roll(x, n - k, 0), 0.0)

    def down(x, k):
        return jnp.where(t >= k, pltpu.roll(x, k, 0), 0.0)

    ahead = a
    behind = down(a, 1)
    wsum = jnp.zeros_like(a)
    half = jnp.zeros_like(t)
    for g, h in enumerate(POOL_HALF_WINDOWS):
        if g > 0:
            ahead = ahead + up(ahead, h // 2)
            behind = behind + down(behind, h // 2)
        wsum = jnp.where(grp == g, ahead + behind, wsum)
        half = jnp.where(grp == g, h, half)
    cnt = jnp.minimum(t + half, n) - jnp.maximum(t - half, 0)
    mean = wsum / cnt.astype(F32)
    out = jnp.dot((mean - a).astype(BF16), w_bd, preferred_element_type=F32)
    return out * scale


def _pool_body(a_ref, w_ref, sc_ref, o_ref, *, n_lat):
    w_bd = w_ref[...]
    scale = sc_ref[...]
    s = a_ref.shape[1]
    o_ref[0, :n_lat] = _pool_segment(a_ref[0, :n_lat], w_bd, scale).astype(o_ref.dtype)
    o_ref[0, n_lat:] = _pool_segment(a_ref[0, n_lat:s], w_bd, scale).astype(o_ref.dtype)


def _pool(pool_in, w_bd, scale, n_lat):
    b, s, w = pool_in.shape
    return pl.pallas_call(
        functools.partial(_pool_body, n_lat=n_lat),
        grid=(b,),
        in_specs=[pl.BlockSpec((1, s, w), lambda i: (i, 0, 0)),
                  pl.BlockSpec((w, w), lambda i: (0, 0)),
                  pl.BlockSpec((1, w), lambda i: (0, 0))],
        out_specs=pl.BlockSpec((1, s, w), lambda i: (i, 0, 0)),
        out_shape=jax.ShapeDtypeStruct((b, s, w), BF16),
        compiler_params=_cparams(("arbitrary",)),
        name="pool",
    )(pool_in, w_bd, scale)


def _na_body(q_ref, k_ref, v_ref, bias_ref, o_ref, *, n_lat, n_rows):
    r = pl.program_id(1)
    q = q_ref[0]
    head = lax.broadcasted_iota(jnp.int32, q.shape, 1) // NA_HEAD_DIM
    zero = jnp.zeros_like(q)
    q4 = jnp.concatenate([jnp.where(head == h, q, zero) for h in range(NA_HEADS)], axis=0)
    s_tot = k_ref.shape[1]
    kc = k_ref[0, n_lat:s_tot, :]
    vc = v_ref[0, n_lat:s_tot, :]
    nt = (((1,), (1,)), ((), ()))
    s_ctx = lax.dot_general(q4, kc, nt, preferred_element_type=F32)

    def finish(o4, denom):
        o4 = o4 / denom
        out = jnp.zeros((GRID_W, NA_WIDTH), F32)
        for h in range(NA_HEADS):
            out = out + jnp.where(head == h, o4[h * GRID_W:(h + 1) * GRID_W], 0.0)
        o_ref[0] = out.astype(o_ref.dtype)

    @pl.when(r < n_rows)
    def _():
        wr = NA_WIN_ROWS
        rs = jnp.clip(r - wr // 2, 0, n_rows - wr)
        start = pl.multiple_of(rs * GRID_W, GRID_W)
        kw = k_ref[0, pl.ds(start, wr * GRID_W), :]
        vw = v_ref[0, pl.ds(start, wr * GRID_W), :]
        s_loc = lax.dot_general(q4, kw, nt, preferred_element_type=F32) + bias_ref[0]
        m = jnp.maximum(jnp.max(s_loc, axis=-1, keepdims=True), jnp.max(s_ctx, axis=-1, keepdims=True))
        p_loc = jnp.exp(s_loc - m)
        p_ctx = jnp.exp(s_ctx - m)
        denom = jnp.sum(p_loc, axis=-1, keepdims=True) + jnp.sum(p_ctx, axis=-1, keepdims=True)
        o4 = (jnp.dot(p_loc.astype(BF16), vw, preferred_element_type=F32)
              + jnp.dot(p_ctx.astype(BF16), vc, preferred_element_type=F32))
        finish(o4, denom)

    @pl.when(r >= n_rows)
    def _():
        m = jnp.max(s_ctx, axis=-1, keepdims=True)
        p_ctx = jnp.exp(s_ctx - m)
        denom = jnp.sum(p_ctx, axis=-1, keepdims=True)
        finish(jnp.dot(p_ctx.astype(BF16), vc, preferred_element_type=F32), denom)


def _na_bias_classes(rpb):
    col = np.arange(GRID_W)
    col_start = np.clip(col - NA_WIN_COLS // 2, 0, GRID_W - NA_WIN_COLS)
    in_window = (col[None, :] >= col_start[:, None]) & (col[None, :] < col_start[:, None] + NA_WIN_COLS)
    col_off = np.clip(col[None, :] - col[:, None], -(NA_WIN_COLS - 1), NA_WIN_COLS - 1) + NA_WIN_COLS - 1
    n_off = 2 * NA_WIN_COLS - 1
    pick = (col_off[None, :, :] == np.arange(n_off)[:, None, None]).astype(np.float32)
    by_col = jnp.einsum('hro,oqk->hrqk', rpb.astype(F32), jnp.asarray(pick), precision=HIGHEST)
    by_col = jnp.where(in_window[None, None], by_col, -jnp.inf)
    b = jnp.stack([by_col[:, o:o + NA_WIN_ROWS] for o in range(NA_WIN_ROWS)], axis=0)
    b = b.transpose(0, 1, 3, 2, 4)
    return b.reshape(NA_WIN_ROWS, NA_HEADS * GRID_W, NA_WIN_ROWS * GRID_W)


def _na(na_qkv, bias_cls, n_lat):
    b, s, _ = na_qkv.shape
    n_rows = n_lat // GRID_W
    n_q = s // GRID_W
    wr = NA_WIN_ROWS

    def bias_idx(i, r):
        cls = jnp.where(r < wr // 2, wr - 1 - r,
                        jnp.where(r > n_rows - wr // 2, n_rows - 1 - r, wr // 2 - 1))
        return (jnp.clip(cls, 0, wr - 1), 0, 0)

    return pl.pallas_call(
        functools.partial(_na_body, n_lat=n_lat, n_rows=n_rows),
        grid=(b, n_q),
        in_specs=[pl.BlockSpec((1, GRID_W, NA_WIDTH), lambda i, r: (i, r, 0)),
                  pl.BlockSpec((1, s, NA_WIDTH), lambda i, r: (i, 0, 1)),
                  pl.BlockSpec((1, s, NA_WIDTH), lambda i, r: (i, 0, 2)),
                  pl.BlockSpec((1,) + bias_cls.shape[1:], bias_idx)],
        out_specs=pl.BlockSpec((1, GRID_W, NA_WIDTH), lambda i, r: (i, r, 0)),
        out_shape=jax.ShapeDtypeStruct((b, s, NA_WIDTH), BF16),
        compiler_params=_cparams(("arbitrary", "arbitrary")),
        name="nbr_attention",
    )(na_qkv, na_qkv, na_qkv, bias_cls)


def _rope_tables(n_lat):
    half = ML_HEAD_DIM // 2
    nf = half // 2
    inv = ROPE_THETA ** (-np.arange(nf, dtype=np.float64) / nf)
    pos = np.arange(n_lat)
    ang = np.concatenate([(pos // GRID_W)[:, None] * inv[None, :]] * 2
                         + [(pos % GRID_W)[:, None] * inv[None, :]] * 2, axis=1)
    lane = np.arange(ML_HEAD_DIM)
    second = (lane % half) >= nf
    cos = np.cos(ang)
    sin = np.sin(ang)
    sin_from_lo = np.where(second[None, :], sin, 0.0)
    sin_from_hi = np.where(second[None, :], 0.0, -sin)
    return (jnp.asarray(cos, F32), jnp.asarray(sin_from_lo, F32), jnp.asarray(sin_from_hi, F32))


def _conv_silu(x, w):
    n = x.shape[0]
    t = lax.broadcasted_iota(jnp.int32, x.shape, 0)
    prev = jnp.where(t >= 1, pltpu.roll(x, 1, 0), 0.0)
    nxt = jnp.where(t < n - 1, pltpu.roll(x, n - 1, 0), 0.0)
    return _silu(w[0:1] * prev + w[1:2] * x + w[2:3] * nxt)


def _rope(x, cos, sin_lo, sin_hi):
    nf = ML_HEAD_DIM // 4
    return x * cos + pltpu.roll(x, nf, 1) * sin_lo + pltpu.roll(x, ML_HEAD_DIM - nf, 1) * sin_hi


def _ml_prep_body(xq_ref, xk_ref, wq_ref, wk_ref, cos_ref, slo_ref, shi_ref, q_ref, kt_ref, *, n_lat):
    s = xq_ref.shape[1]
    wq = wq_ref[...]
    wk = wk_ref[...]
    cos, slo, shi = cos_ref[...], slo_ref[...], shi_ref[...]
    kscale = ML_HEAD_DIM ** -0.5
    for lo, hi, rotary in ((0, n_lat, True), (n_lat, s, False)):
        q = _conv_silu(xq_ref[0, lo:hi], wq)
        k = _conv_silu(xk_ref[0, lo:hi], wk) * kscale
        if rotary:
            q = _rope(q, cos, slo, shi)
            k = _rope(k, cos, slo, shi)
        q_ref[0, lo:hi] = q.astype(q_ref.dtype)
        for j in range((hi - lo) // ML_CHUNK):
            kt_ref[0, 0, lo // ML_CHUNK + j] = k[j * ML_CHUNK:(j + 1) * ML_CHUNK].T.astype(kt_ref.dtype)


def _ml_prep(mlqk, conv_w, rope, n_lat):
    b, s, _ = mlqk.shape
    hd = ML_HEAD_DIM
    nc = s // ML_CHUNK
    tab = pl.BlockSpec((n_lat, hd), lambda i, h: (0, 0))
    return pl.pallas_call(
        functools.partial(_ml_prep_body, n_lat=n_lat),
        grid=(b, ML_HEADS),
        in_specs=[pl.BlockSpec((1, s, hd), lambda i, h: (i, 0, h)),
                  pl.BlockSpec((1, s, hd), lambda i, h: (i, 0, ML_HEADS + h)),
                  pl.BlockSpec((3, hd), lambda i, h: (0, h)),
                  pl.BlockSpec((3, hd), lambda i, h: (0, ML_HEADS + h)),
                  tab, tab, tab],
        out_specs=[pl.BlockSpec((1, s, hd), lambda i, h: (i, 0, h)),
                   pl.BlockSpec((1, 1, nc, hd, ML_CHUNK), lambda i, h: (i, h, 0, 0, 0))],
        out_shape=[jax.ShapeDtypeStruct((b, s, ML_WIDTH), BF16),
                   jax.ShapeDtypeStruct((b, ML_HEADS, nc, hd, ML_CHUNK), BF16)],
        compiler_params=_cparams(("arbitrary", "arbitrary")),
        name="mlstm_prep",
    )(mlqk, mlqk, conv_w, conv_w, *rope)


def _log_sigmoid(x):
    return jnp.minimum(x, 0.0) - jnp.log1p(jnp.exp(-jnp.abs(x)))


def _ml_gates_body(gc_ref, gr_ref, bc_ref, br_ref, oc_ref, or_ref):
    nc = gr_ref.shape[1]
    c = ML_CHUNK
    ii = lax.broadcasted_iota(jnp.int32, (c, c), 0)
    jj = lax.broadcasted_iota(jnp.int32, (c, c), 1)
    lower = (ii >= jj).astype(F32)
    upper = (ii <= jj).astype(F32)
    lane_kind = lax.broadcasted_iota(jnp.int32, (c, LANES), 1) // ML_HEADS
    row_kind = lax.broadcasted_iota(jnp.int32, (4 * ML_HEADS, c), 0) // ML_HEADS
    for j in range(nc):
        g = gc_ref[0, j * c:(j + 1) * c, :] + bc_ref[...]
        lf = _log_sigmoid(g)
        pre = jnp.dot(lower, lf, precision=HIGHEST, preferred_element_type=F32)
        suf = jnp.dot(upper, lf, precision=HIGHEST, preferred_element_type=F32)
        oc_ref[0, j * c:(j + 1) * c, :] = jnp.where(lane_kind == 1, pre, jnp.where(lane_kind == 3, suf, g))
        gr = gr_ref[0, j] + br_ref[...]
        lfr = _log_sigmoid(gr)
        pre_r = jnp.dot(lfr, upper, precision=HIGHEST, preferred_element_type=F32)
        suf_r = jnp.dot(lfr, lower, precision=HIGHEST, preferred_element_type=F32)
        or_ref[0, j] = jnp.where(row_kind == 1, pre_r, jnp.where(row_kind == 3, suf_r, gr))


def _ml_gates(mlg, gate_bias):
    b, s, _ = mlg.shape
    nc = s // ML_CHUNK
    ng = 4 * ML_HEADS
    g_rows = mlg[:, :, :ng].reshape(b, nc, ML_CHUNK, ng).transpose(0, 1, 3, 2)
    bias_c = jnp.zeros((1, LANES), F32).at[0, :ng].set(gate_bias)
    bias_r = gate_bias.reshape(ng, 1)
    return pl.pallas_call(
        _ml_gates_body,
        grid=(b,),
        in_specs=[pl.BlockSpec((1, s, LANES), lambda i: (i, 0, 0)),
                  pl.BlockSpec((1, nc, ng, ML_CHUNK), lambda i: (i, 0, 0, 0)),
                  pl.BlockSpec((1, LANES), lambda i: (0, 0)),
                  pl.BlockSpec((ng, 1), lambda i: (0, 0))],
        out_specs=[pl.BlockSpec((1, s, LANES), lambda i: (i, 0, 0)),
                   pl.BlockSpec((1, nc, ng, ML_CHUNK), lambda i: (i, 0, 0, 0))],
        out_shape=[jax.ShapeDtypeStruct((b, s, LANES), F32),
                   jax.ShapeDtypeStruct((b, nc, ng, ML_CHUNK), F32)],
        compiler_params=_cparams(("arbitrary",)),
        name="mlstm_gates",
    )(mlg, g_rows, bias_c, bias_r)


def _ml_step(q, kt, v_aug, ig_r, b_c, b_r, state, m, *, forward):
    c = ML_CHUNK
    d = ML_HEAD_DIM
    ti = lax.broadcasted_iota(jnp.int32, (c, c), 0)
    si = lax.broadcasted_iota(jnp.int32, (c, c), 1)
    allowed = (si <= ti) if forward else (si >= ti)
    a_r = ig_r - b_r
    log_d = jnp.where(allowed, b_c + a_r, -jnp.inf)
    inter = b_c + m
    m_t = jnp.maximum(jnp.max(log_d, axis=-1, keepdims=True), inter)
    qk = jnp.dot(q, kt, preferred_element_type=F32)
    s_mat = (qk * jnp.exp(log_d - m_t)).astype(BF16)
    w_inter = jnp.exp(inter - m_t)
    tot = (jnp.dot(s_mat, v_aug, preferred_element_type=F32)
           + w_inter * jnp.dot(q, state.astype(BF16), preferred_element_type=F32))
    num = tot[:, :d]
    den = tot[:, d:d + 1]
    h = num / jnp.maximum(jnp.abs(den), jnp.exp(-m_t))
    b_end = b_r[:, c - 1:c] if forward else b_r[:, 0:1]
    log_w = b_end + a_r
    m_new = jnp.maximum(b_end + m, jnp.max(log_w, axis=-1, keepdims=True))
    w_r = jnp.exp(log_w - m_new)
    decay = jnp.exp(b_end + m - m_new)
    kw = (kt.astype(F32) * w_r).astype(BF16)
    state_new = decay * state + jnp.dot(kw, v_aug, preferred_element_type=F32)
    return h, state_new, m_new


def _ml_scan_body(q_ref, kt_ref, v_ref, gc_ref, gr_ref, o_ref, nw_ref, out_ref, hsum_ref, *, n_lat):
    c = ML_CHUNK
    d = ML_HEAD_DIM
    s = q_ref.shape[1]
    nc_lat = n_lat // c
    nc_all = s // c
    ones_col = (lax.broadcasted_iota(jnp.int32, (c, d), 1) == 0).astype(BF16)
    nw = nw_ref[...]

    def run(j, state, m, forward):
        row0 = j * c if isinstance(j, int) else pl.multiple_of(j * c, c)
        q = q_ref[0, pl.ds(row0, c), :]
        v_aug = jnp.concatenate([v_ref[0, pl.ds(row0, c), :], ones_col], axis=1)
        gc = gc_ref[0, 0, pl.ds(row0, c), :]
        gr = gr_ref[0, 0, j]
        k0 = 0 if forward else 2
        return _ml_step(q, kt_ref[0, 0, j], v_aug, gr[k0:k0 + 1], gc[:, k0 + 1:k0 + 2], gr[k0 + 1:k0 + 2],
                        state, m, forward=forward)

    def finalize(j, h):
        row0 = j * c if isinstance(j, int) else pl.multiple_of(j * c, c)
        hc = h - jnp.mean(h, axis=-1, keepdims=True)
        hn = hc * lax.rsqrt(jnp.mean(hc * hc, axis=-1, keepdims=True) + LN_EPS)
        o = o_ref[0, pl.ds(row0, c), :]
        out_ref[0, pl.ds(row0, c), :] = (_sigmoid(o) * hn * nw).astype(out_ref.dtype)

    zero_state = jnp.zeros((d, 2 * d), F32)
    m0 = jnp.full((1, 1), ML_M_INIT, F32)
    sf, mf = zero_state, m0
    hc_f = []
    for j in range(nc_lat, nc_all):
        h, sf, mf = run(j, sf, mf, True)
        hc_f.append(h)
    sb, mb = zero_state, m0
    hc_b = {}
    for j in range(nc_all - 1, nc_lat - 1, -1):
        h, sb, mb = run(j, sb, mb, False)
        hc_b[j] = h
    for idx, j in enumerate(range(nc_lat, nc_all)):
        finalize(j, hc_f[idx] + hc_b[j])

    half = nc_lat // 2

    def first_half(i, carry):
        sf, mf, sb, mb = carry
        jb = nc_lat - 1 - i
        hf, sf, mf = run(i, sf, mf, True)
        hb, sb, mb = run(jb, sb, mb, False)
        hsum_ref[pl.ds(pl.multiple_of(i * c, c), c), :] = hf
        hsum_ref[pl.ds(pl.multiple_of(jb * c, c), c), :] = hb
        return sf, mf, sb, mb

    carry = lax.fori_loop(0, half, first_half, (sf, mf, sb, mb))

    def second_half(i, carry):
        sf, mf, sb, mb = carry
        jb = nc_lat - 1 - i
        hf, sf, mf = run(i, sf, mf, True)
        hb, sb, mb = run(jb, sb, mb, False)
        finalize(i, hf + hsum_ref[pl.ds(pl.multiple_of(i * c, c), c), :])
        finalize(jb, hb + hsum_ref[pl.ds(pl.multiple_of(jb * c, c), c), :])
        return sf, mf, sb, mb

    lax.fori_loop(half, nc_lat, second_half, carry)


def _ml_scan(q, kt, mlv, gcol, grow, mlo, norm_w, n_lat):
    b, s, _ = q.shape
    hd = ML_HEAD_DIM
    nc = s // ML_CHUNK
    ng = 4 * ML_HEADS
    gc_h = gcol[:, :, :ng].reshape(b, s, 4, ML_HEADS).transpose(0, 3, 1, 2)
    gr_h = grow.reshape(b, nc, 4, ML_HEADS, ML_CHUNK).transpose(0, 3, 1, 2, 4)
    tok = pl.BlockSpec((1, s, hd), lambda i, h: (i, 0, h))
    return pl.pallas_call(
        functools.partial(_ml_scan_body, n_lat=n_lat),
        grid=(b, ML_HEADS),
        in_specs=[tok,
                  pl.BlockSpec((1, 1, nc, hd, ML_CHUNK), lambda i, h: (i, h, 0, 0, 0)),
                  tok,
                  pl.BlockSpec((1, 1, s, 4), lambda i, h: (i, h, 0, 0)),
                  pl.BlockSpec((1, 1, nc, 4, ML_CHUNK), lambda i, h: (i, h, 0, 0, 0)),
                  tok,
                  pl.BlockSpec((1, hd), lambda i, h: (0, h))],
        out_specs=tok,
        out_shape=jax.ShapeDtypeStruct((b, s, ML_WIDTH), BF16),
        scratch_shapes=[pltpu.VMEM((n_lat, hd), F32)],
        compiler_params=_cparams(("arbitrary", "arbitrary")),
        name="mlstm_scan",
    )(q, kt, mlv, gc_h, gr_h, mlo, norm_w)


def _top2(logits, n_valid):
    lane = lax.broadcasted_iota(jnp.int32, logits.shape, 1).astype(F32)
    neg = jnp.float32(-jnp.inf)
    l = jnp.where(lane < n_valid, logits, neg)
    m1 = jnp.max(l, axis=-1, keepdims=True)
    e1 = jnp.min(jnp.where(l == m1, lane, float(LANES)), axis=-1, keepdims=True)
    l2 = jnp.where(lane == e1, neg, l)
    m2 = jnp.max(l2, axis=-1, keepdims=True)
    e2 = jnp.min(jnp.where(l2 == m2, lane, float(LANES)), axis=-1, keepdims=True)
    x2 = jnp.exp(m2 - m1)
    p1 = 1.0 / (1.0 + x2)
    p2 = x2 / (1.0 + x2)
    return jnp.where(lane == 0, e1, jnp.where(lane == 1, e2, jnp.where(lane == 2, p1, jnp.where(lane == 3, p2, 0.0))))


def _route_ranks(route, valid, cnt_ref):
    tm = route.shape[0]
    lane = lax.broadcasted_iota(jnp.int32, route.shape, 1).astype(F32)
    pick1 = jnp.where(valid, (lane == route[:, 0:1]).astype(F32), 0.0)
    pick2 = jnp.where(valid, (lane == route[:, 1:2]).astype(F32), 0.0)
    both = pick1 + pick2
    before = (lax.broadcasted_iota(jnp.int32, (tm, tm), 0) > lax.broadcasted_iota(jnp.int32, (tm, tm), 1))
    prefix = jnp.dot(before.astype(BF16), both.astype(BF16), preferred_element_type=F32) + cnt_ref[...]
    rank1 = jnp.sum(prefix * pick1, axis=-1, keepdims=True)
    rank2 = jnp.sum(prefix * pick2, axis=-1, keepdims=True)
    cnt_ref[...] += jnp.sum(both, axis=0, keepdims=True)
    return jnp.where(lane == 4, rank1, jnp.where(lane == 5, rank2, route))


def _merge_body(*refs, n_lat, alpha, moe, n_rows_tok):
    (a_ref, n_ref, m_ref, g_ref, x_ref, wbp_ref, wbn_ref, wbm_ref, wo_ref,
     g1l_ref, g1c_ref, shl_ref, shc_ref, scl_ref, scc_ref, lng_ref, lnb_ref) = refs[:17]
    if moe:
        rw_ref, rb_ref, x_out, tok_out, route_out, cnt_out, cnt_ref = refs[17:]
    else:
        x_out, tok_out = refs[17:]
    tm = x_ref.shape[1]
    d = x_ref.shape[2]
    is_ctx = _is_ctx_rows(pl.program_id(1), tm, n_lat)
    g = g_ref[0]
    y = (_sigmoid(g[:, :d]) * jnp.dot(a_ref[0], wbp_ref[...], preferred_element_type=F32)
         + _sigmoid(g[:, d:2 * d]) * jnp.dot(n_ref[0], wbn_ref[...], preferred_element_type=F32)
         + _sigmoid(g[:, 2 * d:]) * jnp.dot(m_ref[0], wbm_ref[...], preferred_element_type=F32))
    y = jnp.dot(y.astype(BF16), wo_ref[...], preferred_element_type=F32)
    z = alpha * x_ref[0] + _tile_mod(g1l_ref, g1c_ref, is_ctx) * y
    xn = _ln_plain(z) * lng_ref[...] + lnb_ref[...]
    x_out[0] = xn
    tok = _ln_plain(xn) * (1.0 + _tile_mod(scl_ref, scc_ref, is_ctx)) + _tile_mod(shl_ref, shc_ref, is_ctx)
    tok_out[0] = tok.astype(tok_out.dtype)
    if moe:
        @pl.when((pl.program_id(0) == 0) & (pl.program_id(1) == 0))
        def _():
            cnt_ref[...] = jnp.zeros_like(cnt_ref)

        lane = lax.broadcasted_iota(jnp.int32, (tm, LANES), 1)
        logits = jnp.zeros((tm, LANES), F32)
        for e in range(N_EXPERTS):
            col = jnp.sum(tok * rw_ref[e:e + 1, :], axis=-1, keepdims=True)
            logits = jnp.where(lane == e, col, logits)
        valid = (pl.program_id(1) * tm + lax.broadcasted_iota(jnp.int32, (tm, 1), 0)) < n_rows_tok
        route_out[0] = _route_ranks(_top2(logits + rb_ref[...], N_EXPERTS), valid, cnt_ref)
        cnt_out[...] = cnt_ref[...]


def _merge(a, n, m, gates, x_all, w_bp, w_bn, w_bm, w_o, g1, sh2, sc2, ln_g, ln_b, n_lat, alpha,
           router=None, n_rows_tok=0):
    b, s, d = x_all.shape
    tm = TM_MERGE
    moe = router is not None

    def tok(w):
        return pl.BlockSpec((1, tm, w), lambda i, j: (i, j, 0))

    def full(arr):
        return pl.BlockSpec(arr.shape, lambda i, j: (0,) * arr.ndim)

    in_specs = ([tok(a.shape[2]), tok(n.shape[2]), tok(m.shape[2]), tok(gates.shape[2]), tok(d),
                 full(w_bp), full(w_bn), full(w_bm), full(w_o)]
                + _row_specs(d, b) * 3 + [full(ln_g), full(ln_b)])
    args = [a, n, m, gates, x_all, w_bp, w_bn, w_bm, w_o, g1, g1, sh2, sh2, sc2, sc2, ln_g, ln_b]
    out_specs = [tok(d), tok(d)]
    out_shape = [jax.ShapeDtypeStruct((b, s, d), F32), jax.ShapeDtypeStruct((b, s, d), F32 if moe else BF16)]
    scratch = []
    if moe:
        in_specs += [full(router[0]), full(router[1])]
        args += list(router)
        out_specs += [tok(LANES), pl.BlockSpec((1, LANES), lambda i, j: (0, 0))]
        out_shape += [jax.ShapeDtypeStruct((b, s, LANES), F32), jax.ShapeDtypeStruct((1, LANES), F32)]
        scratch = [pltpu.VMEM((1, LANES), F32)]
    return pl.pallas_call(
        functools.partial(_merge_body, n_lat=n_lat, alpha=alpha, moe=moe, n_rows_tok=n_rows_tok),
        grid=(b, s // tm),
        in_specs=in_specs,
        out_specs=out_specs,
        out_shape=out_shape,
        scratch_shapes=scratch,
        compiler_params=_cparams(("arbitrary", "arbitrary")),
        name="merge",
    )(*args)


def _finish_rows(f, x, g2, lng, lnb, alpha):
    return _ln_plain(alpha * x + g2 * f) * lng + lnb


def _ffn_body(*refs, n_lat, alpha, with_next):
    (t_ref, x_ref, w1_ref, w3_ref, w2_ref, g2l_ref, g2c_ref, lng_ref, lnb_ref) = refs[:9]
    if with_next:
        shl_ref, shc_ref, scl_ref, scc_ref, x_out, u_out, acc_ref = refs[9:]
    else:
        x_out, acc_ref = refs[9:]
    f = pl.program_id(2)
    t = t_ref[0]
    h1 = jnp.dot(t, w1_ref[...], preferred_element_type=F32)
    h3 = jnp.dot(t, w3_ref[...], preferred_element_type=F32)
    contrib = jnp.dot((_silu(h1) * h3).astype(BF16), w2_ref[...], preferred_element_type=F32)

    @pl.when(f == 0)
    def _():
        acc_ref[...] = contrib

    @pl.when(f > 0)
    def _():
        acc_ref[...] += contrib

    @pl.when(f == pl.num_programs(2) - 1)
    def _():
        tm = x_ref.shape[1]
        is_ctx = _is_ctx_rows(pl.program_id(1), tm, n_lat)
        xn = _finish_rows(acc_ref[...], x_ref[0], _tile_mod(g2l_ref, g2c_ref, is_ctx), lng_ref[...], lnb_ref[...], alpha)
        x_out[0] = xn
        if with_next:
            u = _ln_plain(xn) * (1.0 + _tile_mod(scl_ref, scc_ref, is_ctx)) + _tile_mod(shl_ref, shc_ref, is_ctx)
            u_out[0] = u.astype(u_out.dtype)


def _ffn(tok, x_all, w1, w3, w2, g2, ln_g, ln_b, n_lat, alpha, nxt=None):
    b, s, d = x_all.shape
    dff = w1.shape[1]
    tm, tf = TM_TOKEN, TF_FFN
    with_next = nxt is not None

    def tokspec():
        return pl.BlockSpec((1, tm, d), lambda i, j, f: (i, j, 0))

    def rows():
        return [pl.BlockSpec((1, 1, d), lambda i, j, f: (i, 0, 0)),
                pl.BlockSpec((1, 1, d), lambda i, j, f: (b, 0, 0))]

    vec = pl.BlockSpec((1, d), lambda i, j, f: (0, 0))
    in_specs = [tokspec(), tokspec(),
                pl.BlockSpec((d, tf), lambda i, j, f: (0, f)),
                pl.BlockSpec((d, tf), lambda i, j, f: (0, f)),
                pl.BlockSpec((tf, d), lambda i, j, f: (f, 0))] + rows() + [vec, vec]
    args = [tok, x_all, w1, w3, w2, g2, g2, ln_g, ln_b]
    out_specs = [tokspec()]
    out_shape = [jax.ShapeDtypeStruct((b, s, d), F32)]
    if with_next:
        in_specs += rows() + rows()
        args += [nxt[0], nxt[0], nxt[1], nxt[1]]
        out_specs.append(tokspec())
        out_shape.append(jax.ShapeDtypeStruct((b, s, d), BF16))
    res = pl.pallas_call(
        functools.partial(_ffn_body, n_lat=n_lat, alpha=alpha, with_next=with_next),
        grid=(b, s // tm, dff // tf),
        in_specs=in_specs,
        out_specs=out_specs,
        out_shape=out_shape,
        scratch_shapes=[pltpu.VMEM((tm, d), F32)],
        compiler_params=_cparams(("arbitrary", "arbitrary", "arbitrary")),
        name="ffn_dense",
    )(*args)
    return res if with_next else (res[0], None)


def _moe_tables(route, counts, n_rows_tok):
    r = MOE_ROWS
    tm = TM_MOE
    b = route.shape[0]
    cnt = counts[0, :N_EXPERTS].astype(jnp.int32)
    nblk = (cnt + r - 1) // r
    bend = jnp.cumsum(nblk)
    bstart = bend - nblk
    total = bend[-1]
    n_blocks = -(-(b * n_rows_tok * TOP_K) // r) + N_EXPERTS
    part = route[:, :n_rows_tok]
    experts = jnp.arange(N_EXPERTS, dtype=F32)
    seg_row0 = jnp.sum((part[:, :, 0:TOP_K, None] == experts).astype(jnp.int32) * (bstart * r), axis=-1)
    dest = seg_row0 + part[:, :, 4:4 + TOP_K].astype(jnp.int32)
    nt = n_rows_tok // tm
    dest_tiles = dest.reshape(b, nt, tm, TOP_K).transpose(0, 1, 3, 2).reshape(b * nt, 1, TOP_K * tm)
    blk = jnp.arange(n_blocks, dtype=jnp.int32)
    e_of = jnp.sum((blk[:, None] >= bend[None, :]).astype(jnp.int32), axis=1)
    e_last = jnp.sum((total - 1 >= bend).astype(jnp.int32))
    active = (blk < total).astype(jnp.int32)
    block_e = jnp.minimum(e_of, e_last)
    spare = total + jnp.arange(N_EXPERTS, dtype=jnp.int32)
    zero_blk = jnp.concatenate([bend - 1, spare])
    zero_on = jnp.concatenate([nblk > 0, spare < n_blocks]).astype(jnp.int32)
    zero_blk = jnp.clip(zero_blk, 0, n_blocks - 1)
    return dest_tiles, zero_blk, zero_on, block_e, active


def _dispatch_body(zblk_ref, zon_ref, dest_ref, tok_hbm, xs_hbm, zbuf, zsem, dsem, *, tiles_per_batch, seq):
    i = pl.program_id(0)
    tm = dest_ref.shape[2] // TOP_K
    r = zbuf.shape[0]

    def zero_copy(z):
        start = pl.multiple_of(zblk_ref[z] * r, r)
        return pltpu.make_async_copy(zbuf, xs_hbm.at[pl.ds(start, r)], zsem)

    @pl.when(i == 0)
    def _():
        zbuf[...] = jnp.zeros_like(zbuf)
        for z in range(2 * N_EXPERTS):
            @pl.when(zon_ref[z] > 0)
            def _():
                zero_copy(z).start()
        for z in range(2 * N_EXPERTS):
            @pl.when(zon_ref[z] > 0)
            def _():
                zero_copy(z).wait()

    src0 = (i // tiles_per_batch) * seq + (i % tiles_per_batch) * tm

    def row_copy(j, k):
        return pltpu.make_async_copy(tok_hbm.at[pl.ds(src0 + j, 1)],
                                     xs_hbm.at[pl.ds(dest_ref[0, 0, k * tm + j], 1)], dsem)

    def wait_tile():
        for _ in range(TOP_K):
            pltpu.make_async_copy(tok_hbm.at[pl.ds(0, tm)], xs_hbm.at[pl.ds(0, tm)], dsem).wait()

    def issue(j, c):
        for k in range(TOP_K):
            row_copy(j, k).start()
        return c

    lax.fori_loop(0, tm, issue, 0, unroll=8)

    @pl.when(i > 0)
    def _():
        wait_tile()

    @pl.when(i == pl.num_programs(0) - 1)
    def _():
        wait_tile()


def _moe_dispatch(tok_flat, zero_blk, zero_on, dest_tiles, n_blocks, n_rows_tok, seq):
    d = tok_flat.shape[1]
    tm = TM_MOE
    n_tiles = dest_tiles.shape[0]
    grid_spec = pltpu.PrefetchScalarGridSpec(
        num_scalar_prefetch=2,
        grid=(n_tiles,),
        in_specs=[pl.BlockSpec((1, 1, TOP_K * tm), lambda i, zb, zo: (i, 0, 0), memory_space=pltpu.SMEM),
                  pl.BlockSpec(memory_space=pl.ANY)],
        out_specs=pl.BlockSpec(memory_space=pl.ANY),
        scratch_shapes=[pltpu.VMEM((MOE_ROWS, d), F32),
                        pltpu.SemaphoreType.DMA(()), pltpu.SemaphoreType.DMA(())],
    )
    return pl.pallas_call(
        functools.partial(_dispatch_body, tiles_per_batch=n_rows_tok // tm, seq=seq),
        grid_spec=grid_spec,
        out_shape=jax.ShapeDtypeStruct((n_blocks * MOE_ROWS, d), F32),
        compiler_params=_cparams(("arbitrary",)),
        name="moe_dispatch",
    )(zero_blk, zero_on, dest_tiles, tok_flat)


def _experts_body(be_ref, act_ref, x_ref, w1_ref, w3_ref, w2_ref, y_ref, xb_ref):
    i = pl.program_id(0)
    f = pl.program_id(1)

    @pl.when((act_ref[i] == 0) & (f == 0))
    def _():
        y_ref[...] = jnp.zeros_like(y_ref)

    @pl.when(act_ref[i] > 0)
    def _():
        @pl.when(f == 0)
        def _():
            xb_ref[...] = x_ref[...].astype(BF16)

        x = xb_ref[...]
        h1 = jnp.dot(x, w1_ref[0].astype(BF16), preferred_element_type=F32)
        h3 = jnp.dot(x, w3_ref[0].astype(BF16), preferred_element_type=F32)
        contrib = jnp.dot((_silu(h1) * h3).astype(BF16), w2_ref[0].astype(BF16), preferred_element_type=F32)

        @pl.when(f == 0)
        def _():
            y_ref[...] = contrib

        @pl.when(f > 0)
        def _():
            y_ref[...] += contrib


def _moe_experts(xs, block_e, active, w1, w3, w2):
    d = xs.shape[1]
    dff = w1.shape[2]
    r, tf = MOE_ROWS, TF_FFN
    nf = dff // tf
    n_blocks = block_e.shape[0]

    def f_eff(i, f, act):
        return jnp.where(act[i] > 0, f, nf - 1)

    grid_spec = pltpu.PrefetchScalarGridSpec(
        num_scalar_prefetch=2,
        grid=(n_blocks, nf),
        in_specs=[pl.BlockSpec((r, d), lambda i, f, be, act: (i, 0)),
                  pl.BlockSpec((1, d, tf), lambda i, f, be, act: (be[i], 0, f_eff(i, f, act))),
                  pl.BlockSpec((1, d, tf), lambda i, f, be, act: (be[i], 0, f_eff(i, f, act))),
                  pl.BlockSpec((1, tf, d), lambda i, f, be, act: (be[i], f_eff(i, f, act), 0))],
        out_specs=pl.BlockSpec((r, d), lambda i, f, be, act: (i, 0)),
        scratch_shapes=[pltpu.VMEM((r, d), BF16)],
    )
    return pl.pallas_call(
        _experts_body,
        grid_spec=grid_spec,
        out_shape=jax.ShapeDtypeStruct(xs.shape, F32),
        compiler_params=_cparams(("arbitrary", "arbitrary")),
        name="moe_experts",
    )(block_e, active, xs, w1, w3, w2)


def _combine_body(*refs, n_lat, alpha, with_next):
    (dest_ref, ys_hbm, p_ref, x_ref, g2l_ref, g2c_ref, lng_ref, lnb_ref) = refs[:8]
    if with_next:
        shl_ref, shc_ref, scl_ref, scc_ref, x_out, u_out, ybuf, sem = refs[8:]
    else:
        x_out, ybuf, sem = refs[8:]
    tm = x_ref.shape[1]

    def row_copy(j, k):
        return pltpu.make_async_copy(ys_hbm.at[pl.ds(dest_ref[0, 0, k * tm + j], 1)],
                                     ybuf.at[k, pl.ds(j, 1)], sem)

    def issue(j, c):
        for k in range(TOP_K):
            row_copy(j, k).start()
        return c

    lax.fori_loop(0, tm, issue, 0, unroll=8)
    for k in range(TOP_K):
        pltpu.make_async_copy(ys_hbm.at[pl.ds(0, tm)], ybuf.at[k], sem).wait()

    is_ctx = _is_ctx_rows(pl.program_id(1), tm, n_lat)
    p = p_ref[0]
    f = p[:, 2:3] * ybuf[0] + p[:, 3:4] * ybuf[1]
    xn = _finish_rows(f, x_ref[0], _tile_mod(g2l_ref, g2c_ref, is_ctx), lng_ref[...], lnb_ref[...], alpha)
    x_out[0] = xn
    if with_next:
        u = _ln_plain(xn) * (1.0 + _tile_mod(scl_ref, scc_ref, is_ctx)) + _tile_mod(shl_ref, shc_ref, is_ctx)
        u_out[0] = u.astype(u_out.dtype)


def _combine(ys, dest_tiles, route, x_all, g2, ln_g, ln_b, n_lat, alpha, n_tok_rows, nxt=None):
    b, s, d = x_all.shape
    tm = TM_MOE
    nt = n_tok_rows // tm
    with_next = nxt is not None

    def tok(w):
        return pl.BlockSpec((1, tm, w), lambda i, j: (i, j, 0))

    vec = pl.BlockSpec((1, d), lambda i, j: (0, 0))
    in_specs = [pl.BlockSpec((1, 1, TOP_K * tm), lambda i, j: (i * nt + j, 0, 0), memory_space=pltpu.SMEM),
                pl.BlockSpec(memory_space=pl.ANY), tok(LANES), tok(d)] + _row_specs(d, b) + [vec, vec]
    args = [dest_tiles, ys, route, x_all, g2, g2, ln_g, ln_b]
    out_specs = [tok(d)]
    out_shape = [jax.ShapeDtypeStruct((b, n_tok_rows, d), F32)]
    if with_next:
        in_specs += _row_specs(d, b) * 2
        args += [nxt[0], nxt[0], nxt[1], nxt[1]]
        out_specs.append(tok(d))
        out_shape.append(jax.ShapeDtypeStruct((b, n_tok_rows, d), BF16))
    res = pl.pallas_call(
        functools.partial(_combine_body, n_lat=n_lat, alpha=alpha, with_next=with_next),
        grid=(b, nt),
        in_specs=in_specs,
        out_specs=out_specs,
        out_shape=out_shape,
        scratch_shapes=[pltpu.VMEM((TOP_K, tm, d), F32), pltpu.SemaphoreType.DMA(())],
        compiler_params=_cparams(("arbitrary", "arbitrary")),
        name="moe_combine",
    )(*args)
    return res if with_next else (res[0], None)


def _block_diag(pool_w):
    g, c, _ = pool_w.shape
    out = jnp.zeros((g * c, g * c), pool_w.dtype)
    for i in range(g):
        out = out.at[i * c:(i + 1) * c, i * c:(i + 1) * c].set(pool_w[i])
    return out


def kernel(x, c, ctx, c_ctx, w_mod, b_mod, w_in, ml_gate_bias, ml_conv_w, ml_norm_w, pool_w, pool_scale,
           na_rpb, w_branch_pool, w_branch_na, w_branch_ml, w_out, ln1_g, ln1_b, ln2_g, ln2_b,
           ffn_w1, ffn_w3, ffn_w2, moe_router_w, moe_router_b, moe_w1, moe_w3, moe_w2):
    b, n_lat, d = x.shape
    n_ctx = ctx.shape[1]
    s = n_lat + n_ctx
    depth = w_in.shape[0]
    alpha = (2 * depth) ** 0.25
    off_g = POOL_WIDTH + 3 * NA_WIDTH + 4 * ML_WIDTH
    n_gate_cols = 4 * ML_HEADS

    x_all = jnp.concatenate([x, ctx], axis=1)
    mod = _mod_vectors(c, c_ctx, w_mod, b_mod)

    def mod_part(layer, k):
        return mod[layer, :, k * d:(k + 1) * d].reshape(MOD_ROWS, 1, d)

    rope = _rope_tables(n_lat)
    u = _ln_mod(x_all, mod_part(0, 0), mod_part(0, 1), n_lat)

    for layer in range(depth):
        last = layer == depth - 1
        w_l = w_in[layer]
        w_main = w_l[:, :off_g].astype(BF16)
        w_g = jnp.zeros((d, LANES), BF16).at[:, :n_gate_cols].set(w_l[:, off_g:off_g + n_gate_cols].astype(BF16))
        w_gate = w_l[:, off_g + n_gate_cols:].astype(BF16)
        pool_in, na_qkv, mlqk, mlv, mlo, mlg = _proj_main(u, w_main, w_g)
        gates = _proj_gate(u, w_gate)

        a = _pool(pool_in, _block_diag(pool_w[layer]).astype(BF16), pool_scale[layer].reshape(1, -1), n_lat)
        n = _na(na_qkv, _na_bias_classes(na_rpb[layer]), n_lat)
        q_ml, kt_ml = _ml_prep(mlqk, ml_conv_w[layer], rope, n_lat)
        gcol, grow = _ml_gates(mlg, ml_gate_bias[layer])
        m = _ml_scan(q_ml, kt_ml, mlv, gcol, grow, mlo, ml_norm_w[layer].reshape(1, -1), n_lat)

        is_moe = layer % 2 == 1
        i = layer // 2
        router = None
        if is_moe:
            rw = moe_router_w[i].T
            rb = jnp.zeros((1, LANES), F32).at[0, :N_EXPERTS].set(moe_router_b[i])
            router = (rw, rb)
        n_rows_tok = n_lat if last else s
        merged = _merge(a, n, m, gates, x_all,
                        w_branch_pool[layer].astype(BF16), w_branch_na[layer].astype(BF16),
                        w_branch_ml[layer].astype(BF16), w_out[layer].astype(BF16),
                        mod_part(layer, 2), mod_part(layer, 3), mod_part(layer, 4),
                        ln1_g[layer].reshape(1, d), ln1_b[layer].reshape(1, d), n_lat, alpha, router, n_rows_tok)
        nxt = None if last else (mod_part(layer + 1, 0), mod_part(layer + 1, 1))
        g2 = mod_part(layer, 5)
        lng, lnb = ln2_g[layer].reshape(1, d), ln2_b[layer].reshape(1, d)
        if not is_moe:
            x_mid, tok = merged
            x_all, u = _ffn(tok, x_mid, ffn_w1[i].astype(BF16), ffn_w3[i].astype(BF16), ffn_w2[i].astype(BF16),
                            g2, lng, lnb, n_lat, alpha, nxt)
        else:
            x_mid, tok, route, counts = merged
            x_all, u = _moe_layer(tok, route, counts, x_mid, moe_w1[i], moe_w3[i], moe_w2[i], g2, lng, lnb,
                                  n_lat, alpha, n_rows_tok, nxt)
    return x_all[:, :n_lat]


def _moe_layer(tok, route, counts, x_mid, w1, w3, w2, g2, lng, lnb, n_lat, alpha, n_rows_tok, nxt):
    b, s, d = tok.shape
    dest_tiles, zero_blk, zero_on, block_e, active = _moe_tables(route, counts, n_rows_tok)
    xs = _moe_dispatch(tok.reshape(b * s, d), zero_blk, zero_on, dest_tiles, block_e.shape[0], n_rows_tok, s)
    ys = _moe_experts(xs, block_e, active, w1, w3, w2)
    return _combine(ys, dest_tiles, route, x_mid, g2, lng, lnb, n_lat, alpha, n_rows_tok, nxt)
```

```python
import functools

import numpy as np
import jax
import jax.numpy as jnp
from jax import lax
from jax.experimental import pallas as pl
from jax.experimental.pallas import tpu as pltpu

F32 = jnp.float32
BF16 = jnp.bfloat16
HIGHEST = lax.Precision.HIGHEST

GRID_W = 64
POOL_GROUPS = 4
POOL_GROUP_DIM = 64
POOL_WIDTH = POOL_GROUPS * POOL_GROUP_DIM
POOL_HALF_WINDOWS = (1, 2, 4, 8)
NA_HEADS = 4
NA_HEAD_DIM = 64
NA_WIDTH = NA_HEADS * NA_HEAD_DIM
NA_WIN_ROWS = 8
NA_WIN_COLS = 16
ML_HEADS = 4
ML_HEAD_DIM = 128
ML_WIDTH = ML_HEADS * ML_HEAD_DIM
ML_CHUNK = 128
ML_M_INIT = -1e30
ROPE_THETA = 10000.0
N_EXPERTS = 8
TOP_K = 2
N_BRANCHES = 3
LN_EPS = 1e-6

LANES = 128
MOD_ROWS = 16
VMEM_LIMIT = 56 * 1024 * 1024

TM_TOKEN = 768
TM_MERGE = 384
TF_FFN = 512
MOE_ROWS = 1024
TM_MOE = 256


def _cparams(sem):
    return pltpu.CompilerParams(dimension_semantics=sem, vmem_limit_bytes=VMEM_LIMIT)


def _sigmoid(x):
    return 1.0 / (1.0 + jnp.exp(-x))


def _silu(x):
    return x * _sigmoid(x)


def _ln_plain(x):
    xc = x - jnp.mean(x, axis=-1, keepdims=True)
    return xc * lax.rsqrt(jnp.mean(xc * xc, axis=-1, keepdims=True) + LN_EPS)


def _is_ctx_rows(tile_idx, tm, n_lat):
    row = tile_idx * tm + lax.broadcasted_iota(jnp.int32, (tm, 1), 0)
    return row >= n_lat


def _tile_mod(lat_ref, ctx_ref, is_ctx):
    return jnp.where(is_ctx, ctx_ref[0], lat_ref[0])


def _mod_body(s_ref, w_ref, b_ref, o_ref):
    s = _silu(s_ref[...])
    o_ref[0] = jnp.dot(s, w_ref[0], precision=HIGHEST, preferred_element_type=F32) + b_ref[0]


def _mod_vectors(c, c_ctx, w_mod, b_mod):
    depth, d, d6 = w_mod.shape
    b = c.shape[0]
    s = jnp.zeros((MOD_ROWS, d), F32).at[:b].set(c).at[b].set(c_ctx)
    tn = 1536
    return pl.pallas_call(
        _mod_body,
        grid=(depth, d6 // tn),
        in_specs=[pl.BlockSpec((MOD_ROWS, d), lambda l, j: (0, 0)),
                  pl.BlockSpec((1, d, tn), lambda l, j: (l, 0, j)),
                  pl.BlockSpec((1, 1, tn), lambda l, j: (l, 0, j))],
        out_specs=pl.BlockSpec((1, MOD_ROWS, tn), lambda l, j: (l, 0, j)),
        out_shape=jax.ShapeDtypeStruct((depth, MOD_ROWS, d6), F32),
        compiler_params=_cparams(("arbitrary", "arbitrary")),
        name="mod_vectors",
    )(s, w_mod, b_mod.reshape(depth, 1, d6))


def _ln_mod_body(x_ref, shl_ref, shc_ref, scl_ref, scc_ref, u_ref, *, n_lat):
    tm = x_ref.shape[1]
    is_ctx = _is_ctx_rows(pl.program_id(1), tm, n_lat)
    sh = _tile_mod(shl_ref, shc_ref, is_ctx)
    sc = _tile_mod(scl_ref, scc_ref, is_ctx)
    u_ref[0] = (_ln_plain(x_ref[0]) * (1.0 + sc) + sh).astype(u_ref.dtype)


def _row_specs(d, nb):
    return [pl.BlockSpec((1, 1, d), lambda b, j: (b, 0, 0)),
            pl.BlockSpec((1, 1, d), lambda b, j: (nb, 0, 0))]


def _ln_mod(x_all, shift, scale, n_lat):
    b, s, d = x_all.shape
    tm = TM_TOKEN
    return pl.pallas_call(
        functools.partial(_ln_mod_body, n_lat=n_lat),
        grid=(b, s // tm),
        in_specs=[pl.BlockSpec((1, tm, d), lambda i, j: (i, j, 0))] + _row_specs(d, b) + _row_specs(d, b),
        out_specs=pl.BlockSpec((1, tm, d), lambda i, j: (i, j, 0)),
        out_shape=jax.ShapeDtypeStruct((b, s, d), BF16),
        compiler_params=_cparams(("arbitrary", "arbitrary")),
        name="ln_mod",
    )(x_all, shift, shift, scale, scale)


def _proj_main_body(u_ref, w_ref, wg_ref, pool_ref, na_ref, mlqk_ref, mlv_ref, mlo_ref, mlg_ref):
    u = u_ref[0]

    def seg(lo, hi):
        return jnp.dot(u, w_ref[:, lo:hi], preferred_element_type=F32)

    off_na = POOL_WIDTH
    off_ml = off_na + 3 * NA_WIDTH
    off_mlv = off_ml + 2 * ML_WIDTH
    off_mlo = off_ml + 3 * ML_WIDTH
    pool_ref[0] = seg(0, off_na)
    na_ref[0, :, :NA_WIDTH] = (seg(off_na, off_na + NA_WIDTH) * NA_HEAD_DIM ** -0.5).astype(na_ref.dtype)
    na_ref[0, :, NA_WIDTH:] = seg(off_na + NA_WIDTH, off_ml).astype(na_ref.dtype)
    mlqk_ref[0] = seg(off_ml, off_mlv)
    mlv_ref[0] = seg(off_mlv, off_mlo).astype(mlv_ref.dtype)
    mlo_ref[0] = seg(off_mlo, off_mlo + ML_WIDTH)
    mlg_ref[0] = jnp.dot(u, wg_ref[...], preferred_element_type=F32)


def _proj_main(u, w_main, w_g):
    b, s, d = u.shape
    tm = TM_TOKEN
    widths = (POOL_WIDTH, 3 * NA_WIDTH, 2 * ML_WIDTH, ML_WIDTH, ML_WIDTH, LANES)
    dtypes = (F32, BF16, F32, BF16, F32, F32)
    return pl.pallas_call(
        _proj_main_body,
        grid=(b, s // tm),
        in_specs=[pl.BlockSpec((1, tm, d), lambda i, j: (i, j, 0)),
                  pl.BlockSpec(w_main.shape, lambda i, j: (0, 0)),
                  pl.BlockSpec(w_g.shape, lambda i, j: (0, 0))],
        out_specs=[pl.BlockSpec((1, tm, w), lambda i, j: (i, j, 0)) for w in widths],
        out_shape=[jax.ShapeDtypeStruct((b, s, w), dt) for w, dt in zip(widths, dtypes)],
        compiler_params=_cparams(("arbitrary", "arbitrary")),
        name="proj_main",
    )(u, w_main, w_g)


def _proj_gate_body(u_ref, w_ref, g_ref):
    u = u_ref[0]
    n = w_ref.shape[1]
    step = 1024
    for lo in range(0, n, step):
        g_ref[0, :, lo:lo + step] = jnp.dot(u, w_ref[:, lo:lo + step], preferred_element_type=F32)


def _proj_gate(u, w_gate):
    b, s, d = u.shape
    tm = TM_TOKEN
    n = w_gate.shape[1]
    return pl.pallas_call(
        _proj_gate_body,
        grid=(b, s // tm),
        in_specs=[pl.BlockSpec((1, tm, d), lambda i, j: (i, j, 0)),
                  pl.BlockSpec(w_gate.shape, lambda i, j: (0, 0))],
        out_specs=pl.BlockSpec((1, tm, n), lambda i, j: (i, j, 0)),
        out_shape=jax.ShapeDtypeStruct((b, s, n), F32),
        compiler_params=_cparams(("arbitrary", "arbitrary")),
        name="proj_gate",
    )(u, w_gate)


def _pool_segment(a, w_bd, scale):
    n = a.shape[0]
    t = lax.broadcasted_iota(jnp.int32, a.shape, 0)
    grp = lax.broadcasted_iota(jnp.int32, a.shape, 1) // POOL_GROUP_DIM

    def up(x, k):
        return jnp.where(t < n - k, pltpu.roll(x, n - k, 0), 0.0)

    def down(x, k):
        return jnp.where(t >= k, pltpu.roll(x, k, 0), 0.0)

    ahead = a
    behind = down(a, 1)
    wsum = jnp.zeros_like(a)
    half = jnp.zeros_like(t)
    for g, h in enumerate(POOL_HALF_WINDOWS):
        if g > 0:
            ahead = ahead + up(ahead, h // 2)
            behind = behind + down(behind, h // 2)
        wsum = jnp.where(grp == g, ahead + behind, wsum)
        half = jnp.where(grp == g, h, half)
    cnt = jnp.minimum(t + half, n) - jnp.maximum(t - half, 0)
    mean = wsum / cnt.astype(F32)
    out = jnp.dot((mean - a).astype(BF16), w_bd, preferred_element_type=F32)
    return out * scale


def _pool_body(a_ref, w_ref, sc_ref, o_ref, *, n_lat):
    w_bd = w_ref[...]
    scale = sc_ref[...]
    s = a_ref.shape[1]
    o_ref[0, :n_lat] = _pool_segment(a_ref[0, :n_lat], w_bd, scale).astype(o_ref.dtype)
    o_ref[0, n_lat:] = _pool_segment(a_ref[0, n_lat:s], w_bd, scale).astype(o_ref.dtype)


def _pool(pool_in, w_bd, scale, n_lat):
    b, s, w = pool_in.shape
    return pl.pallas_call(
        functools.partial(_pool_body, n_lat=n_lat),
        grid=(b,),
        in_specs=[pl.BlockSpec((1, s, w), lambda i: (i, 0, 0)),
                  pl.BlockSpec((w, w), lambda i: (0, 0)),
                  pl.BlockSpec((1, w), lambda i: (0, 0))],
        out_specs=pl.BlockSpec((1, s, w), lambda i: (i, 0, 0)),
        out_shape=jax.ShapeDtypeStruct((b, s, w), BF16),
        compiler_params=_cparams(("arbitrary",)),
        name="pool",
    )(pool_in, w_bd, scale)


def _na_body(q_ref, k_ref, v_ref, bias_ref, o_ref, *, n_lat, n_rows):
    r = pl.program_id(1)
    q = q_ref[0]
    head = lax.broadcasted_iota(jnp.int32, q.shape, 1) // NA_HEAD_DIM
    zero = jnp.zeros_like(q)
    q4 = jnp.concatenate([jnp.where(head == h, q, zero) for h in range(NA_HEADS)], axis=0)
    s_tot = k_ref.shape[1]
    kc = k_ref[0, n_lat:s_tot, :]
    vc = v_ref[0, n_lat:s_tot, :]
    nt = (((1,), (1,)), ((), ()))
    s_ctx = lax.dot_general(q4, kc, nt, preferred_element_type=F32)

    def finish(o4, denom):
        o4 = o4 / denom
        out = jnp.zeros((GRID_W, NA_WIDTH), F32)
        for h in range(NA_HEADS):
            out = out + jnp.where(head == h, o4[h * GRID_W:(h + 1) * GRID_W], 0.0)
        o_ref[0] = out.astype(o_ref.dtype)

    @pl.when(r < n_rows)
    def _():
        wr = NA_WIN_ROWS
        rs = jnp.clip(r - wr // 2, 0, n_rows - wr)
        start = pl.multiple_of(rs * GRID_W, GRID_W)
        kw = k_ref[0, pl.ds(start, wr * GRID_W), :]
        vw = v_ref[0, pl.ds(start, wr * GRID_W), :]
        s_loc = lax.dot_general(q4, kw, nt, preferred_element_type=F32) + bias_ref[0]
        m = jnp.maximum(jnp.max(s_loc, axis=-1, keepdims=True), jnp.max(s_ctx, axis=-1, keepdims=True))
        p_loc = jnp.exp(s_loc - m)
        p_ctx = jnp.exp(s_ctx - m)
        denom = jnp.sum(p_loc, axis=-1, keepdims=True) + jnp.sum(p_ctx, axis=-1, keepdims=True)
        o4 = (jnp.dot(p_loc.astype(BF16), vw, preferred_element_type=F32)
              + jnp.dot(p_ctx.astype(BF16), vc, preferred_element_type=F32))
        finish(o4, denom)

    @pl.when(r >= n_rows)
    def _():
        m = jnp.max(s_ctx, axis=-1, keepdims=True)
        p_ctx = jnp.exp(s_ctx - m)
        denom = jnp.sum(p_ctx, axis=-1, keepdims=True)
        finish(jnp.dot(p_ctx.astype(BF16), vc, preferred_element_type=F32), denom)


def _na_bias_classes(rpb):
    col = np.arange(GRID_W)
    col_start = np.clip(col - NA_WIN_COLS // 2, 0, GRID_W - NA_WIN_COLS)
    in_window = (col[None, :] >= col_start[:, None]) & (col[None, :] < col_start[:, None] + NA_WIN_COLS)
    col_off = np.clip(col[None, :] - col[:, None], -(NA_WIN_COLS - 1), NA_WIN_COLS - 1) + NA_WIN_COLS - 1
    n_off = 2 * NA_WIN_COLS - 1
    pick = (col_off[None, :, :] == np.arange(n_off)[:, None, None]).astype(np.float32)
    by_col = jnp.einsum('hro,oqk->hrqk', rpb.astype(F32), jnp.asarray(pick), precision=HIGHEST)
    by_col = jnp.where(in_window[None, None], by_col, -jnp.inf)
    b = jnp.stack([by_col[:, o:o + NA_WIN_ROWS] for o in range(NA_WIN_ROWS)], axis=0)
    b = b.transpose(0, 1, 3, 2, 4)
    return b.reshape(NA_WIN_ROWS, NA_HEADS * GRID_W, NA_WIN_ROWS * GRID_W)


def _na(na_qkv, bias_cls, n_lat):
    b, s, _ = na_qkv.shape
    n_rows = n_lat // GRID_W
    n_q = s // GRID_W
    wr = NA_WIN_ROWS

    def bias_idx(i, r):
        cls = jnp.where(r < wr // 2, wr - 1 - r,
                        jnp.where(r > n_rows - wr // 2, n_rows - 1 - r, wr // 2 - 1))
        return (jnp.clip(cls, 0, wr - 1), 0, 0)

    return pl.pallas_call(
        functools.partial(_na_body, n_lat=n_lat, n_rows=n_rows),
        grid=(b, n_q),
        in_specs=[pl.BlockSpec((1, GRID_W, NA_WIDTH), lambda i, r: (i, r, 0)),
                  pl.BlockSpec((1, s, NA_WIDTH), lambda i, r: (i, 0, 1)),
                  pl.BlockSpec((1, s, NA_WIDTH), lambda i, r: (i, 0, 2)),
                  pl.BlockSpec((1,) + bias_cls.shape[1:], bias_idx)],
        out_specs=pl.BlockSpec((1, GRID_W, NA_WIDTH), lambda i, r: (i, r, 0)),
        out_shape=jax.ShapeDtypeStruct((b, s, NA_WIDTH), BF16),
        compiler_params=_cparams(("arbitrary", "arbitrary")),
        name="nbr_attention",
    )(na_qkv, na_qkv, na_qkv, bias_cls)


def _rope_tables(n_lat):
    half = ML_HEAD_DIM // 2
    nf = half // 2
    inv = ROPE_THETA ** (-np.arange(nf, dtype=np.float64) / nf)
    pos = np.arange(n_lat)
    ang = np.concatenate([(pos // GRID_W)[:, None] * inv[None, :]] * 2
                         + [(pos % GRID_W)[:, None] * inv[None, :]] * 2, axis=1)
    lane = np.arange(ML_HEAD_DIM)
    second = (lane % half) >= nf
    cos = np.cos(ang)
    sin = np.sin(ang)
    sin_from_lo = np.where(second[None, :], sin, 0.0)
    sin_from_hi = np.where(second[None, :], 0.0, -sin)
    return (jnp.asarray(cos, F32), jnp.asarray(sin_from_lo, F32), jnp.asarray(sin_from_hi, F32))


def _conv_silu(x, w):
    n = x.shape[0]
    t = lax.broadcasted_iota(jnp.int32, x.shape, 0)
    prev = jnp.where(t >= 1, pltpu.roll(x, 1, 0), 0.0)
    nxt = jnp.where(t < n - 1, pltpu.roll(x, n - 1, 0), 0.0)
    return _silu(w[0:1] * prev + w[1:2] * x + w[2:3] * nxt)


def _rope(x, cos, sin_lo, sin_hi):
    nf = ML_HEAD_DIM // 4
    return x * cos + pltpu.roll(x, nf, 1) * sin_lo + pltpu.roll(x, ML_HEAD_DIM - nf, 1) * sin_hi


def _ml_prep_body(xq_ref, xk_ref, v_ref, wq_ref, wk_ref, cos_ref, slo_ref, shi_ref, k_ref, qt_ref, vt_ref, *, n_lat):
    s = xq_ref.shape[1]
    wq = wq_ref[...]
    wk = wk_ref[...]
    cos, slo, shi = cos_ref[...], slo_ref[...], shi_ref[...]
    kscale = ML_HEAD_DIM ** -0.5
    for lo, hi, rotary in ((0, n_lat, True), (n_lat, s, False)):
        q = _conv_silu(xq_ref[0, lo:hi], wq)
        k = _conv_silu(xk_ref[0, lo:hi], wk) * kscale
        if rotary:
            q = _rope(q, cos, slo, shi)
            k = _rope(k, cos, slo, shi)
        k_ref[0, lo:hi] = k.astype(k_ref.dtype)
        v = v_ref[0, lo:hi].astype(F32)
        for j in range((hi - lo) // ML_CHUNK):
            rows = slice(j * ML_CHUNK, (j + 1) * ML_CHUNK)
            qt_ref[0, 0, lo // ML_CHUNK + j] = q[rows].T.astype(qt_ref.dtype)
            vt_ref[0, 0, lo // ML_CHUNK + j] = v[rows].T.astype(vt_ref.dtype)


def _ml_prep(mlqk, mlv, conv_w, rope, n_lat):
    b, s, _ = mlqk.shape
    hd = ML_HEAD_DIM
    nc = s // ML_CHUNK
    tab = pl.BlockSpec((n_lat, hd), lambda i, h: (0, 0))
    chunks = pl.BlockSpec((1, 1, nc, hd, ML_CHUNK), lambda i, h: (i, h, 0, 0, 0))
    chunk_shape = jax.ShapeDtypeStruct((b, ML_HEADS, nc, hd, ML_CHUNK), BF16)
    return pl.pallas_call(
        functools.partial(_ml_prep_body, n_lat=n_lat),
        grid=(b, ML_HEADS),
        in_specs=[pl.BlockSpec((1, s, hd), lambda i, h: (i, 0, h)),
                  pl.BlockSpec((1, s, hd), lambda i, h: (i, 0, ML_HEADS + h)),
                  pl.BlockSpec((1, s, hd), lambda i, h: (i, 0, h)),
                  pl.BlockSpec((3, hd), lambda i, h: (0, h)),
                  pl.BlockSpec((3, hd), lambda i, h: (0, ML_HEADS + h)),
                  tab, tab, tab],
        out_specs=[pl.BlockSpec((1, s, hd), lambda i, h: (i, 0, h)), chunks, chunks],
        out_shape=[jax.ShapeDtypeStruct((b, s, ML_WIDTH), BF16), chunk_shape, chunk_shape],
        compiler_params=_cparams(("arbitrary", "arbitrary")),
        name="mlstm_prep",
    )(mlqk, mlqk, mlv, conv_w, conv_w, *rope)


def _log_sigmoid(x):
    return jnp.minimum(x, 0.0) - jnp.log1p(jnp.exp(-jnp.abs(x)))


def _ml_gates_body(gc_ref, gr_ref, bc_ref, br_ref, oc_ref, or_ref):
    nc = gr_ref.shape[1]
    c = ML_CHUNK
    ii = lax.broadcasted_iota(jnp.int32, (c, c), 0)
    jj = lax.broadcasted_iota(jnp.int32, (c, c), 1)
    lower = (ii >= jj).astype(F32)
    upper = (ii <= jj).astype(F32)
    lane_kind = lax.broadcasted_iota(jnp.int32, (c, LANES), 1) // ML_HEADS
    row_kind = lax.broadcasted_iota(jnp.int32, (4 * ML_HEADS, c), 0) // ML_HEADS
    for j in range(nc):
        g = gc_ref[0, j * c:(j + 1) * c, :] + bc_ref[...]
        lf = _log_sigmoid(g)
        pre = jnp.dot(lower, lf, precision=HIGHEST, preferred_element_type=F32)
        suf = jnp.dot(upper, lf, precision=HIGHEST, preferred_element_type=F32)
        oc_ref[0, j * c:(j + 1) * c, :] = jnp.where(lane_kind == 1, pre, jnp.where(lane_kind == 3, suf, g))
        gr = gr_ref[0, j] + br_ref[...]
        lfr = _log_sigmoid(gr)
        pre_r = jnp.dot(lfr, upper, precision=HIGHEST, preferred_element_type=F32)
        suf_r = jnp.dot(lfr, lower, precision=HIGHEST, preferred_element_type=F32)
        or_ref[0, j] = jnp.where(row_kind == 1, pre_r, jnp.where(row_kind == 3, suf_r, gr))


def _ml_gates(mlg, gate_bias):
    b, s, _ = mlg.shape
    nc = s // ML_CHUNK
    ng = 4 * ML_HEADS
    g_rows = mlg[:, :, :ng].reshape(b, nc, ML_CHUNK, ng).transpose(0, 1, 3, 2)
    bias_c = jnp.zeros((1, LANES), F32).at[0, :ng].set(gate_bias)
    bias_r = gate_bias.reshape(ng, 1)
    return pl.pallas_call(
        _ml_gates_body,
        grid=(b,),
        in_specs=[pl.BlockSpec((1, s, LANES), lambda i: (i, 0, 0)),
                  pl.BlockSpec((1, nc, ng, ML_CHUNK), lambda i: (i, 0, 0, 0)),
                  pl.BlockSpec((1, LANES), lambda i: (0, 0)),
                  pl.BlockSpec((ng, 1), lambda i: (0, 0))],
        out_specs=[pl.BlockSpec((1, s, LANES), lambda i: (i, 0, 0)),
                   pl.BlockSpec((1, nc, ng, ML_CHUNK), lambda i: (i, 0, 0, 0))],
        out_shape=[jax.ShapeDtypeStruct((b, s, LANES), F32),
                   jax.ShapeDtypeStruct((b, nc, ng, ML_CHUNK), F32)],
        compiler_params=_cparams(("arbitrary",)),
        name="mlstm_gates",
    )(mlg, g_rows, bias_c, bias_r)


def _ml_step(k, qt, vt_aug, a_c, b_r, state_t, m, *, forward):
    c = ML_CHUNK
    d = ML_HEAD_DIM
    si = lax.broadcasted_iota(jnp.int32, (c, c), 0)
    ti = lax.broadcasted_iota(jnp.int32, (c, c), 1)
    allowed = (si <= ti) if forward else (si >= ti)
    log_d = jnp.where(allowed, b_r + a_c, -jnp.inf)
    inter = b_r + m
    m_t = jnp.maximum(jnp.max(log_d, axis=0, keepdims=True), inter)
    kq = jnp.dot(k, qt, preferred_element_type=F32)
    s_t = (kq * jnp.exp(log_d - m_t)).astype(BF16)
    w_inter = jnp.exp(inter - m_t)
    tot = (jnp.dot(vt_aug, s_t, preferred_element_type=F32)
           + w_inter * jnp.dot(state_t.astype(BF16), qt, preferred_element_type=F32))
    num = tot[:d]
    den = tot[d:d + 1]
    h_t = num / jnp.maximum(jnp.abs(den), jnp.exp(-m_t))
    b_end = b_r[:, c - 1:c] if forward else b_r[:, 0:1]
    log_w = b_end + a_c
    m_new = jnp.maximum(b_end + m, jnp.max(log_w, axis=0, keepdims=True))
    w_c = jnp.exp(log_w - m_new)
    decay = jnp.exp(b_end + m - m_new)
    kw = (k.astype(F32) * w_c).astype(BF16)
    state_new = decay * state_t + jnp.dot(vt_aug, kw, preferred_element_type=F32)
    return h_t, state_new, m_new


def _ml_scan_body(k_ref, qt_ref, vt_ref, gc_ref, gr_ref, o_ref, nw_ref, out_ref, hsum_ref, *, n_lat):
    c = ML_CHUNK
    d = ML_HEAD_DIM
    s = k_ref.shape[1]
    nc_lat = n_lat // c
    nc_all = s // c
    ones_row = (lax.broadcasted_iota(jnp.int32, (d, c), 0) == 0).astype(BF16)
    nw = nw_ref[...]

    def rows(j):
        return pl.ds(j * c if isinstance(j, int) else pl.multiple_of(j * c, c), c)

    def run(j, state, m, forward):
        vt_aug = jnp.concatenate([vt_ref[0, 0, j], ones_row], axis=0)
        gc = gc_ref[0, 0, rows(j), :]
        gr = gr_ref[0, 0, j]
        k0 = 0 if forward else 2
        a_c = gc[:, k0:k0 + 1] - gc[:, k0 + 1:k0 + 2]
        return _ml_step(k_ref[0, rows(j), :], qt_ref[0, 0, j], vt_aug, a_c, gr[k0 + 1:k0 + 2],
                        state, m, forward=forward)

    def finalize(j, hsum_t):
        hc = hsum_t - jnp.mean(hsum_t, axis=0, keepdims=True)
        hn = (hc * lax.rsqrt(jnp.mean(hc * hc, axis=0, keepdims=True) + LN_EPS)).T
        out_ref[0, rows(j), :] = (_sigmoid(o_ref[0, rows(j), :]) * hn * nw).astype(out_ref.dtype)

    zero_state = jnp.zeros((2 * d, d), F32)
    m0 = jnp.full((1, 1), ML_M_INIT, F32)
    sf, mf = zero_state, m0
    hc_f = []
    for j in range(nc_lat, nc_all):
        h, sf, mf = run(j, sf, mf, True)
        hc_f.append(h)
    sb, mb = zero_state, m0
    hc_b = {}
    for j in range(nc_all - 1, nc_lat - 1, -1):
        h, sb, mb = run(j, sb, mb, False)
        hc_b[j] = h
    for idx, j in enumerate(range(nc_lat, nc_all)):
        finalize(j, hc_f[idx] + hc_b[j])

    half = nc_lat // 2

    def first_half(i, carry):
        sf, mf, sb, mb = carry
        jb = nc_lat - 1 - i
        hf, sf, mf = run(i, sf, mf, True)
        hb, sb, mb = run(jb, sb, mb, False)
        hsum_ref[i] = hf
        hsum_ref[jb] = hb
        return sf, mf, sb, mb

    carry = lax.fori_loop(0, half, first_half, (sf, mf, sb, mb))

    def second_half(i, carry):
        sf, mf, sb, mb = carry
        jb = nc_lat - 1 - i
        hf, sf, mf = run(i, sf, mf, True)
        hb, sb, mb = run(jb, sb, mb, False)
        finalize(i, hf + hsum_ref[i])
        finalize(jb, hb + hsum_ref[jb])
        return sf, mf, sb, mb

    lax.fori_loop(half, nc_lat, second_half, carry)


def _ml_scan(k, qt, vt, gcol, grow, mlo, norm_w, n_lat):
    b, s, _ = k.shape
    hd = ML_HEAD_DIM
    nc = s // ML_CHUNK
    ng = 4 * ML_HEADS
    gc_h = gcol[:, :, :ng].reshape(b, s, 4, ML_HEADS).transpose(0, 3, 1, 2)
    gr_h = grow.reshape(b, nc, 4, ML_HEADS, ML_CHUNK).transpose(0, 3, 1, 2, 4)
    tok = pl.BlockSpec((1, s, hd), lambda i, h: (i, 0, h))
    chunks = pl.BlockSpec((1, 1, nc, hd, ML_CHUNK), lambda i, h: (i, h, 0, 0, 0))
    return pl.pallas_call(
        functools.partial(_ml_scan_body, n_lat=n_lat),
        grid=(b, ML_HEADS),
        in_specs=[tok, chunks, chunks,
                  pl.BlockSpec((1, 1, s, 4), lambda i, h: (i, h, 0, 0)),
                  pl.BlockSpec((1, 1, nc, 4, ML_CHUNK), lambda i, h: (i, h, 0, 0, 0)),
                  tok,
                  pl.BlockSpec((1, hd), lambda i, h: (0, h))],
        out_specs=tok,
        out_shape=jax.ShapeDtypeStruct((b, s, ML_WIDTH), BF16),
        scratch_shapes=[pltpu.VMEM((n_lat // ML_CHUNK, hd, ML_CHUNK), F32)],
        compiler_params=_cparams(("arbitrary", "arbitrary")),
        name="mlstm_scan",
    )(k, qt, vt, gc_h, gr_h, mlo, norm_w)


def _top2(logits, n_valid):
    lane = lax.broadcasted_iota(jnp.int32, logits.shape, 1).astype(F32)
    neg = jnp.float32(-jnp.inf)
    l = jnp.where(lane < n_valid, logits, neg)
    m1 = jnp.max(l, axis=-1, keepdims=True)
    e1 = jnp.min(jnp.where(l == m1, lane, float(LANES)), axis=-1, keepdims=True)
    l2 = jnp.where(lane == e1, neg, l)
    m2 = jnp.max(l2, axis=-1, keepdims=True)
    e2 = jnp.min(jnp.where(l2 == m2, lane, float(LANES)), axis=-1, keepdims=True)
    x2 = jnp.exp(m2 - m1)
    p1 = 1.0 / (1.0 + x2)
    p2 = x2 / (1.0 + x2)
    return jnp.where(lane == 0, e1, jnp.where(lane == 1, e2, jnp.where(lane == 2, p1, jnp.where(lane == 3, p2, 0.0))))


def _route_ranks(route, valid, cnt_ref):
    tm = route.shape[0]
    lane = lax.broadcasted_iota(jnp.int32, route.shape, 1).astype(F32)
    pick1 = jnp.where(valid, (lane == route[:, 0:1]).astype(F32), 0.0)
    pick2 = jnp.where(valid, (lane == route[:, 1:2]).astype(F32), 0.0)
    both = pick1 + pick2
    before = (lax.broadcasted_iota(jnp.int32, (tm, tm), 0) > lax.broadcasted_iota(jnp.int32, (tm, tm), 1))
    prefix = jnp.dot(before.astype(BF16), both.astype(BF16), preferred_element_type=F32) + cnt_ref[...]
    rank1 = jnp.sum(prefix * pick1, axis=-1, keepdims=True)
    rank2 = jnp.sum(prefix * pick2, axis=-1, keepdims=True)
    cnt_ref[...] += jnp.sum(both, axis=0, keepdims=True)
    return jnp.where(lane == 4, rank1, jnp.where(lane == 5, rank2, route))


def _merge_body(*refs, n_lat, alpha, moe, n_rows_tok):
    (a_ref, n_ref, m_ref, g_ref, x_ref, wbp_ref, wbn_ref, wbm_ref, wo_ref,
     g1l_ref, g1c_ref, shl_ref, shc_ref, scl_ref, scc_ref, lng_ref, lnb_ref) = refs[:17]
    if moe:
        rw_ref, rb_ref, x_out, tok_out, route_out, cnt_out, cnt_ref = refs[17:]
    else:
        x_out, tok_out = refs[17:]
    tm = x_ref.shape[1]
    d = x_ref.shape[2]
    is_ctx = _is_ctx_rows(pl.program_id(1), tm, n_lat)
    g = g_ref[0]
    y = (_sigmoid(g[:, :d]) * jnp.dot(a_ref[0], wbp_ref[...], preferred_element_type=F32)
         + _sigmoid(g[:, d:2 * d]) * jnp.dot(n_ref[0], wbn_ref[...], preferred_element_type=F32)
         + _sigmoid(g[:, 2 * d:]) * jnp.dot(m_ref[0], wbm_ref[...], preferred_element_type=F32))
    y = jnp.dot(y.astype(BF16), wo_ref[...], preferred_element_type=F32)
    z = alpha * x_ref[0] + _tile_mod(g1l_ref, g1c_ref, is_ctx) * y
    xn = _ln_plain(z) * lng_ref[...] + lnb_ref[...]
    x_out[0] = xn
    tok = _ln_plain(xn) * (1.0 + _tile_mod(scl_ref, scc_ref, is_ctx)) + _tile_mod(shl_ref, shc_ref, is_ctx)
    tok_out[0] = tok.astype(tok_out.dtype)
    if moe:
        @pl.when((pl.program_id(0) == 0) & (pl.program_id(1) == 0))
        def _():
            cnt_ref[...] = jnp.zeros_like(cnt_ref)

        lane = lax.broadcasted_iota(jnp.int32, (tm, LANES), 1)
        logits = jnp.zeros((tm, LANES), F32)
        for e in range(N_EXPERTS):
            col = jnp.sum(tok * rw_ref[e:e + 1, :], axis=-1, keepdims=True)
            logits = jnp.where(lane == e, col, logits)
        valid = (pl.program_id(1) * tm + lax.broadcasted_iota(jnp.int32, (tm, 1), 0)) < n_rows_tok
        route_out[0] = _route_ranks(_top2(logits + rb_ref[...], N_EXPERTS), valid, cnt_ref)
        cnt_out[...] = cnt_ref[...]


def _merge(a, n, m, gates, x_all, w_bp, w_bn, w_bm, w_o, g1, sh2, sc2, ln_g, ln_b, n_lat, alpha,
           router=None, n_rows_tok=0):
    b, s, d = x_all.shape
    tm = TM_MERGE
    moe = router is not None

    def tok(w):
        return pl.BlockSpec((1, tm, w), lambda i, j: (i, j, 0))

    def full(arr):
        return pl.BlockSpec(arr.shape, lambda i, j: (0,) * arr.ndim)

    in_specs = ([tok(a.shape[2]), tok(n.shape[2]), tok(m.shape[2]), tok(gates.shape[2]), tok(d),
                 full(w_bp), full(w_bn), full(w_bm), full(w_o)]
                + _row_specs(d, b) * 3 + [full(ln_g), full(ln_b)])
    args = [a, n, m, gates, x_all, w_bp, w_bn, w_bm, w_o, g1, g1, sh2, sh2, sc2, sc2, ln_g, ln_b]
    out_specs = [tok(d), tok(d)]
    out_shape = [jax.ShapeDtypeStruct((b, s, d), F32), jax.ShapeDtypeStruct((b, s, d), F32 if moe else BF16)]
    scratch = []
    if moe:
        in_specs += [full(router[0]), full(router[1])]
        args += list(router)
        out_specs += [tok(LANES), pl.BlockSpec((1, LANES), lambda i, j: (0, 0))]
        out_shape += [jax.ShapeDtypeStruct((b, s, LANES), F32), jax.ShapeDtypeStruct((1, LANES), F32)]
        scratch = [pltpu.VMEM((1, LANES), F32)]
    return pl.pallas_call(
        functools.partial(_merge_body, n_lat=n_lat, alpha=alpha, moe=moe, n_rows_tok=n_rows_tok),
        grid=(b, s // tm),
        in_specs=in_specs,
        out_specs=out_specs,
        out_shape=out_shape,
        scratch_shapes=scratch,
        compiler_params=_cparams(("arbitrary", "arbitrary")),
        name="merge",
    )(*args)


def _finish_rows(f, x, g2, lng, lnb, alpha):
    return _ln_plain(alpha * x + g2 * f) * lng + lnb


def _ffn_body(*refs, n_lat, alpha, with_next):
    (t_ref, x_ref, w1_ref, w3_ref, w2_ref, g2l_ref, g2c_ref, lng_ref, lnb_ref) = refs[:9]
    if with_next:
        shl_ref, shc_ref, scl_ref, scc_ref, x_out, u_out, acc_ref = refs[9:]
    else:
        x_out, acc_ref = refs[9:]
    f = pl.program_id(2)
    t = t_ref[0]
    h1 = jnp.dot(t, w1_ref[...], preferred_element_type=F32)
    h3 = jnp.dot(t, w3_ref[...], preferred_element_type=F32)
    contrib = jnp.dot((_silu(h1) * h3).astype(BF16), w2_ref[...], preferred_element_type=F32)

    @pl.when(f == 0)
    def _():
        acc_ref[...] = contrib

    @pl.when(f > 0)
    def _():
        acc_ref[...] += contrib

    @pl.when(f == pl.num_programs(2) - 1)
    def _():
        tm = x_ref.shape[1]
        is_ctx = _is_ctx_rows(pl.program_id(1), tm, n_lat)
        xn = _finish_rows(acc_ref[...], x_ref[0], _tile_mod(g2l_ref, g2c_ref, is_ctx), lng_ref[...], lnb_ref[...], alpha)
        x_out[0] = xn
        if with_next:
            u = _ln_plain(xn) * (1.0 + _tile_mod(scl_ref, scc_ref, is_ctx)) + _tile_mod(shl_ref, shc_ref, is_ctx)
            u_out[0] = u.astype(u_out.dtype)


def _ffn(tok, x_all, w1, w3, w2, g2, ln_g, ln_b, n_lat, alpha, nxt=None):
    b, s, d = x_all.shape
    dff = w1.shape[1]
    tm, tf = TM_TOKEN, TF_FFN
    with_next = nxt is not None

    def tokspec():
        return pl.BlockSpec((1, tm, d), lambda i, j, f: (i, j, 0))

    def rows():
        return [pl.BlockSpec((1, 1, d), lambda i, j, f: (i, 0, 0)),
                pl.BlockSpec((1, 1, d), lambda i, j, f: (b, 0, 0))]

    vec = pl.BlockSpec((1, d), lambda i, j, f: (0, 0))
    in_specs = [tokspec(), tokspec(),
                pl.BlockSpec((d, tf), lambda i, j, f: (0, f)),
                pl.BlockSpec((d, tf), lambda i, j, f: (0, f)),
                pl.BlockSpec((tf, d), lambda i, j, f: (f, 0))] + rows() + [vec, vec]
    args = [tok, x_all, w1, w3, w2, g2, g2, ln_g, ln_b]
    out_specs = [tokspec()]
    out_shape = [jax.ShapeDtypeStruct((b, s, d), F32)]
    if with_next:
        in_specs += rows() + rows()
        args += [nxt[0], nxt[0], nxt[1], nxt[1]]
        out_specs.append(tokspec())
        out_shape.append(jax.ShapeDtypeStruct((b, s, d), BF16))
    res = pl.pallas_call(
        functools.partial(_ffn_body, n_lat=n_lat, alpha=alpha, with_next=with_next),
        grid=(b, s // tm, dff // tf),
        in_specs=in_specs,
        out_specs=out_specs,
        out_shape=out_shape,
        scratch_shapes=[pltpu.VMEM((tm, d), F32)],
        compiler_params=_cparams(("arbitrary", "arbitrary", "arbitrary")),
        name="ffn_dense",
    )(*args)
    return res if with_next else (res[0], None)


def _moe_tables(route, counts, n_rows_tok):
    r = MOE_ROWS
    tm = TM_MOE
    b = route.shape[0]
    cnt = counts[0, :N_EXPERTS].astype(jnp.int32)
    nblk = (cnt + r - 1) // r
    bend = jnp.cumsum(nblk)
    bstart = bend - nblk
    total = bend[-1]
    n_blocks = -(-(b * n_rows_tok * TOP_K) // r) + N_EXPERTS
    part = route[:, :n_rows_tok]
    experts = jnp.arange(N_EXPERTS, dtype=F32)
    seg_row0 = jnp.sum((part[:, :, 0:TOP_K, None] == experts).astype(jnp.int32) * (bstart * r), axis=-1)
    dest = seg_row0 + part[:, :, 4:4 + TOP_K].astype(jnp.int32)
    nt = n_rows_tok // tm
    dest_tiles = dest.reshape(b, nt, tm, TOP_K).transpose(0, 1, 3, 2).reshape(b * nt, 1, TOP_K * tm)
    blk = jnp.arange(n_blocks, dtype=jnp.int32)
    e_of = jnp.sum((blk[:, None] >= bend[None, :]).astype(jnp.int32), axis=1)
    e_last = jnp.sum((total - 1 >= bend).astype(jnp.int32))
    active = (blk < total).astype(jnp.int32)
    block_e = jnp.minimum(e_of, e_last)
    spare = total + jnp.arange(N_EXPERTS, dtype=jnp.int32)
    zero_blk = jnp.concatenate([bend - 1, spare])
    zero_on = jnp.concatenate([nblk > 0, spare < n_blocks]).astype(jnp.int32)
    zero_blk = jnp.clip(zero_blk, 0, n_blocks - 1)
    return dest_tiles, zero_blk, zero_on, block_e, active


def _dispatch_body(zblk_ref, zon_ref, dest_ref, tok_ref, xs_hbm, zbuf, zsem, dsem):
    i = pl.program_id(0)
    tm = tok_ref.shape[1]
    r = zbuf.shape[0]

    def zero_copy(z):
        start = pl.multiple_of(zblk_ref[z] * r, r)
        return pltpu.make_async_copy(zbuf, xs_hbm.at[pl.ds(start, r)], zsem)

    @pl.when(i == 0)
    def _():
        zbuf[...] = jnp.zeros_like(zbuf)
        for z in range(2 * N_EXPERTS):
            @pl.when(zon_ref[z] > 0)
            def _():
                zero_copy(z).start()
        for z in range(2 * N_EXPERTS):
            @pl.when(zon_ref[z] > 0)
            def _():
                zero_copy(z).wait()

    def row_copy(j, k):
        return pltpu.make_async_copy(tok_ref.at[0, pl.ds(j, 1)],
                                     xs_hbm.at[pl.ds(dest_ref[0, 0, k * tm + j], 1)], dsem)

    def issue(j, c):
        for k in range(TOP_K):
            row_copy(j, k).start()
        return c

    lax.fori_loop(0, tm, issue, 0, unroll=8)
    for _ in range(TOP_K):
        pltpu.make_async_copy(tok_ref.at[0], xs_hbm.at[pl.ds(0, tm)], dsem).wait()


def _moe_dispatch(tok, zero_blk, zero_on, dest_tiles, n_blocks, n_rows_tok):
    d = tok.shape[2]
    tm = TM_MOE
    nt = n_rows_tok // tm
    grid_spec = pltpu.PrefetchScalarGridSpec(
        num_scalar_prefetch=2,
        grid=(dest_tiles.shape[0],),
        in_specs=[pl.BlockSpec((1, 1, TOP_K * tm), lambda i, zb, zo: (i, 0, 0), memory_space=pltpu.SMEM),
                  pl.BlockSpec((1, tm, d), lambda i, zb, zo: (i // nt, i % nt, 0))],
        out_specs=pl.BlockSpec(memory_space=pl.ANY),
        scratch_shapes=[pltpu.VMEM((MOE_ROWS, d), F32),
                        pltpu.SemaphoreType.DMA(()), pltpu.SemaphoreType.DMA(())],
    )
    return pl.pallas_call(
        _dispatch_body,
        grid_spec=grid_spec,
        out_shape=jax.ShapeDtypeStruct((n_blocks * MOE_ROWS, d), F32),
        compiler_params=_cparams(("arbitrary",)),
        name="moe_dispatch",
    )(zero_blk, zero_on, dest_tiles, tok)


def _experts_body(be_ref, act_ref, x_ref, w1_ref, w3_ref, w2_ref, y_ref, xb_ref):
    i = pl.program_id(0)
    f = pl.program_id(1)

    @pl.when((act_ref[i] == 0) & (f == 0))
    def _():
        y_ref[...] = jnp.zeros_like(y_ref)

    @pl.when(act_ref[i] > 0)
    def _():
        @pl.when(f == 0)
        def _():
            xb_ref[...] = x_ref[...].astype(BF16)

        x = xb_ref[...]
        h1 = jnp.dot(x, w1_ref[0, 0].astype(BF16), preferred_element_type=F32)
        h3 = jnp.dot(x, w3_ref[0, 0].astype(BF16), preferred_element_type=F32)
        contrib = jnp.dot((_silu(h1) * h3).astype(BF16), w2_ref[0, 0].astype(BF16), preferred_element_type=F32)

        @pl.when(f == 0)
        def _():
            y_ref[...] = contrib

        @pl.when(f > 0)
        def _():
            y_ref[...] += contrib


def _moe_experts(xs, block_e, active, w1, w3, w2, li):
    d = xs.shape[1]
    dff = w1.shape[3]
    r, tf = MOE_ROWS, TF_FFN
    nf = dff // tf
    n_blocks = block_e.shape[0]

    def f_eff(i, f, act):
        return jnp.where(act[i] > 0, f, nf - 1)

    grid_spec = pltpu.PrefetchScalarGridSpec(
        num_scalar_prefetch=2,
        grid=(n_blocks, nf),
        in_specs=[pl.BlockSpec((r, d), lambda i, f, be, act: (i, 0)),
                  pl.BlockSpec((1, 1, d, tf), lambda i, f, be, act: (li, be[i], 0, f_eff(i, f, act))),
                  pl.BlockSpec((1, 1, d, tf), lambda i, f, be, act: (li, be[i], 0, f_eff(i, f, act))),
                  pl.BlockSpec((1, 1, tf, d), lambda i, f, be, act: (li, be[i], f_eff(i, f, act), 0))],
        out_specs=pl.BlockSpec((r, d), lambda i, f, be, act: (i, 0)),
        scratch_shapes=[pltpu.VMEM((r, d), BF16)],
    )
    return pl.pallas_call(
        _experts_body,
        grid_spec=grid_spec,
        out_shape=jax.ShapeDtypeStruct(xs.shape, F32),
        compiler_params=_cparams(("arbitrary", "arbitrary")),
        name="moe_experts",
    )(block_e, active, xs, w1, w3, w2)


def _combine_body(*refs, n_lat, alpha, with_next):
    (dest_ref, ys_hbm, p_ref, x_ref, g2l_ref, g2c_ref, lng_ref, lnb_ref) = refs[:8]
    if with_next:
        shl_ref, shc_ref, scl_ref, scc_ref, x_out, u_out, ybuf, sem = refs[8:]
    else:
        x_out, ybuf, sem = refs[8:]
    tm = x_ref.shape[1]

    def row_copy(j, k):
        return pltpu.make_async_copy(ys_hbm.at[pl.ds(dest_ref[0, 0, k * tm + j], 1)],
                                     ybuf.at[k, pl.ds(j, 1)], sem)

    def issue(j, c):
        for k in range(TOP_K):
            row_copy(j, k).start()
        return c

    lax.fori_loop(0, tm, issue, 0, unroll=8)
    for k in range(TOP_K):
        pltpu.make_async_copy(ys_hbm.at[pl.ds(0, tm)], ybuf.at[k], sem).wait()

    is_ctx = _is_ctx_rows(pl.program_id(1), tm, n_lat)
    p = p_ref[0]
    f = p[:, 2:3] * ybuf[0] + p[:, 3:4] * ybuf[1]
    xn = _finish_rows(f, x_ref[0], _tile_mod(g2l_ref, g2c_ref, is_ctx), lng_ref[...], lnb_ref[...], alpha)
    x_out[0] = xn
    if with_next:
        u = _ln_plain(xn) * (1.0 + _tile_mod(scl_ref, scc_ref, is_ctx)) + _tile_mod(shl_ref, shc_ref, is_ctx)
        u_out[0] = u.astype(u_out.dtype)


def _combine(ys, dest_tiles, route, x_all, g2, ln_g, ln_b, n_lat, alpha, n_tok_rows, nxt=None):
    b, s, d = x_all.shape
    tm = TM_MOE
    nt = n_tok_rows // tm
    with_next = nxt is not None

    def tok(w):
        return pl.BlockSpec((1, tm, w), lambda i, j: (i, j, 0))

    vec = pl.BlockSpec((1, d), lambda i, j: (0, 0))
    in_specs = [pl.BlockSpec((1, 1, TOP_K * tm), lambda i, j: (i * nt + j, 0, 0), memory_space=pltpu.SMEM),
                pl.BlockSpec(memory_space=pl.ANY), tok(LANES), tok(d)] + _row_specs(d, b) + [vec, vec]
    args = [dest_tiles, ys, route, x_all, g2, g2, ln_g, ln_b]
    out_specs = [tok(d)]
    out_shape = [jax.ShapeDtypeStruct((b, n_tok_rows, d), F32)]
    if with_next:
        in_specs += _row_specs(d, b) * 2
        args += [nxt[0], nxt[0], nxt[1], nxt[1]]
        out_specs.append(tok(d))
        out_shape.append(jax.ShapeDtypeStruct((b, n_tok_rows, d), BF16))
    res = pl.pallas_call(
        functools.partial(_combine_body, n_lat=n_lat, alpha=alpha, with_next=with_next),
        grid=(b, nt),
        in_specs=in_specs,
        out_specs=out_specs,
        out_shape=out_shape,
        scratch_shapes=[pltpu.VMEM((TOP_K, tm, d), F32), pltpu.SemaphoreType.DMA(())],
        compiler_params=_cparams(("arbitrary", "arbitrary")),
        name="moe_combine",
    )(*args)
    return res if with_next else (res[0], None)


def _block_diag(pool_w):
    g, c, _ = pool_w.shape
    out = jnp.zeros((g * c, g * c), pool_w.dtype)
    for i in range(g):
        out = out.at[i * c:(i + 1) * c, i * c:(i + 1) * c].set(pool_w[i])
    return out


def kernel(x, c, ctx, c_ctx, w_mod, b_mod, w_in, ml_gate_bias, ml_conv_w, ml_norm_w, pool_w, pool_scale,
           na_rpb, w_branch_pool, w_branch_na, w_branch_ml, w_out, ln1_g, ln1_b, ln2_g, ln2_b,
           ffn_w1, ffn_w3, ffn_w2, moe_router_w, moe_router_b, moe_w1, moe_w3, moe_w2):
    b, n_lat, d = x.shape
    n_ctx = ctx.shape[1]
    s = n_lat + n_ctx
    depth = w_in.shape[0]
    alpha = (2 * depth) ** 0.25
    off_g = POOL_WIDTH + 3 * NA_WIDTH + 4 * ML_WIDTH
    n_gate_cols = 4 * ML_HEADS

    x_all = jnp.concatenate([x, ctx], axis=1)
    mod = _mod_vectors(c, c_ctx, w_mod, b_mod)

    def mod_part(layer, k):
        return mod[layer, :, k * d:(k + 1) * d].reshape(MOD_ROWS, 1, d)

    rope = _rope_tables(n_lat)
    u = _ln_mod(x_all, mod_part(0, 0), mod_part(0, 1), n_lat)

    for layer in range(depth):
        last = layer == depth - 1
        w_l = w_in[layer]
        w_main = w_l[:, :off_g].astype(BF16)
        w_g = jnp.zeros((d, LANES), BF16).at[:, :n_gate_cols].set(w_l[:, off_g:off_g + n_gate_cols].astype(BF16))
        w_gate = w_l[:, off_g + n_gate_cols:].astype(BF16)
        pool_in, na_qkv, mlqk, mlv, mlo, mlg = _proj_main(u, w_main, w_g)
        gates = _proj_gate(u, w_gate)

        a = _pool(pool_in, _block_diag(pool_w[layer]).astype(BF16), pool_scale[layer].reshape(1, -1), n_lat)
        n = _na(na_qkv, _na_bias_classes(na_rpb[layer]), n_lat)
        k_ml, qt_ml, vt_ml = _ml_prep(mlqk, mlv, ml_conv_w[layer], rope, n_lat)
        gcol, grow = _ml_gates(mlg, ml_gate_bias[layer])
        m = _ml_scan(k_ml, qt_ml, vt_ml, gcol, grow, mlo, ml_norm_w[layer].reshape(1, -1), n_lat)

        is_moe = layer % 2 == 1
        i = layer // 2
        router = None
        if is_moe:
            rw = moe_router_w[i].T
            rb = jnp.zeros((1, LANES), F32).at[0, :N_EXPERTS].set(moe_router_b[i])
            router = (rw, rb)
        n_rows_tok = n_lat if last else s
        merged = _merge(a, n, m, gates, x_all,
                        w_branch_pool[layer].astype(BF16), w_branch_na[layer].astype(BF16),
                        w_branch_ml[layer].astype(BF16), w_out[layer].astype(BF16),
                        mod_part(layer, 2), mod_part(layer, 3), mod_part(layer, 4),
                        ln1_g[layer].reshape(1, d), ln1_b[layer].reshape(1, d), n_lat, alpha, router, n_rows_tok)
        nxt = None if last else (mod_part(layer + 1, 0), mod_part(layer + 1, 1))
        g2 = mod_part(layer, 5)
        lng, lnb = ln2_g[layer].reshape(1, d), ln2_b[layer].reshape(1, d)
        if not is_moe:
            x_mid, tok = merged
            x_all, u = _ffn(tok, x_mid, ffn_w1[i].astype(BF16), ffn_w3[i].astype(BF16), ffn_w2[i].astype(BF16),
                            g2, lng, lnb, n_lat, alpha, nxt)
        else:
            x_mid, tok, route, counts = merged
            x_all, u = _moe_layer(tok, route, counts, x_mid, moe_w1, moe_w3, moe_w2, i, g2, lng, lnb,
                                  n_lat, alpha, n_rows_tok, nxt)
    return x_all[:, :n_lat]


def _moe_layer(tok, route, counts, x_mid, w1, w3, w2, li, g2, lng, lnb, n_lat, alpha, n_rows_tok, nxt):
    dest_tiles, zero_blk, zero_on, block_e, active = _moe_tables(route, counts, n_rows_tok)
    xs = _moe_dispatch(tok, zero_blk, zero_on, dest_tiles, block_e.shape[0], n_rows_tok)
    ys = _moe_experts(xs, block_e, active, w1, w3, w2, li)
    return _combine(ys, dest_tiles, route, x_mid, g2, lng, lnb, n_lat, alpha, n_rows_tok, nxt)
```

```python
import functools

import numpy as np
import jax
import jax.numpy as jnp
from jax import lax
from jax.experimental import pallas as pl
from jax.experimental.pallas import tpu as pltpu

F32 = jnp.float32
BF16 = jnp.bfloat16
HIGHEST = lax.Precision.HIGHEST

GRID_W = 64
POOL_GROUPS = 4
POOL_GROUP_DIM = 64
POOL_WIDTH = POOL_GROUPS * POOL_GROUP_DIM
POOL_HALF_WINDOWS = (1, 2, 4, 8)
NA_HEADS = 4
NA_HEAD_DIM = 64
NA_WIDTH = NA_HEADS * NA_HEAD_DIM
NA_WIN_ROWS = 8
NA_WIN_COLS = 16
ML_HEADS = 4
ML_HEAD_DIM = 128
ML_WIDTH = ML_HEADS * ML_HEAD_DIM
ML_CHUNK = 128
ML_M_INIT = -1e30
ROPE_THETA = 10000.0
N_EXPERTS = 8
TOP_K = 2
N_BRANCHES = 3
LN_EPS = 1e-6

LANES = 128
MOD_ROWS = 16
VMEM_LIMIT = 56 * 1024 * 1024

TM_TOKEN = 768
TM_MERGE = 384
TF_FFN = 512
MOE_ROWS = 1024
TM_MOE = 256


def _cparams(sem):
    return pltpu.CompilerParams(dimension_semantics=sem, vmem_limit_bytes=VMEM_LIMIT)


def _sigmoid(x):
    return 0.5 + 0.5 * jnp.tanh(0.5 * x)


def _silu(x):
    t = 0.5 * x
    return t + t * jnp.tanh(t)


def _ln_plain(x):
    xc = x - jnp.mean(x, axis=-1, keepdims=True)
    return xc * lax.rsqrt(jnp.mean(xc * xc, axis=-1, keepdims=True) + LN_EPS)


def _is_ctx_rows(tile_idx, tm, n_lat):
    row = tile_idx * tm + lax.broadcasted_iota(jnp.int32, (tm, 1), 0)
    return row >= n_lat


def _tile_mod(lat_ref, ctx_ref, is_ctx):
    return jnp.where(is_ctx, ctx_ref[0], lat_ref[0])


def _mod_body(s_ref, w_ref, b_ref, o_ref):
    s = _silu(s_ref[...])
    o_ref[0] = jnp.dot(s, w_ref[0], precision=HIGHEST, preferred_element_type=F32) + b_ref[0]


def _mod_vectors(c, c_ctx, w_mod, b_mod):
    depth, d, d6 = w_mod.shape
    b = c.shape[0]
    s = jnp.zeros((MOD_ROWS, d), F32).at[:b].set(c).at[b].set(c_ctx)
    tn = 1536
    return pl.pallas_call(
        _mod_body,
        grid=(depth, d6 // tn),
        in_specs=[pl.BlockSpec((MOD_ROWS, d), lambda l, j: (0, 0)),
                  pl.BlockSpec((1, d, tn), lambda l, j: (l, 0, j)),
                  pl.BlockSpec((1, 1, tn), lambda l, j: (l, 0, j))],
        out_specs=pl.BlockSpec((1, MOD_ROWS, tn), lambda l, j: (l, 0, j)),
        out_shape=jax.ShapeDtypeStruct((depth, MOD_ROWS, d6), F32),
        compiler_params=_cparams(("arbitrary", "arbitrary")),
        name="mod_vectors",
    )(s, w_mod, b_mod.reshape(depth, 1, d6))


def _ln_mod_body(x_ref, shl_ref, shc_ref, scl_ref, scc_ref, u_ref, *, n_lat):
    tm = x_ref.shape[1]
    is_ctx = _is_ctx_rows(pl.program_id(1), tm, n_lat)
    sh = _tile_mod(shl_ref, shc_ref, is_ctx)
    sc = _tile_mod(scl_ref, scc_ref, is_ctx)
    u_ref[0] = (_ln_plain(x_ref[0]) * (1.0 + sc) + sh).astype(u_ref.dtype)


def _row_specs(d, nb):
    return [pl.BlockSpec((1, 1, d), lambda b, j: (b, 0, 0)),
            pl.BlockSpec((1, 1, d), lambda b, j: (nb, 0, 0))]


def _ln_mod(x_all, shift, scale, n_lat):
    b, s, d = x_all.shape
    tm = TM_TOKEN
    return pl.pallas_call(
        functools.partial(_ln_mod_body, n_lat=n_lat),
        grid=(b, s // tm),
        in_specs=[pl.BlockSpec((1, tm, d), lambda i, j: (i, j, 0))] + _row_specs(d, b) + _row_specs(d, b),
        out_specs=pl.BlockSpec((1, tm, d), lambda i, j: (i, j, 0)),
        out_shape=jax.ShapeDtypeStruct((b, s, d), BF16),
        compiler_params=_cparams(("arbitrary", "arbitrary")),
        name="ln_mod",
    )(x_all, shift, shift, scale, scale)


def _proj_main_body(u_ref, w_ref, wg_ref, pool_ref, na_ref, mlqk_ref, mlv_ref, mlo_ref, mlg_ref):
    u = u_ref[0]

    def seg(lo, hi):
        return jnp.dot(u, w_ref[:, lo:hi], preferred_element_type=F32)

    off_na = POOL_WIDTH
    off_ml = off_na + 3 * NA_WIDTH
    off_mlv = off_ml + 2 * ML_WIDTH
    off_mlo = off_ml + 3 * ML_WIDTH
    pool_ref[0] = seg(0, off_na)
    na_ref[0, :, :NA_WIDTH] = (seg(off_na, off_na + NA_WIDTH) * NA_HEAD_DIM ** -0.5).astype(na_ref.dtype)
    na_ref[0, :, NA_WIDTH:] = seg(off_na + NA_WIDTH, off_ml).astype(na_ref.dtype)
    mlqk_ref[0] = seg(off_ml, off_mlv)
    mlv_ref[0] = seg(off_mlv, off_mlo).astype(mlv_ref.dtype)
    mlo_ref[0] = seg(off_mlo, off_mlo + ML_WIDTH).astype(mlo_ref.dtype)
    mlg_ref[0] = jnp.dot(u, wg_ref[...], preferred_element_type=F32)


def _proj_main(u, w_main, w_g):
    b, s, d = u.shape
    tm = TM_TOKEN
    widths = (POOL_WIDTH, 3 * NA_WIDTH, 2 * ML_WIDTH, ML_WIDTH, ML_WIDTH, LANES)
    dtypes = (F32, BF16, F32, BF16, BF16, F32)
    return pl.pallas_call(
        _proj_main_body,
        grid=(b, s // tm),
        in_specs=[pl.BlockSpec((1, tm, d), lambda i, j: (i, j, 0)),
                  pl.BlockSpec(w_main.shape, lambda i, j: (0, 0)),
                  pl.BlockSpec(w_g.shape, lambda i, j: (0, 0))],
        out_specs=[pl.BlockSpec((1, tm, w), lambda i, j: (i, j, 0)) for w in widths],
        out_shape=[jax.ShapeDtypeStruct((b, s, w), dt) for w, dt in zip(widths, dtypes)],
        compiler_params=_cparams(("arbitrary", "arbitrary")),
        name="proj_main",
    )(u, w_main, w_g)


def _pool_segment(a, w_bd, scale):
    n = a.shape[0]
    t = lax.broadcasted_iota(jnp.int32, a.shape, 0)
    grp = lax.broadcasted_iota(jnp.int32, a.shape, 1) // POOL_GROUP_DIM

    def up(x, k):
        return jnp.where(t < n - k, pltpu.roll(x, n - k, 0), 0.0)

    def down(x, k):
        return jnp.where(t >= k, pltpu.roll(x, k, 0), 0.0)

    ahead = a
    behind = down(a, 1)
    wsum = jnp.zeros_like(a)
    half = jnp.zeros_like(t)
    for g, h in enumerate(POOL_HALF_WINDOWS):
        if g > 0:
            ahead = ahead + up(ahead, h // 2)
            behind = behind + down(behind, h // 2)
        wsum = jnp.where(grp == g, ahead + behind, wsum)
        half = jnp.where(grp == g, h, half)
    cnt = jnp.minimum(t + half, n) - jnp.maximum(t - half, 0)
    mean = wsum / cnt.astype(F32)
    out = jnp.dot((mean - a).astype(BF16), w_bd, preferred_element_type=F32)
    return out * scale


def _pool_body(a_ref, w_ref, sc_ref, o_ref, *, n_lat):
    w_bd = w_ref[...]
    scale = sc_ref[...]
    s = a_ref.shape[1]
    o_ref[0, :n_lat] = _pool_segment(a_ref[0, :n_lat], w_bd, scale).astype(o_ref.dtype)
    o_ref[0, n_lat:] = _pool_segment(a_ref[0, n_lat:s], w_bd, scale).astype(o_ref.dtype)


def _pool(pool_in, w_bd, scale, n_lat):
    b, s, w = pool_in.shape
    return pl.pallas_call(
        functools.partial(_pool_body, n_lat=n_lat),
        grid=(b,),
        in_specs=[pl.BlockSpec((1, s, w), lambda i: (i, 0, 0)),
                  pl.BlockSpec((w, w), lambda i: (0, 0)),
                  pl.BlockSpec((1, w), lambda i: (0, 0))],
        out_specs=pl.BlockSpec((1, s, w), lambda i: (i, 0, 0)),
        out_shape=jax.ShapeDtypeStruct((b, s, w), BF16),
        compiler_params=_cparams(("arbitrary",)),
        name="pool",
    )(pool_in, w_bd, scale)


NA_ROWS_PER_STEP = 4


def _na_body(q_ref, k_ref, v_ref, bias_ref, o_ref, *, n_lat, n_rows):
    step = pl.program_id(1)
    s_tot = k_ref.shape[1]
    kc = k_ref[0, n_lat:s_tot, :]
    vc = v_ref[0, n_lat:s_tot, :]
    nt = (((1,), (1,)), ((), ()))
    head = lax.broadcasted_iota(jnp.int32, (GRID_W, NA_WIDTH), 1) // NA_HEAD_DIM
    wr = NA_WIN_ROWS

    def one_row(g, local):
        rows = slice(g * GRID_W, (g + 1) * GRID_W)
        q = q_ref[0, rows, :]
        zero = jnp.zeros_like(q)
        q4 = jnp.concatenate([jnp.where(head == h, q, zero) for h in range(NA_HEADS)], axis=0)
        s_ctx = lax.dot_general(q4, kc, nt, preferred_element_type=F32)
        m = jnp.max(s_ctx, axis=-1, keepdims=True)
        if local:
            r = step * NA_ROWS_PER_STEP + g
            rs = jnp.clip(r - wr // 2, 0, n_rows - wr)
            start = pl.multiple_of(rs * GRID_W, GRID_W)
            kw = k_ref[0, pl.ds(start, wr * GRID_W), :]
            vw = v_ref[0, pl.ds(start, wr * GRID_W), :]
            s_loc = lax.dot_general(q4, kw, nt, preferred_element_type=F32) + bias_ref[rs - r + wr - 1]
            m = jnp.maximum(m, jnp.max(s_loc, axis=-1, keepdims=True))
            p_loc = jnp.exp(s_loc - m)
        p_ctx = jnp.exp(s_ctx - m)
        denom = jnp.sum(p_ctx, axis=-1, keepdims=True)
        o4 = jnp.dot(p_ctx.astype(BF16), vc, preferred_element_type=F32)
        if local:
            denom = denom + jnp.sum(p_loc, axis=-1, keepdims=True)
            o4 = o4 + jnp.dot(p_loc.astype(BF16), vw, preferred_element_type=F32)
        o4 = o4 / denom
        out = jnp.zeros((GRID_W, NA_WIDTH), F32)
        for h in range(NA_HEADS):
            out = out + jnp.where(head == h, o4[h * GRID_W:(h + 1) * GRID_W], 0.0)
        o_ref[0, rows, :] = out.astype(o_ref.dtype)

    @pl.when(step * NA_ROWS_PER_STEP < n_rows)
    def _():
        for g in range(NA_ROWS_PER_STEP):
            one_row(g, True)

    @pl.when(step * NA_ROWS_PER_STEP >= n_rows)
    def _():
        for g in range(NA_ROWS_PER_STEP):
            one_row(g, False)


def _na_bias_classes(rpb):
    col = np.arange(GRID_W)
    col_start = np.clip(col - NA_WIN_COLS // 2, 0, GRID_W - NA_WIN_COLS)
    in_window = (col[None, :] >= col_start[:, None]) & (col[None, :] < col_start[:, None] + NA_WIN_COLS)
    col_off = np.clip(col[None, :] - col[:, None], -(NA_WIN_COLS - 1), NA_WIN_COLS - 1) + NA_WIN_COLS - 1
    n_off = 2 * NA_WIN_COLS - 1
    pick = (col_off[None, :, :] == np.arange(n_off)[:, None, None]).astype(np.float32)
    by_col = jnp.einsum('hro,oqk->hrqk', rpb.astype(F32), jnp.asarray(pick), precision=HIGHEST)
    by_col = jnp.where(in_window[None, None], by_col, -jnp.inf)
    b = jnp.stack([by_col[:, o:o + NA_WIN_ROWS] for o in range(NA_WIN_ROWS)], axis=0)
    b = b.transpose(0, 1, 3, 2, 4)
    return b.reshape(NA_WIN_ROWS, NA_HEADS * GRID_W, NA_WIN_ROWS * GRID_W)


def _na(na_qkv, bias_cls, n_lat):
    b, s, _ = na_qkv.shape
    n_rows = n_lat // GRID_W
    tq = NA_ROWS_PER_STEP * GRID_W

    return pl.pallas_call(
        functools.partial(_na_body, n_lat=n_lat, n_rows=n_rows),
        grid=(b, s // tq),
        in_specs=[pl.BlockSpec((1, tq, NA_WIDTH), lambda i, r: (i, r, 0)),
                  pl.BlockSpec((1, s, NA_WIDTH), lambda i, r: (i, 0, 1)),
                  pl.BlockSpec((1, s, NA_WIDTH), lambda i, r: (i, 0, 2)),
                  pl.BlockSpec(bias_cls.shape, lambda i, r: (0, 0, 0))],
        out_specs=pl.BlockSpec((1, tq, NA_WIDTH), lambda i, r: (i, r, 0)),
        out_shape=jax.ShapeDtypeStruct((b, s, NA_WIDTH), BF16),
        compiler_params=_cparams(("arbitrary", "arbitrary")),
        name="nbr_attention",
    )(na_qkv, na_qkv, na_qkv, bias_cls)


def _rope_tables(n_lat):
    half = ML_HEAD_DIM // 2
    nf = half // 2
    inv = ROPE_THETA ** (-np.arange(nf, dtype=np.float64) / nf)
    pos = np.arange(n_lat)
    ang = np.concatenate([(pos // GRID_W)[:, None] * inv[None, :]] * 2
                         + [(pos % GRID_W)[:, None] * inv[None, :]] * 2, axis=1)
    lane = np.arange(ML_HEAD_DIM)
    second = (lane % half) >= nf
    cos = np.cos(ang)
    sin = np.sin(ang)
    sin_from_lo = np.where(second[None, :], sin, 0.0)
    sin_from_hi = np.where(second[None, :], 0.0, -sin)
    return (jnp.asarray(cos, F32), jnp.asarray(sin_from_lo, F32), jnp.asarray(sin_from_hi, F32))


def _conv_silu(x, w):
    n = x.shape[0]
    t = lax.broadcasted_iota(jnp.int32, x.shape, 0)
    prev = jnp.where(t >= 1, pltpu.roll(x, 1, 0), 0.0)
    nxt = jnp.where(t < n - 1, pltpu.roll(x, n - 1, 0), 0.0)
    return _silu(w[0:1] * prev + w[1:2] * x + w[2:3] * nxt)


def _rope(x, cos, sin_lo, sin_hi):
    nf = ML_HEAD_DIM // 4
    return x * cos + pltpu.roll(x, nf, 1) * sin_lo + pltpu.roll(x, ML_HEAD_DIM - nf, 1) * sin_hi


def _ml_prep_body(xq_ref, xk_ref, v_ref, wq_ref, wk_ref, cos_ref, slo_ref, shi_ref, k_ref, qt_ref, vt_ref, *, n_lat):
    s = xq_ref.shape[1]
    wq = wq_ref[...]
    wk = wk_ref[...]
    cos, slo, shi = cos_ref[...], slo_ref[...], shi_ref[...]
    kscale = ML_HEAD_DIM ** -0.5
    for lo, hi, rotary in ((0, n_lat, True), (n_lat, s, False)):
        q = _conv_silu(xq_ref[0, lo:hi], wq)
        k = _conv_silu(xk_ref[0, lo:hi], wk) * kscale
        if rotary:
            q = _rope(q, cos, slo, shi)
            k = _rope(k, cos, slo, shi)
        k_ref[0, lo:hi] = k.astype(k_ref.dtype)
        v = v_ref[0, lo:hi].astype(F32)
        for j in range((hi - lo) // ML_CHUNK):
            rows = slice(j * ML_CHUNK, (j + 1) * ML_CHUNK)
            qt_ref[0, 0, lo // ML_CHUNK + j] = q[rows].T.astype(qt_ref.dtype)
            vt_ref[0, 0, lo // ML_CHUNK + j] = v[rows].T.astype(vt_ref.dtype)


def _ml_prep(mlqk, mlv, conv_w, rope, n_lat):
    b, s, _ = mlqk.shape
    hd = ML_HEAD_DIM
    nc = s // ML_CHUNK
    tab = pl.BlockSpec((n_lat, hd), lambda i, h: (0, 0))
    chunks = pl.BlockSpec((1, 1, nc, hd, ML_CHUNK), lambda i, h: (i, h, 0, 0, 0))
    chunk_shape = jax.ShapeDtypeStruct((b, ML_HEADS, nc, hd, ML_CHUNK), BF16)
    return pl.pallas_call(
        functools.partial(_ml_prep_body, n_lat=n_lat),
        grid=(b, ML_HEADS),
        in_specs=[pl.BlockSpec((1, s, hd), lambda i, h: (i, 0, h)),
                  pl.BlockSpec((1, s, hd), lambda i, h: (i, 0, ML_HEADS + h)),
                  pl.BlockSpec((1, s, hd), lambda i, h: (i, 0, h)),
                  pl.BlockSpec((3, hd), lambda i, h: (0, h)),
                  pl.BlockSpec((3, hd), lambda i, h: (0, ML_HEADS + h)),
                  tab, tab, tab],
        out_specs=[pl.BlockSpec((1, s, hd), lambda i, h: (i, 0, h)), chunks, chunks],
        out_shape=[jax.ShapeDtypeStruct((b, s, ML_WIDTH), BF16), chunk_shape, chunk_shape],
        compiler_params=_cparams(("arbitrary", "arbitrary")),
        name="mlstm_prep",
    )(mlqk, mlqk, mlv, conv_w, conv_w, *rope)


def _log_sigmoid(x):
    return jnp.minimum(x, 0.0) - jnp.log1p(jnp.exp(-jnp.abs(x)))


def _ml_gates_body(gc_ref, gr_ref, bc_ref, br_ref, oc_ref, or_ref):
    nc = gr_ref.shape[1]
    c = ML_CHUNK
    ii = lax.broadcasted_iota(jnp.int32, (c, c), 0)
    jj = lax.broadcasted_iota(jnp.int32, (c, c), 1)
    lower = (ii >= jj).astype(F32)
    upper = (ii <= jj).astype(F32)
    lane_kind = lax.broadcasted_iota(jnp.int32, (c, LANES), 1) // ML_HEADS
    row_kind = lax.broadcasted_iota(jnp.int32, (4 * ML_HEADS, c), 0) // ML_HEADS
    for j in range(nc):
        g = gc_ref[0, j * c:(j + 1) * c, :] + bc_ref[...]
        lf = _log_sigmoid(g)
        pre = jnp.dot(lower, lf, precision=HIGHEST, preferred_element_type=F32)
        suf = jnp.dot(upper, lf, precision=HIGHEST, preferred_element_type=F32)
        oc_ref[0, j * c:(j + 1) * c, :] = jnp.where(lane_kind == 1, pre, jnp.where(lane_kind == 3, suf, g))
        gr = gr_ref[0, j] + br_ref[...]
        lfr = _log_sigmoid(gr)
        pre_r = jnp.dot(lfr, upper, precision=HIGHEST, preferred_element_type=F32)
        suf_r = jnp.dot(lfr, lower, precision=HIGHEST, preferred_element_type=F32)
        or_ref[0, j] = jnp.where(row_kind == 1, pre_r, jnp.where(row_kind == 3, suf_r, gr))


def _ml_gates(mlg, gate_bias):
    b, s, _ = mlg.shape
    nc = s // ML_CHUNK
    ng = 4 * ML_HEADS
    g_rows = mlg[:, :, :ng].reshape(b, nc, ML_CHUNK, ng).transpose(0, 1, 3, 2)
    bias_c = jnp.zeros((1, LANES), F32).at[0, :ng].set(gate_bias)
    bias_r = gate_bias.reshape(ng, 1)
    return pl.pallas_call(
        _ml_gates_body,
        grid=(b,),
        in_specs=[pl.BlockSpec((1, s, LANES), lambda i: (i, 0, 0)),
                  pl.BlockSpec((1, nc, ng, ML_CHUNK), lambda i: (i, 0, 0, 0)),
                  pl.BlockSpec((1, LANES), lambda i: (0, 0)),
                  pl.BlockSpec((ng, 1), lambda i: (0, 0))],
        out_specs=[pl.BlockSpec((1, s, LANES), lambda i: (i, 0, 0)),
                   pl.BlockSpec((1, nc, ng, ML_CHUNK), lambda i: (i, 0, 0, 0))],
        out_shape=[jax.ShapeDtypeStruct((b, s, LANES), F32),
                   jax.ShapeDtypeStruct((b, nc, ng, ML_CHUNK), F32)],
        compiler_params=_cparams(("arbitrary",)),
        name="mlstm_gates",
    )(mlg, g_rows, bias_c, bias_r)


def _ml_step(k, qt, vt_aug, a_c, b_r, state_t, m, *, forward):
    c = ML_CHUNK
    d = ML_HEAD_DIM
    si = lax.broadcasted_iota(jnp.int32, (c, c), 0)
    ti = lax.broadcasted_iota(jnp.int32, (c, c), 1)
    allowed = (si <= ti) if forward else (si >= ti)
    log_d = jnp.where(allowed, b_r + a_c, -jnp.inf)
    inter = b_r + m
    m_t = jnp.maximum(jnp.max(log_d, axis=0, keepdims=True), inter)
    kq = jnp.dot(k, qt, preferred_element_type=F32)
    s_t = (kq * jnp.exp(log_d - m_t)).astype(BF16)
    w_inter = jnp.exp(inter - m_t)
    tot = (jnp.dot(vt_aug, s_t, preferred_element_type=F32)
           + w_inter * jnp.dot(state_t.astype(BF16), qt, preferred_element_type=F32))
    num = tot[:d]
    den = tot[d:d + 1]
    h_t = num / jnp.maximum(jnp.abs(den), jnp.exp(-m_t))
    b_end = b_r[:, c - 1:c] if forward else b_r[:, 0:1]
    log_w = b_end + a_c
    m_new = jnp.maximum(b_end + m, jnp.max(log_w, axis=0, keepdims=True))
    w_c = jnp.exp(log_w - m_new)
    decay = jnp.exp(b_end + m - m_new)
    kw = (k.astype(F32) * w_c).astype(BF16)
    state_new = decay * state_t + jnp.dot(vt_aug, kw, preferred_element_type=F32)
    return h_t, state_new, m_new


def _ml_scan_body(k_ref, qt_ref, vt_ref, gc_ref, gr_ref, o_ref, nw_ref, out_ref, hsum_ref, *, n_lat):
    c = ML_CHUNK
    d = ML_HEAD_DIM
    s = k_ref.shape[1]
    nc_lat = n_lat // c
    nc_all = s // c
    ones_row = (lax.broadcasted_iota(jnp.int32, (d, c), 0) == 0).astype(BF16)
    nw = nw_ref[...]

    def rows(j):
        return pl.ds(j * c if isinstance(j, int) else pl.multiple_of(j * c, c), c)

    def run(j, state, m, forward):
        vt_aug = jnp.concatenate([vt_ref[0, 0, j], ones_row], axis=0)
        gc = gc_ref[0, 0, rows(j), :]
        gr = gr_ref[0, 0, j]
        k0 = 0 if forward else 2
        a_c = gc[:, k0:k0 + 1] - gc[:, k0 + 1:k0 + 2]
        return _ml_step(k_ref[0, rows(j), :], qt_ref[0, 0, j], vt_aug, a_c, gr[k0 + 1:k0 + 2],
                        state, m, forward=forward)

    def finalize(j, hsum_t):
        hc = hsum_t - jnp.mean(hsum_t, axis=0, keepdims=True)
        hn = (hc * lax.rsqrt(jnp.mean(hc * hc, axis=0, keepdims=True) + LN_EPS)).T
        out_ref[0, rows(j), :] = (_sigmoid(o_ref[0, rows(j), :].astype(F32)) * hn * nw).astype(out_ref.dtype)

    zero_state = jnp.zeros((2 * d, d), F32)
    m0 = jnp.full((1, 1), ML_M_INIT, F32)
    sf, mf = zero_state, m0
    hc_f = []
    for j in range(nc_lat, nc_all):
        h, sf, mf = run(j, sf, mf, True)
        hc_f.append(h)
    sb, mb = zero_state, m0
    hc_b = {}
    for j in range(nc_all - 1, nc_lat - 1, -1):
        h, sb, mb = run(j, sb, mb, False)
        hc_b[j] = h
    for idx, j in enumerate(range(nc_lat, nc_all)):
        finalize(j, hc_f[idx] + hc_b[j])

    half = nc_lat // 2

    def first_half(i, carry):
        sf, mf, sb, mb = carry
        jb = nc_lat - 1 - i
        hf, sf, mf = run(i, sf, mf, True)
        hb, sb, mb = run(jb, sb, mb, False)
        hsum_ref[i] = hf
        hsum_ref[jb] = hb
        return sf, mf, sb, mb

    carry = lax.fori_loop(0, half, first_half, (sf, mf, sb, mb))

    def second_half(i, carry):
        sf, mf, sb, mb = carry
        jb = nc_lat - 1 - i
        hf, sf, mf = run(i, sf, mf, True)
        hb, sb, mb = run(jb, sb, mb, False)
        finalize(i, hf + hsum_ref[i])
        finalize(jb, hb + hsum_ref[jb])
        return sf, mf, sb, mb

    lax.fori_loop(half, nc_lat, second_half, carry)


def _ml_scan(k, qt, vt, gcol, grow, mlo, norm_w, n_lat):
    b, s, _ = k.shape
    hd = ML_HEAD_DIM
    nc = s // ML_CHUNK
    ng = 4 * ML_HEADS
    gc_h = gcol[:, :, :ng].reshape(b, s, 4, ML_HEADS).transpose(0, 3, 1, 2)
    gr_h = grow.reshape(b, nc, 4, ML_HEADS, ML_CHUNK).transpose(0, 3, 1, 2, 4)
    tok = pl.BlockSpec((1, s, hd), lambda i, h: (i, 0, h))
    chunks = pl.BlockSpec((1, 1, nc, hd, ML_CHUNK), lambda i, h: (i, h, 0, 0, 0))
    return pl.pallas_call(
        functools.partial(_ml_scan_body, n_lat=n_lat),
        grid=(b, ML_HEADS),
        in_specs=[tok, chunks, chunks,
                  pl.BlockSpec((1, 1, s, 4), lambda i, h: (i, h, 0, 0)),
                  pl.BlockSpec((1, 1, nc, 4, ML_CHUNK), lambda i, h: (i, h, 0, 0, 0)),
                  tok,
                  pl.BlockSpec((1, hd), lambda i, h: (0, h))],
        out_specs=tok,
        out_shape=jax.ShapeDtypeStruct((b, s, ML_WIDTH), BF16),
        scratch_shapes=[pltpu.VMEM((n_lat // ML_CHUNK, hd, ML_CHUNK), F32)],
        compiler_params=_cparams(("arbitrary", "arbitrary")),
        name="mlstm_scan",
    )(k, qt, vt, gc_h, gr_h, mlo, norm_w)


def _top2(logits, n_valid):
    lane = lax.broadcasted_iota(jnp.int32, logits.shape, 1).astype(F32)
    neg = jnp.float32(-jnp.inf)
    l = jnp.where(lane < n_valid, logits, neg)
    m1 = jnp.max(l, axis=-1, keepdims=True)
    e1 = jnp.min(jnp.where(l == m1, lane, float(LANES)), axis=-1, keepdims=True)
    l2 = jnp.where(lane == e1, neg, l)
    m2 = jnp.max(l2, axis=-1, keepdims=True)
    e2 = jnp.min(jnp.where(l2 == m2, lane, float(LANES)), axis=-1, keepdims=True)
    x2 = jnp.exp(m2 - m1)
    p1 = 1.0 / (1.0 + x2)
    p2 = x2 / (1.0 + x2)
    return jnp.where(lane == 0, e1, jnp.where(lane == 1, e2, jnp.where(lane == 2, p1, jnp.where(lane == 3, p2, 0.0))))


def _route_ranks(route, valid, cnt_ref):
    tm = route.shape[0]
    lane = lax.broadcasted_iota(jnp.int32, route.shape, 1).astype(F32)
    pick1 = jnp.where(valid, (lane == route[:, 0:1]).astype(F32), 0.0)
    pick2 = jnp.where(valid, (lane == route[:, 1:2]).astype(F32), 0.0)
    both = pick1 + pick2
    before = (lax.broadcasted_iota(jnp.int32, (tm, tm), 0) > lax.broadcasted_iota(jnp.int32, (tm, tm), 1))
    prefix = jnp.dot(before.astype(BF16), both.astype(BF16), preferred_element_type=F32) + cnt_ref[...]
    rank1 = jnp.sum(prefix * pick1, axis=-1, keepdims=True)
    rank2 = jnp.sum(prefix * pick2, axis=-1, keepdims=True)
    cnt_ref[...] += jnp.sum(both, axis=0, keepdims=True)
    return jnp.where(lane == 4, rank1, jnp.where(lane == 5, rank2, route))


def _merge_body(*refs, n_lat, alpha, moe, n_rows_tok):
    (a_ref, n_ref, m_ref, u_ref, x_ref, wbp_ref, wbn_ref, wbm_ref, wg_ref, wo_ref,
     g1l_ref, g1c_ref, shl_ref, shc_ref, scl_ref, scc_ref, lng_ref, lnb_ref) = refs[:18]
    if moe:
        rw_ref, rb_ref, x_out, tok_out, route_out, cnt_out, cnt_ref = refs[18:]
    else:
        x_out, tok_out = refs[18:]
    tm = x_ref.shape[1]
    d = x_ref.shape[2]
    is_ctx = _is_ctx_rows(pl.program_id(1), tm, n_lat)
    u = u_ref[0]
    y = jnp.zeros((tm, d), F32)
    for k, (br_ref, w_ref) in enumerate(((a_ref, wbp_ref), (n_ref, wbn_ref), (m_ref, wbm_ref))):
        gate = _sigmoid(jnp.dot(u, wg_ref[:, k * d:(k + 1) * d], preferred_element_type=F32))
        y = y + gate * jnp.dot(br_ref[0], w_ref[...], preferred_element_type=F32)
    y = jnp.dot(y.astype(BF16), wo_ref[...], preferred_element_type=F32)
    z = alpha * x_ref[0] + _tile_mod(g1l_ref, g1c_ref, is_ctx) * y
    xn = _ln_plain(z) * lng_ref[...] + lnb_ref[...]
    x_out[0] = xn
    tok = _ln_plain(xn) * (1.0 + _tile_mod(scl_ref, scc_ref, is_ctx)) + _tile_mod(shl_ref, shc_ref, is_ctx)
    tok_out[0] = tok.astype(tok_out.dtype)
    if moe:
        @pl.when((pl.program_id(0) == 0) & (pl.program_id(1) == 0))
        def _():
            cnt_ref[...] = jnp.zeros_like(cnt_ref)

        lane = lax.broadcasted_iota(jnp.int32, (tm, LANES), 1)
        logits = jnp.zeros((tm, LANES), F32)
        for e in range(N_EXPERTS):
            col = jnp.sum(tok * rw_ref[e:e + 1, :], axis=-1, keepdims=True)
            logits = jnp.where(lane == e, col, logits)
        valid = (pl.program_id(1) * tm + lax.broadcasted_iota(jnp.int32, (tm, 1), 0)) < n_rows_tok
        route_out[0] = _route_ranks(_top2(logits + rb_ref[...], N_EXPERTS), valid, cnt_ref)
        cnt_out[...] = cnt_ref[...]


def _merge(a, n, m, u, x_all, w_bp, w_bn, w_bm, w_gate, w_o, g1, sh2, sc2, ln_g, ln_b, n_lat, alpha,
           router=None, n_rows_tok=0):
    b, s, d = x_all.shape
    tm = TM_MERGE
    moe = router is not None

    def tok(w):
        return pl.BlockSpec((1, tm, w), lambda i, j: (i, j, 0))

    def full(arr):
        return pl.BlockSpec(arr.shape, lambda i, j: (0,) * arr.ndim)

    in_specs = ([tok(a.shape[2]), tok(n.shape[2]), tok(m.shape[2]), tok(d), tok(d),
                 full(w_bp), full(w_bn), full(w_bm), full(w_gate), full(w_o)]
                + _row_specs(d, b) * 3 + [full(ln_g), full(ln_b)])
    args = [a, n, m, u, x_all, w_bp, w_bn, w_bm, w_gate, w_o, g1, g1, sh2, sh2, sc2, sc2, ln_g, ln_b]
    out_specs = [tok(d), tok(d)]
    out_shape = [jax.ShapeDtypeStruct((b, s, d), F32), jax.ShapeDtypeStruct((b, s, d), F32 if moe else BF16)]
    scratch = []
    if moe:
        in_specs += [full(router[0]), full(router[1])]
        args += list(router)
        out_specs += [tok(LANES), pl.BlockSpec((1, LANES), lambda i, j: (0, 0))]
        out_shape += [jax.ShapeDtypeStruct((b, s, LANES), F32), jax.ShapeDtypeStruct((1, LANES), F32)]
        scratch = [pltpu.VMEM((1, LANES), F32)]
    return pl.pallas_call(
        functools.partial(_merge_body, n_lat=n_lat, alpha=alpha, moe=moe, n_rows_tok=n_rows_tok),
        grid=(b, s // tm),
        in_specs=in_specs,
        out_specs=out_specs,
        out_shape=out_shape,
        scratch_shapes=scratch,
        compiler_params=_cparams(("arbitrary", "arbitrary")),
        name="merge",
    )(*args)


def _finish_rows(f, x, g2, lng, lnb, alpha):
    return _ln_plain(alpha * x + g2 * f) * lng + lnb


def _ffn_body(*refs, n_lat, alpha, with_next):
    (t_ref, x_ref, w1_ref, w3_ref, w2_ref, g2l_ref, g2c_ref, lng_ref, lnb_ref) = refs[:9]
    if with_next:
        shl_ref, shc_ref, scl_ref, scc_ref, x_out, u_out, acc_ref = refs[9:]
    else:
        x_out, acc_ref = refs[9:]
    f = pl.program_id(2)
    t = t_ref[0]
    h1 = jnp.dot(t, w1_ref[...], preferred_element_type=F32)
    h3 = jnp.dot(t, w3_ref[...], preferred_element_type=F32)
    contrib = jnp.dot((_silu(h1) * h3).astype(BF16), w2_ref[...], preferred_element_type=F32)

    @pl.when(f == 0)
    def _():
        acc_ref[...] = contrib

    @pl.when(f > 0)
    def _():
        acc_ref[...] += contrib

    @pl.when(f == pl.num_programs(2) - 1)
    def _():
        tm = x_ref.shape[1]
        is_ctx = _is_ctx_rows(pl.program_id(1), tm, n_lat)
        xn = _finish_rows(acc_ref[...], x_ref[0], _tile_mod(g2l_ref, g2c_ref, is_ctx), lng_ref[...], lnb_ref[...], alpha)
        x_out[0] = xn
        if with_next:
            u = _ln_plain(xn) * (1.0 + _tile_mod(scl_ref, scc_ref, is_ctx)) + _tile_mod(shl_ref, shc_ref, is_ctx)
            u_out[0] = u.astype(u_out.dtype)


def _ffn(tok, x_all, w1, w3, w2, g2, ln_g, ln_b, n_lat, alpha, nxt=None):
    b, s, d = x_all.shape
    dff = w1.shape[1]
    tm, tf = TM_TOKEN, TF_FFN
    with_next = nxt is not None

    def tokspec():
        return pl.BlockSpec((1, tm, d), lambda i, j, f: (i, j, 0))

    def rows():
        return [pl.BlockSpec((1, 1, d), lambda i, j, f: (i, 0, 0)),
                pl.BlockSpec((1, 1, d), lambda i, j, f: (b, 0, 0))]

    vec = pl.BlockSpec((1, d), lambda i, j, f: (0, 0))
    in_specs = [tokspec(), tokspec(),
                pl.BlockSpec((d, tf), lambda i, j, f: (0, f)),
                pl.BlockSpec((d, tf), lambda i, j, f: (0, f)),
                pl.BlockSpec((tf, d), lambda i, j, f: (f, 0))] + rows() + [vec, vec]
    args = [tok, x_all, w1, w3, w2, g2, g2, ln_g, ln_b]
    out_specs = [tokspec()]
    out_shape = [jax.ShapeDtypeStruct((b, s, d), F32)]
    if with_next:
        in_specs += rows() + rows()
        args += [nxt[0], nxt[0], nxt[1], nxt[1]]
        out_specs.append(tokspec())
        out_shape.append(jax.ShapeDtypeStruct((b, s, d), BF16))
    res = pl.pallas_call(
        functools.partial(_ffn_body, n_lat=n_lat, alpha=alpha, with_next=with_next),
        grid=(b, s // tm, dff // tf),
        in_specs=in_specs,
        out_specs=out_specs,
        out_shape=out_shape,
        scratch_shapes=[pltpu.VMEM((tm, d), F32)],
        compiler_params=_cparams(("arbitrary", "arbitrary", "arbitrary")),
        name="ffn_dense",
    )(*args)
    return res if with_next else (res[0], None)


def _moe_tables(route, counts, n_rows_tok):
    r = MOE_ROWS
    tm = TM_MOE
    b = route.shape[0]
    cnt = counts[0, :N_EXPERTS].astype(jnp.int32)
    nblk = (cnt + r - 1) // r
    bend = jnp.cumsum(nblk)
    bstart = bend - nblk
    total = bend[-1]
    n_blocks = -(-(b * n_rows_tok * TOP_K) // r) + N_EXPERTS
    part = route[:, :n_rows_tok]
    experts = jnp.arange(N_EXPERTS, dtype=F32)
    seg_row0 = jnp.sum((part[:, :, 0:TOP_K, None] == experts).astype(jnp.int32) * (bstart * r), axis=-1)
    dest = seg_row0 + part[:, :, 4:4 + TOP_K].astype(jnp.int32)
    nt = n_rows_tok // tm
    dest_tiles = dest.reshape(b, nt, tm, TOP_K).transpose(0, 1, 3, 2).reshape(b * nt, 1, TOP_K * tm)
    blk = jnp.arange(n_blocks, dtype=jnp.int32)
    e_of = jnp.sum((blk[:, None] >= bend[None, :]).astype(jnp.int32), axis=1)
    e_last = jnp.sum((total - 1 >= bend).astype(jnp.int32))
    active = (blk < total).astype(jnp.int32)
    block_e = jnp.minimum(e_of, e_last)
    spare = total + jnp.arange(N_EXPERTS, dtype=jnp.int32)
    zero_blk = jnp.concatenate([bend - 1, spare])
    zero_on = jnp.concatenate([nblk > 0, spare < n_blocks]).astype(jnp.int32)
    zero_blk = jnp.clip(zero_blk, 0, n_blocks - 1)
    return dest_tiles, zero_blk, zero_on, block_e, active


def _dispatch_body(zblk_ref, zon_ref, dest_ref, tok_ref, xs_hbm, zbuf, zsem, dsem):
    i = pl.program_id(0)
    tm = tok_ref.shape[1]
    r = zbuf.shape[0]

    def zero_copy(z):
        start = pl.multiple_of(zblk_ref[z] * r, r)
        return pltpu.make_async_copy(zbuf, xs_hbm.at[pl.ds(start, r)], zsem)

    @pl.when(i == 0)
    def _():
        zbuf[...] = jnp.zeros_like(zbuf)
        for z in range(2 * N_EXPERTS):
            @pl.when(zon_ref[z] > 0)
            def _():
                zero_copy(z).start()
        for z in range(2 * N_EXPERTS):
            @pl.when(zon_ref[z] > 0)
            def _():
                zero_copy(z).wait()

    def row_copy(j, k):
        return pltpu.make_async_copy(tok_ref.at[0, pl.ds(j, 1)],
                                     xs_hbm.at[pl.ds(dest_ref[0, 0, k * tm + j], 1)], dsem)

    def issue(j, c):
        for k in range(TOP_K):
            row_copy(j, k).start()
        return c

    lax.fori_loop(0, tm, issue, 0, unroll=8)
    for _ in range(TOP_K):
        pltpu.make_async_copy(tok_ref.at[0], xs_hbm.at[pl.ds(0, tm)], dsem).wait()


def _moe_dispatch(tok, zero_blk, zero_on, dest_tiles, n_blocks, n_rows_tok):
    d = tok.shape[2]
    tm = TM_MOE
    nt = n_rows_tok // tm
    grid_spec = pltpu.PrefetchScalarGridSpec(
        num_scalar_prefetch=2,
        grid=(dest_tiles.shape[0],),
        in_specs=[pl.BlockSpec((1, 1, TOP_K * tm), lambda i, zb, zo: (i, 0, 0), memory_space=pltpu.SMEM),
                  pl.BlockSpec((1, tm, d), lambda i, zb, zo: (i // nt, i % nt, 0))],
        out_specs=pl.BlockSpec(memory_space=pl.ANY),
        scratch_shapes=[pltpu.VMEM((MOE_ROWS, d), F32),
                        pltpu.SemaphoreType.DMA(()), pltpu.SemaphoreType.DMA(())],
    )
    return pl.pallas_call(
        _dispatch_body,
        grid_spec=grid_spec,
        out_shape=jax.ShapeDtypeStruct((n_blocks * MOE_ROWS, d), F32),
        compiler_params=_cparams(("arbitrary",)),
        name="moe_dispatch",
    )(zero_blk, zero_on, dest_tiles, tok)


def _experts_body(be_ref, act_ref, x_ref, w1_ref, w3_ref, w2_ref, y_ref, xb_ref):
    i = pl.program_id(0)
    f = pl.program_id(1)

    @pl.when((act_ref[i] == 0) & (f == 0))
    def _():
        y_ref[...] = jnp.zeros_like(y_ref)

    @pl.when(act_ref[i] > 0)
    def _():
        @pl.when(f == 0)
        def _():
            xb_ref[...] = x_ref[...].astype(BF16)

        x = xb_ref[...]
        h1 = jnp.dot(x, w1_ref[0, 0].astype(BF16), preferred_element_type=F32)
        h3 = jnp.dot(x, w3_ref[0, 0].astype(BF16), preferred_element_type=F32)
        contrib = jnp.dot((_silu(h1) * h3).astype(BF16), w2_ref[0, 0].astype(BF16), preferred_element_type=F32)

        @pl.when(f == 0)
        def _():
            y_ref[...] = contrib

        @pl.when(f > 0)
        def _():
            y_ref[...] += contrib


def _moe_experts(xs, block_e, active, w1, w3, w2, li):
    d = xs.shape[1]
    dff = w1.shape[3]
    r, tf = MOE_ROWS, TF_FFN
    nf = dff // tf
    n_blocks = block_e.shape[0]

    def f_eff(i, f, act):
        return jnp.where(act[i] > 0, f, nf - 1)

    grid_spec = pltpu.PrefetchScalarGridSpec(
        num_scalar_prefetch=2,
        grid=(n_blocks, nf),
        in_specs=[pl.BlockSpec((r, d), lambda i, f, be, act: (i, 0)),
                  pl.BlockSpec((1, 1, d, tf), lambda i, f, be, act: (li, be[i], 0, f_eff(i, f, act))),
                  pl.BlockSpec((1, 1, d, tf), lambda i, f, be, act: (li, be[i], 0, f_eff(i, f, act))),
                  pl.BlockSpec((1, 1, tf, d), lambda i, f, be, act: (li, be[i], f_eff(i, f, act), 0))],
        out_specs=pl.BlockSpec((r, d), lambda i, f, be, act: (i, 0)),
        scratch_shapes=[pltpu.VMEM((r, d), BF16)],
    )
    return pl.pallas_call(
        _experts_body,
        grid_spec=grid_spec,
        out_shape=jax.ShapeDtypeStruct(xs.shape, F32),
        compiler_params=_cparams(("arbitrary", "arbitrary")),
        name="moe_experts",
    )(block_e, active, xs, w1, w3, w2)


def _combine_body(*refs, n_lat, alpha, with_next):
    (dest_ref, dnext_ref, ys_hbm, p_ref, x_ref, g2l_ref, g2c_ref, lng_ref, lnb_ref) = refs[:9]
    if with_next:
        shl_ref, shc_ref, scl_ref, scc_ref, x_out, u_out, ybuf, sem = refs[9:]
    else:
        x_out, ybuf, sem = refs[9:]
    tm = x_ref.shape[1]
    t = pl.program_id(0) * pl.num_programs(1) + pl.program_id(1)
    n_tiles = pl.num_programs(0) * pl.num_programs(1)
    slot = t % 2

    def gather_tile(table_ref, to_slot):
        def issue(j, c):
            for k in range(TOP_K):
                pltpu.make_async_copy(ys_hbm.at[pl.ds(table_ref[0, 0, k * tm + j], 1)],
                                      ybuf.at[to_slot, k, pl.ds(j, 1)], sem.at[to_slot]).start()
            return c

        lax.fori_loop(0, tm, issue, 0, unroll=8)

    @pl.when(t == 0)
    def _():
        gather_tile(dest_ref, 0)

    @pl.when(t + 1 < n_tiles)
    def _():
        gather_tile(dnext_ref, 1 - slot)

    for k in range(TOP_K):
        pltpu.make_async_copy(ys_hbm.at[pl.ds(0, tm)], ybuf.at[slot, k], sem.at[slot]).wait()

    is_ctx = _is_ctx_rows(pl.program_id(1), tm, n_lat)
    p = p_ref[0]
    f = p[:, 2:3] * ybuf[slot, 0] + p[:, 3:4] * ybuf[slot, 1]
    xn = _finish_rows(f, x_ref[0], _tile_mod(g2l_ref, g2c_ref, is_ctx), lng_ref[...], lnb_ref[...], alpha)
    x_out[0] = xn
    if with_next:
        u = _ln_plain(xn) * (1.0 + _tile_mod(scl_ref, scc_ref, is_ctx)) + _tile_mod(shl_ref, shc_ref, is_ctx)
        u_out[0] = u.astype(u_out.dtype)


def _combine(ys, dest_tiles, route, x_all, g2, ln_g, ln_b, n_lat, alpha, n_tok_rows, nxt=None):
    b, s, d = x_all.shape
    tm = TM_MOE
    nt = n_tok_rows // tm
    with_next = nxt is not None

    def tok(w):
        return pl.BlockSpec((1, tm, w), lambda i, j: (i, j, 0))

    vec = pl.BlockSpec((1, d), lambda i, j: (0, 0))
    last_tile = b * nt - 1
    in_specs = [pl.BlockSpec((1, 1, TOP_K * tm), lambda i, j: (i * nt + j, 0, 0), memory_space=pltpu.SMEM),
                pl.BlockSpec((1, 1, TOP_K * tm), lambda i, j: (jnp.minimum(i * nt + j + 1, last_tile), 0, 0),
                             memory_space=pltpu.SMEM),
                pl.BlockSpec(memory_space=pl.ANY), tok(LANES), tok(d)] + _row_specs(d, b) + [vec, vec]
    args = [dest_tiles, dest_tiles, ys, route, x_all, g2, g2, ln_g, ln_b]
    out_specs = [tok(d)]
    out_shape = [jax.ShapeDtypeStruct((b, n_tok_rows, d), F32)]
    if with_next:
        in_specs += _row_specs(d, b) * 2
        args += [nxt[0], nxt[0], nxt[1], nxt[1]]
        out_specs.append(tok(d))
        out_shape.append(jax.ShapeDtypeStruct((b, n_tok_rows, d), BF16))
    res = pl.pallas_call(
        functools.partial(_combine_body, n_lat=n_lat, alpha=alpha, with_next=with_next),
        grid=(b, nt),
        in_specs=in_specs,
        out_specs=out_specs,
        out_shape=out_shape,
        scratch_shapes=[pltpu.VMEM((2, TOP_K, tm, d), F32), pltpu.SemaphoreType.DMA((2,))],
        compiler_params=_cparams(("arbitrary", "arbitrary")),
        name="moe_combine",
    )(*args)
    return res if with_next else (res[0], None)


def _block_diag(pool_w):
    g, c, _ = pool_w.shape
    out = jnp.zeros((g * c, g * c), pool_w.dtype)
    for i in range(g):
        out = out.at[i * c:(i + 1) * c, i * c:(i + 1) * c].set(pool_w[i])
    return out


def kernel(x, c, ctx, c_ctx, w_mod, b_mod, w_in, ml_gate_bias, ml_conv_w, ml_norm_w, pool_w, pool_scale,
           na_rpb, w_branch_pool, w_branch_na, w_branch_ml, w_out, ln1_g, ln1_b, ln2_g, ln2_b,
           ffn_w1, ffn_w3, ffn_w2, moe_router_w, moe_router_b, moe_w1, moe_w3, moe_w2):
    b, n_lat, d = x.shape
    n_ctx = ctx.shape[1]
    s = n_lat + n_ctx
    depth = w_in.shape[0]
    alpha = (2 * depth) ** 0.25
    off_g = POOL_WIDTH + 3 * NA_WIDTH + 4 * ML_WIDTH
    n_gate_cols = 4 * ML_HEADS

    x_all = jnp.concatenate([x, ctx], axis=1)
    mod = _mod_vectors(c, c_ctx, w_mod, b_mod)

    def mod_part(layer, k):
        return mod[layer, :, k * d:(k + 1) * d].reshape(MOD_ROWS, 1, d)

    rope = _rope_tables(n_lat)
    u = _ln_mod(x_all, mod_part(0, 0), mod_part(0, 1), n_lat)

    for layer in range(depth):
        last = layer == depth - 1
        w_l = w_in[layer]
        w_main = w_l[:, :off_g].astype(BF16)
        w_g = jnp.zeros((d, LANES), BF16).at[:, :n_gate_cols].set(w_l[:, off_g:off_g + n_gate_cols].astype(BF16))
        w_gate = w_l[:, off_g + n_gate_cols:].astype(BF16)
        pool_in, na_qkv, mlqk, mlv, mlo, mlg = _proj_main(u, w_main, w_g)

        a = _pool(pool_in, _block_diag(pool_w[layer]).astype(BF16), pool_scale[layer].reshape(1, -1), n_lat)
        n = _na(na_qkv, _na_bias_classes(na_rpb[layer]), n_lat)
        k_ml, qt_ml, vt_ml = _ml_prep(mlqk, mlv, ml_conv_w[layer], rope, n_lat)
        gcol, grow = _ml_gates(mlg, ml_gate_bias[layer])
        m = _ml_scan(k_ml, qt_ml, vt_ml, gcol, grow, mlo, ml_norm_w[layer].reshape(1, -1), n_lat)

        is_moe = layer % 2 == 1
        i = layer // 2
        router = None
        if is_moe:
            rw = moe_router_w[i].T
            rb = jnp.zeros((1, LANES), F32).at[0, :N_EXPERTS].set(moe_router_b[i])
            router = (rw, rb)
        n_rows_tok = n_lat if last else s
        merged = _merge(a, n, m, u, x_all,
                        w_branch_pool[layer].astype(BF16), w_branch_na[layer].astype(BF16),
                        w_branch_ml[layer].astype(BF16), w_gate, w_out[layer].astype(BF16),
                        mod_part(layer, 2), mod_part(layer, 3), mod_part(layer, 4),
                        ln1_g[layer].reshape(1, d), ln1_b[layer].reshape(1, d), n_lat, alpha, router, n_rows_tok)
        nxt = None if last else (mod_part(layer + 1, 0), mod_part(layer + 1, 1))
        g2 = mod_part(layer, 5)
        lng, lnb = ln2_g[layer].reshape(1, d), ln2_b[layer].reshape(1, d)
        if not is_moe:
            x_mid, tok = merged
            x_all, u = _ffn(tok, x_mid, ffn_w1[i].astype(BF16), ffn_w3[i].astype(BF16), ffn_w2[i].astype(BF16),
                            g2, lng, lnb, n_lat, alpha, nxt)
        else:
            x_mid, tok, route, counts = merged
            x_all, u = _moe_layer(tok, route, counts, x_mid, moe_w1, moe_w3, moe_w2, i, g2, lng, lnb,
                                  n_lat, alpha, n_rows_tok, nxt)
    return x_all[:, :n_lat]


def _moe_layer(tok, route, counts, x_mid, w1, w3, w2, li, g2, lng, lnb, n_lat, alpha, n_rows_tok, nxt):
    dest_tiles, zero_blk, zero_on, block_e, active = _moe_tables(route, counts, n_rows_tok)
    xs = _moe_dispatch(tok, zero_blk, zero_on, dest_tiles, block_e.shape[0], n_rows_tok)
    ys = _moe_experts(xs, block_e, active, w1, w3, w2, li)
    return _combine(ys, dest_tiles, route, x_mid, g2, lng, lnb, n_lat, alpha, n_rows_tok, nxt)
```

```python
import functools

import numpy as np
import jax
import jax.numpy as jnp
from jax import lax
from jax.experimental import pallas as pl
from jax.experimental.pallas import tpu as pltpu

F32 = jnp.float32
BF16 = jnp.bfloat16
HIGHEST = lax.Precision.HIGHEST

GRID_W = 64
POOL_GROUPS = 4
POOL_GROUP_DIM = 64
POOL_WIDTH = POOL_GROUPS * POOL_GROUP_DIM
POOL_HALF_WINDOWS = (1, 2, 4, 8)
NA_HEADS = 4
NA_HEAD_DIM = 64
NA_WIDTH = NA_HEADS * NA_HEAD_DIM
NA_WIN_ROWS = 8
NA_WIN_COLS = 16
ML_HEADS = 4
ML_HEAD_DIM = 128
ML_WIDTH = ML_HEADS * ML_HEAD_DIM
ML_CHUNK = 128
ML_M_INIT = -1e30
ROPE_THETA = 10000.0
N_EXPERTS = 8
TOP_K = 2
N_BRANCHES = 3
LN_EPS = 1e-6

LANES = 128
MOD_ROWS = 16
VMEM_LIMIT = 56 * 1024 * 1024

TM_TOKEN = 768
TM_MERGE = 768
TF_FFN = 1792
TF_MOE = 512
MOE_ROWS = 1024
TM_MOE = 256


def _cparams(sem):
    return pltpu.CompilerParams(dimension_semantics=sem, vmem_limit_bytes=VMEM_LIMIT)


def _sigmoid(x):
    return 0.5 + 0.5 * jnp.tanh(0.5 * x)


def _silu(x):
    t = 0.5 * x
    return t + t * jnp.tanh(t)


def _ln_plain(x):
    xc = x - jnp.mean(x, axis=-1, keepdims=True)
    return xc * lax.rsqrt(jnp.mean(xc * xc, axis=-1, keepdims=True) + LN_EPS)


def _is_ctx_rows(tile_idx, tm, n_lat):
    row = tile_idx * tm + lax.broadcasted_iota(jnp.int32, (tm, 1), 0)
    return row >= n_lat


def _tile_mod(lat_ref, ctx_ref, is_ctx):
    return jnp.where(is_ctx, ctx_ref[0], lat_ref[0])


def _mod_body(s_ref, w_ref, b_ref, o_ref):
    s = _silu(s_ref[...])
    o_ref[0] = jnp.dot(s, w_ref[0], precision=HIGHEST, preferred_element_type=F32) + b_ref[0]


def _mod_vectors(c, c_ctx, w_mod, b_mod):
    depth, d, d6 = w_mod.shape
    b = c.shape[0]
    s = jnp.zeros((MOD_ROWS, d), F32).at[:b].set(c).at[b].set(c_ctx)
    tn = 1536
    return pl.pallas_call(
        _mod_body,
        grid=(depth, d6 // tn),
        in_specs=[pl.BlockSpec((MOD_ROWS, d), lambda l, j: (0, 0)),
                  pl.BlockSpec((1, d, tn), lambda l, j: (l, 0, j)),
                  pl.BlockSpec((1, 1, tn), lambda l, j: (l, 0, j))],
        out_specs=pl.BlockSpec((1, MOD_ROWS, tn), lambda l, j: (l, 0, j)),
        out_shape=jax.ShapeDtypeStruct((depth, MOD_ROWS, d6), F32),
        compiler_params=_cparams(("arbitrary", "arbitrary")),
        name="mod_vectors",
    )(s, w_mod, b_mod.reshape(depth, 1, d6))


def _ln_mod_body(x_ref, shl_ref, shc_ref, scl_ref, scc_ref, u_ref, *, n_lat):
    tm = x_ref.shape[1]
    is_ctx = _is_ctx_rows(pl.program_id(1), tm, n_lat)
    sh = _tile_mod(shl_ref, shc_ref, is_ctx)
    sc = _tile_mod(scl_ref, scc_ref, is_ctx)
    u_ref[0] = (_ln_plain(x_ref[0]) * (1.0 + sc) + sh).astype(u_ref.dtype)


def _row_specs(d, nb):
    return [pl.BlockSpec((1, 1, d), lambda b, j: (b, 0, 0)),
            pl.BlockSpec((1, 1, d), lambda b, j: (nb, 0, 0))]


def _ln_mod(x_all, shift, scale, n_lat):
    b, s, d = x_all.shape
    tm = TM_TOKEN
    return pl.pallas_call(
        functools.partial(_ln_mod_body, n_lat=n_lat),
        grid=(b, s // tm),
        in_specs=[pl.BlockSpec((1, tm, d), lambda i, j: (i, j, 0))] + _row_specs(d, b) + _row_specs(d, b),
        out_specs=pl.BlockSpec((1, tm, d), lambda i, j: (i, j, 0)),
        out_shape=jax.ShapeDtypeStruct((b, s, d), BF16),
        compiler_params=_cparams(("arbitrary", "arbitrary")),
        name="ln_mod",
    )(x_all, shift, shift, scale, scale)


def _proj_main_body(u_ref, w_ref, wg_ref, pool_ref, na_ref, mlqk_ref, mlv_ref, mlo_ref, mlg_ref):
    u = u_ref[0]

    def seg(lo, hi):
        return jnp.dot(u, w_ref[:, lo:hi], preferred_element_type=F32)

    off_na = POOL_WIDTH
    off_ml = off_na + 3 * NA_WIDTH
    off_mlv = off_ml + 2 * ML_WIDTH
    off_mlo = off_ml + 3 * ML_WIDTH
    pool_ref[0] = seg(0, off_na)
    na_ref[0, :, :NA_WIDTH] = (seg(off_na, off_na + NA_WIDTH) * NA_HEAD_DIM ** -0.5).astype(na_ref.dtype)
    na_ref[0, :, NA_WIDTH:] = seg(off_na + NA_WIDTH, off_ml).astype(na_ref.dtype)
    mlqk_ref[0] = seg(off_ml, off_mlv)
    mlv_ref[0] = seg(off_mlv, off_mlo).astype(mlv_ref.dtype)
    mlo_ref[0] = seg(off_mlo, off_mlo + ML_WIDTH).astype(mlo_ref.dtype)
    mlg_ref[0] = jnp.dot(u, wg_ref[...], preferred_element_type=F32)


def _proj_main(u, w_main, w_g):
    b, s, d = u.shape
    tm = TM_TOKEN
    widths = (POOL_WIDTH, 3 * NA_WIDTH, 2 * ML_WIDTH, ML_WIDTH, ML_WIDTH, LANES)
    dtypes = (F32, BF16, F32, BF16, BF16, F32)
    return pl.pallas_call(
        _proj_main_body,
        grid=(b, s // tm),
        in_specs=[pl.BlockSpec((1, tm, d), lambda i, j: (i, j, 0)),
                  pl.BlockSpec(w_main.shape, lambda i, j: (0, 0)),
                  pl.BlockSpec(w_g.shape, lambda i, j: (0, 0))],
        out_specs=[pl.BlockSpec((1, tm, w), lambda i, j: (i, j, 0)) for w in widths],
        out_shape=[jax.ShapeDtypeStruct((b, s, w), dt) for w, dt in zip(widths, dtypes)],
        compiler_params=_cparams(("arbitrary", "arbitrary")),
        name="proj_main",
    )(u, w_main, w_g)


def _pool_segment(a, w_bd, scale):
    n = a.shape[0]
    t = lax.broadcasted_iota(jnp.int32, a.shape, 0)
    grp = lax.broadcasted_iota(jnp.int32, a.shape, 1) // POOL_GROUP_DIM

    def up(x, k):
        return jnp.where(t < n - k, pltpu.roll(x, n - k, 0), 0.0)

    def down(x, k):
        return jnp.where(t >= k, pltpu.roll(x, k, 0), 0.0)

    ahead = a
    behind = down(a, 1)
    wsum = jnp.zeros_like(a)
    half = jnp.zeros_like(t)
    for g, h in enumerate(POOL_HALF_WINDOWS):
        if g > 0:
            ahead = ahead + up(ahead, h // 2)
            behind = behind + down(behind, h // 2)
        wsum = jnp.where(grp == g, ahead + behind, wsum)
        half = jnp.where(grp == g, h, half)
    cnt = jnp.minimum(t + half, n) - jnp.maximum(t - half, 0)
    mean = wsum / cnt.astype(F32)
    out = jnp.dot((mean - a).astype(BF16), w_bd, preferred_element_type=F32)
    return out * scale


def _pool_body(a_ref, w_ref, sc_ref, o_ref, *, n_lat):
    w_bd = w_ref[...]
    scale = sc_ref[...]
    s = a_ref.shape[1]
    o_ref[0, :n_lat] = _pool_segment(a_ref[0, :n_lat], w_bd, scale).astype(o_ref.dtype)
    o_ref[0, n_lat:] = _pool_segment(a_ref[0, n_lat:s], w_bd, scale).astype(o_ref.dtype)


def _pool(pool_in, w_bd, scale, n_lat):
    b, s, w = pool_in.shape
    return pl.pallas_call(
        functools.partial(_pool_body, n_lat=n_lat),
        grid=(b,),
        in_specs=[pl.BlockSpec((1, s, w), lambda i: (i, 0, 0)),
                  pl.BlockSpec((w, w), lambda i: (0, 0)),
                  pl.BlockSpec((1, w), lambda i: (0, 0))],
        out_specs=pl.BlockSpec((1, s, w), lambda i: (i, 0, 0)),
        out_shape=jax.ShapeDtypeStruct((b, s, w), BF16),
        compiler_params=_cparams(("arbitrary",)),
        name="pool",
    )(pool_in, w_bd, scale)


NA_ROWS_PER_STEP = 4


def _na_body(q_ref, k_ref, v_ref, bias_ref, o_ref, *, n_lat, n_rows):
    step = pl.program_id(1)
    s_tot = k_ref.shape[1]
    kc = k_ref[0, n_lat:s_tot, :]
    vc = v_ref[0, n_lat:s_tot, :]
    nt = (((1,), (1,)), ((), ()))
    head = lax.broadcasted_iota(jnp.int32, (GRID_W, NA_WIDTH), 1) // NA_HEAD_DIM
    wr = NA_WIN_ROWS

    def one_row(g, local):
        rows = slice(g * GRID_W, (g + 1) * GRID_W)
        q = q_ref[0, rows, :]
        zero = jnp.zeros_like(q)
        q4 = jnp.concatenate([jnp.where(head == h, q, zero) for h in range(NA_HEADS)], axis=0)
        s_ctx = lax.dot_general(q4, kc, nt, preferred_element_type=F32)
        m = jnp.max(s_ctx, axis=-1, keepdims=True)
        if local:
            r = step * NA_ROWS_PER_STEP + g
            rs = jnp.clip(r - wr // 2, 0, n_rows - wr)
            start = pl.multiple_of(rs * GRID_W, GRID_W)
            kw = k_ref[0, pl.ds(start, wr * GRID_W), :]
            vw = v_ref[0, pl.ds(start, wr * GRID_W), :]
            s_loc = lax.dot_general(q4, kw, nt, preferred_element_type=F32) + bias_ref[rs - r + wr - 1]
            m = jnp.maximum(m, jnp.max(s_loc, axis=-1, keepdims=True))
            p_loc = jnp.exp(s_loc - m)
        p_ctx = jnp.exp(s_ctx - m)
        denom = jnp.sum(p_ctx, axis=-1, keepdims=True)
        o4 = jnp.dot(p_ctx.astype(BF16), vc, preferred_element_type=F32)
        if local:
            denom = denom + jnp.sum(p_loc, axis=-1, keepdims=True)
            o4 = o4 + jnp.dot(p_loc.astype(BF16), vw, preferred_element_type=F32)
        o4 = o4 / denom
        out = jnp.zeros((GRID_W, NA_WIDTH), F32)
        for h in range(NA_HEADS):
            out = out + jnp.where(head == h, o4[h * GRID_W:(h + 1) * GRID_W], 0.0)
        o_ref[0, rows, :] = out.astype(o_ref.dtype)

    @pl.when(step * NA_ROWS_PER_STEP < n_rows)
    def _():
        for g in range(NA_ROWS_PER_STEP):
            one_row(g, True)

    @pl.when(step * NA_ROWS_PER_STEP >= n_rows)
    def _():
        for g in range(NA_ROWS_PER_STEP):
            one_row(g, False)


def _na_bias_classes(rpb):
    col = np.arange(GRID_W)
    col_start = np.clip(col - NA_WIN_COLS // 2, 0, GRID_W - NA_WIN_COLS)
    in_window = (col[None, :] >= col_start[:, None]) & (col[None, :] < col_start[:, None] + NA_WIN_COLS)
    col_off = np.clip(col[None, :] - col[:, None], -(NA_WIN_COLS - 1), NA_WIN_COLS - 1) + NA_WIN_COLS - 1
    n_off = 2 * NA_WIN_COLS - 1
    pick = (col_off[None, :, :] == np.arange(n_off)[:, None, None]).astype(np.float32)
    by_col = jnp.einsum('hro,oqk->hrqk', rpb.astype(F32), jnp.asarray(pick), precision=HIGHEST)
    by_col = jnp.where(in_window[None, None], by_col, -jnp.inf)
    b = jnp.stack([by_col[:, o:o + NA_WIN_ROWS] for o in range(NA_WIN_ROWS)], axis=0)
    b = b.transpose(0, 1, 3, 2, 4)
    return b.reshape(NA_WIN_ROWS, NA_HEADS * GRID_W, NA_WIN_ROWS * GRID_W)


def _na(na_qkv, bias_cls, n_lat):
    b, s, _ = na_qkv.shape
    n_rows = n_lat // GRID_W
    tq = NA_ROWS_PER_STEP * GRID_W

    return pl.pallas_call(
        functools.partial(_na_body, n_lat=n_lat, n_rows=n_rows),
        grid=(b, s // tq),
        in_specs=[pl.BlockSpec((1, tq, NA_WIDTH), lambda i, r: (i, r, 0)),
                  pl.BlockSpec((1, s, NA_WIDTH), lambda i, r: (i, 0, 1)),
                  pl.BlockSpec((1, s, NA_WIDTH), lambda i, r: (i, 0, 2)),
                  pl.BlockSpec(bias_cls.shape, lambda i, r: (0, 0, 0))],
        out_specs=pl.BlockSpec((1, tq, NA_WIDTH), lambda i, r: (i, r, 0)),
        out_shape=jax.ShapeDtypeStruct((b, s, NA_WIDTH), BF16),
        compiler_params=_cparams(("arbitrary", "arbitrary")),
        name="nbr_attention",
    )(na_qkv, na_qkv, na_qkv, bias_cls)


def _rope_tables(n_lat):
    half = ML_HEAD_DIM // 2
    nf = half // 2
    inv = ROPE_THETA ** (-np.arange(nf, dtype=np.float64) / nf)
    pos = np.arange(n_lat)
    ang = np.concatenate([(pos // GRID_W)[:, None] * inv[None, :]] * 2
                         + [(pos % GRID_W)[:, None] * inv[None, :]] * 2, axis=1)
    lane = np.arange(ML_HEAD_DIM)
    second = (lane % half) >= nf
    cos = np.cos(ang)
    sin = np.sin(ang)
    sin_from_lo = np.where(second[None, :], sin, 0.0)
    sin_from_hi = np.where(second[None, :], 0.0, -sin)
    return (jnp.asarray(cos, F32), jnp.asarray(sin_from_lo, F32), jnp.asarray(sin_from_hi, F32))


def _conv_silu(x, w):
    n = x.shape[0]
    t = lax.broadcasted_iota(jnp.int32, x.shape, 0)
    prev = jnp.where(t >= 1, pltpu.roll(x, 1, 0), 0.0)
    nxt = jnp.where(t < n - 1, pltpu.roll(x, n - 1, 0), 0.0)
    return _silu(w[0:1] * prev + w[1:2] * x + w[2:3] * nxt)


def _rope(x, cos, sin_lo, sin_hi):
    nf = ML_HEAD_DIM // 4
    return x * cos + pltpu.roll(x, nf, 1) * sin_lo + pltpu.roll(x, ML_HEAD_DIM - nf, 1) * sin_hi


def _ml_prep_body(xq_ref, xk_ref, v_ref, wq_ref, wk_ref, cos_ref, slo_ref, shi_ref, k_ref, qt_ref, vt_ref, *, n_lat):
    s = xq_ref.shape[1]
    wq = wq_ref[...]
    wk = wk_ref[...]
    cos, slo, shi = cos_ref[...], slo_ref[...], shi_ref[...]
    kscale = ML_HEAD_DIM ** -0.5
    for lo, hi, rotary in ((0, n_lat, True), (n_lat, s, False)):
        q = _conv_silu(xq_ref[0, lo:hi], wq)
        k = _conv_silu(xk_ref[0, lo:hi], wk) * kscale
        if rotary:
            q = _rope(q, cos, slo, shi)
            k = _rope(k, cos, slo, shi)
        k_ref[0, lo:hi] = k.astype(k_ref.dtype)
        v = v_ref[0, lo:hi].astype(F32)
        for j in range((hi - lo) // ML_CHUNK):
            rows = slice(j * ML_CHUNK, (j + 1) * ML_CHUNK)
            qt_ref[0, 0, lo // ML_CHUNK + j] = q[rows].T.astype(qt_ref.dtype)
            vt_ref[0, 0, lo // ML_CHUNK + j] = v[rows].T.astype(vt_ref.dtype)


def _ml_prep(mlqk, mlv, conv_w, rope, n_lat):
    b, s, _ = mlqk.shape
    hd = ML_HEAD_DIM
    nc = s // ML_CHUNK
    tab = pl.BlockSpec((n_lat, hd), lambda i, h: (0, 0))
    chunks = pl.BlockSpec((1, 1, nc, hd, ML_CHUNK), lambda i, h: (i, h, 0, 0, 0))
    chunk_shape = jax.ShapeDtypeStruct((b, ML_HEADS, nc, hd, ML_CHUNK), BF16)
    return pl.pallas_call(
        functools.partial(_ml_prep_body, n_lat=n_lat),
        grid=(b, ML_HEADS),
        in_specs=[pl.BlockSpec((1, s, hd), lambda i, h: (i, 0, h)),
                  pl.BlockSpec((1, s, hd), lambda i, h: (i, 0, ML_HEADS + h)),
                  pl.BlockSpec((1, s, hd), lambda i, h: (i, 0, h)),
                  pl.BlockSpec((3, hd), lambda i, h: (0, h)),
                  pl.BlockSpec((3, hd), lambda i, h: (0, ML_HEADS + h)),
                  tab, tab, tab],
        out_specs=[pl.BlockSpec((1, s, hd), lambda i, h: (i, 0, h)), chunks, chunks],
        out_shape=[jax.ShapeDtypeStruct((b, s, ML_WIDTH), BF16), chunk_shape, chunk_shape],
        compiler_params=_cparams(("arbitrary", "arbitrary")),
        name="mlstm_prep",
    )(mlqk, mlqk, mlv, conv_w, conv_w, *rope)


def _log_sigmoid(x):
    return jnp.minimum(x, 0.0) - jnp.log1p(jnp.exp(-jnp.abs(x)))


def _ml_gates_body(gc_ref, gr_ref, bc_ref, br_ref, oc_ref, or_ref):
    nc = gr_ref.shape[1]
    c = ML_CHUNK
    ii = lax.broadcasted_iota(jnp.int32, (c, c), 0)
    jj = lax.broadcasted_iota(jnp.int32, (c, c), 1)
    lower = (ii >= jj).astype(F32)
    upper = (ii <= jj).astype(F32)
    lane_kind = lax.broadcasted_iota(jnp.int32, (c, LANES), 1) // ML_HEADS
    row_kind = lax.broadcasted_iota(jnp.int32, (4 * ML_HEADS, c), 0) // ML_HEADS
    for j in range(nc):
        g = gc_ref[0, j * c:(j + 1) * c, :] + bc_ref[...]
        lf = _log_sigmoid(g)
        pre = jnp.dot(lower, lf, precision=HIGHEST, preferred_element_type=F32)
        suf = jnp.dot(upper, lf, precision=HIGHEST, preferred_element_type=F32)
        oc_ref[0, j * c:(j + 1) * c, :] = jnp.where(lane_kind == 1, pre, jnp.where(lane_kind == 3, suf, g))
        gr = gr_ref[0, j] + br_ref[...]
        lfr = _log_sigmoid(gr)
        pre_r = jnp.dot(lfr, upper, precision=HIGHEST, preferred_element_type=F32)
        suf_r = jnp.dot(lfr, lower, precision=HIGHEST, preferred_element_type=F32)
        or_ref[0, j] = jnp.where(row_kind == 1, pre_r, jnp.where(row_kind == 3, suf_r, gr))


def _ml_gates(mlg, gate_bias):
    b, s, _ = mlg.shape
    nc = s // ML_CHUNK
    ng = 4 * ML_HEADS
    g_rows = mlg[:, :, :ng].reshape(b, nc, ML_CHUNK, ng).transpose(0, 1, 3, 2)
    bias_c = jnp.zeros((1, LANES), F32).at[0, :ng].set(gate_bias)
    bias_r = gate_bias.reshape(ng, 1)
    return pl.pallas_call(
        _ml_gates_body,
        grid=(b,),
        in_specs=[pl.BlockSpec((1, s, LANES), lambda i: (i, 0, 0)),
                  pl.BlockSpec((1, nc, ng, ML_CHUNK), lambda i: (i, 0, 0, 0)),
                  pl.BlockSpec((1, LANES), lambda i: (0, 0)),
                  pl.BlockSpec((ng, 1), lambda i: (0, 0))],
        out_specs=[pl.BlockSpec((1, s, LANES), lambda i: (i, 0, 0)),
                   pl.BlockSpec((1, nc, ng, ML_CHUNK), lambda i: (i, 0, 0, 0))],
        out_shape=[jax.ShapeDtypeStruct((b, s, LANES), F32),
                   jax.ShapeDtypeStruct((b, nc, ng, ML_CHUNK), F32)],
        compiler_params=_cparams(("arbitrary",)),
        name="mlstm_gates",
    )(mlg, g_rows, bias_c, bias_r)


def _ml_step(k, qt, vt_aug, a_c, b_r, state_t, m, *, forward):
    c = ML_CHUNK
    d = ML_HEAD_DIM
    si = lax.broadcasted_iota(jnp.int32, (c, c), 0)
    ti = lax.broadcasted_iota(jnp.int32, (c, c), 1)
    allowed = (si <= ti) if forward else (si >= ti)
    log_d = jnp.where(allowed, b_r + a_c, -jnp.inf)
    inter = b_r + m
    m_t = jnp.maximum(jnp.max(log_d, axis=0, keepdims=True), inter)
    kq = jnp.dot(k, qt, preferred_element_type=F32)
    s_t = (kq * jnp.exp(log_d - m_t)).astype(BF16)
    w_inter = jnp.exp(inter - m_t)
    tot = (jnp.dot(vt_aug, s_t, preferred_element_type=F32)
           + w_inter * jnp.dot(state_t.astype(BF16), qt, preferred_element_type=F32))
    num = tot[:d]
    den = tot[d:d + 1]
    h_t = num / jnp.maximum(jnp.abs(den), jnp.exp(-m_t))
    b_end = b_r[:, c - 1:c] if forward else b_r[:, 0:1]
    log_w = b_end + a_c
    m_new = jnp.maximum(b_end + m, jnp.max(log_w, axis=0, keepdims=True))
    w_c = jnp.exp(log_w - m_new)
    decay = jnp.exp(b_end + m - m_new)
    kw = (k.astype(F32) * w_c).astype(BF16)
    state_new = decay * state_t + jnp.dot(vt_aug, kw, preferred_element_type=F32)
    return h_t, state_new, m_new


ML_HEADS_PER_STEP = 2


def _ml_scan_body(k_ref, qt_ref, vt_ref, gc_ref, gr_ref, o_ref, nw_ref, out_ref, hsum_ref, *, n_lat):
    c = ML_CHUNK
    d = ML_HEAD_DIM
    s = k_ref.shape[1]
    heads = qt_ref.shape[1]
    nc_lat = n_lat // c
    nc_all = s // c
    ones_row = (lax.broadcasted_iota(jnp.int32, (d, c), 0) == 0).astype(BF16)

    def rows(j):
        return pl.ds(j * c if isinstance(j, int) else pl.multiple_of(j * c, c), c)

    def run(j, h, state, m, forward):
        vt_aug = jnp.concatenate([vt_ref[0, h, j], ones_row], axis=0)
        gc = gc_ref[0, h, rows(j), :]
        gr = gr_ref[0, h, j]
        k0 = 0 if forward else 2
        a_c = gc[:, k0:k0 + 1] - gc[:, k0 + 1:k0 + 2]
        return _ml_step(k_ref[0, rows(j), h * d:(h + 1) * d], qt_ref[0, h, j], vt_aug, a_c, gr[k0 + 1:k0 + 2],
                        state, m, forward=forward)

    def finalize(j, h, hsum_t):
        cols = slice(h * d, (h + 1) * d)
        hc = hsum_t - jnp.mean(hsum_t, axis=0, keepdims=True)
        hn = (hc * lax.rsqrt(jnp.mean(hc * hc, axis=0, keepdims=True) + LN_EPS)).T
        gate = _sigmoid(o_ref[0, rows(j), cols].astype(F32))
        out_ref[0, rows(j), cols] = (gate * hn * nw_ref[:, cols]).astype(out_ref.dtype)

    zero_state = jnp.zeros((2 * d, d), F32)
    m0 = jnp.full((1, 1), ML_M_INIT, F32)
    carry = []
    for h in range(heads):
        sf, mf = zero_state, m0
        hc_f = []
        for j in range(nc_lat, nc_all):
            out, sf, mf = run(j, h, sf, mf, True)
            hc_f.append(out)
        sb, mb = zero_state, m0
        hc_b = {}
        for j in range(nc_all - 1, nc_lat - 1, -1):
            out, sb, mb = run(j, h, sb, mb, False)
            hc_b[j] = out
        for idx, j in enumerate(range(nc_lat, nc_all)):
            finalize(j, h, hc_f[idx] + hc_b[j])
        carry += [sf, mf, sb, mb]

    half = nc_lat // 2

    def first_half(i, carry):
        jb = nc_lat - 1 - i
        new = []
        for h in range(heads):
            sf, mf, sb, mb = carry[4 * h:4 * h + 4]
            hf, sf, mf = run(i, h, sf, mf, True)
            hb, sb, mb = run(jb, h, sb, mb, False)
            hsum_ref[h, i] = hf
            hsum_ref[h, jb] = hb
            new += [sf, mf, sb, mb]
        return tuple(new)

    carry = lax.fori_loop(0, half, first_half, tuple(carry))

    def second_half(i, carry):
        jb = nc_lat - 1 - i
        new = []
        for h in range(heads):
            sf, mf, sb, mb = carry[4 * h:4 * h + 4]
            hf, sf, mf = run(i, h, sf, mf, True)
            hb, sb, mb = run(jb, h, sb, mb, False)
            finalize(i, h, hf + hsum_ref[h, i])
            finalize(jb, h, hb + hsum_ref[h, jb])
            new += [sf, mf, sb, mb]
        return tuple(new)

    lax.fori_loop(half, nc_lat, second_half, carry)


def _ml_scan(k, qt, vt, gcol, grow, mlo, norm_w, n_lat):
    b, s, _ = k.shape
    hd = ML_HEAD_DIM
    hb = ML_HEADS_PER_STEP
    nc = s // ML_CHUNK
    ng = 4 * ML_HEADS
    gc_h = gcol[:, :, :ng].reshape(b, s, 4, ML_HEADS).transpose(0, 3, 1, 2)
    gr_h = grow.reshape(b, nc, 4, ML_HEADS, ML_CHUNK).transpose(0, 3, 1, 2, 4)
    tok = pl.BlockSpec((1, s, hb * hd), lambda i, h: (i, 0, h))
    chunks = pl.BlockSpec((1, hb, nc, hd, ML_CHUNK), lambda i, h: (i, h, 0, 0, 0))
    return pl.pallas_call(
        functools.partial(_ml_scan_body, n_lat=n_lat),
        grid=(b, ML_HEADS // hb),
        in_specs=[tok, chunks, chunks,
                  pl.BlockSpec((1, hb, s, 4), lambda i, h: (i, h, 0, 0)),
                  pl.BlockSpec((1, hb, nc, 4, ML_CHUNK), lambda i, h: (i, h, 0, 0, 0)),
                  tok,
                  pl.BlockSpec((1, hb * hd), lambda i, h: (0, h))],
        out_specs=tok,
        out_shape=jax.ShapeDtypeStruct((b, s, ML_WIDTH), BF16),
        scratch_shapes=[pltpu.VMEM((hb, n_lat // ML_CHUNK, hd, ML_CHUNK), F32)],
        compiler_params=_cparams(("arbitrary", "arbitrary")),
        name="mlstm_scan",
    )(k, qt, vt, gc_h, gr_h, mlo, norm_w)


def _top2(logits, n_valid):
    lane = lax.broadcasted_iota(jnp.int32, logits.shape, 1).astype(F32)
    neg = jnp.float32(-jnp.inf)
    l = jnp.where(lane < n_valid, logits, neg)
    m1 = jnp.max(l, axis=-1, keepdims=True)
    e1 = jnp.min(jnp.where(l == m1, lane, float(LANES)), axis=-1, keepdims=True)
    l2 = jnp.where(lane == e1, neg, l)
    m2 = jnp.max(l2, axis=-1, keepdims=True)
    e2 = jnp.min(jnp.where(l2 == m2, lane, float(LANES)), axis=-1, keepdims=True)
    x2 = jnp.exp(m2 - m1)
    p1 = 1.0 / (1.0 + x2)
    p2 = x2 / (1.0 + x2)
    return jnp.where(lane == 0, e1, jnp.where(lane == 1, e2, jnp.where(lane == 2, p1, jnp.where(lane == 3, p2, 0.0))))


def _route_ranks(route, valid, cnt_ref):
    tm = route.shape[0]
    lane = lax.broadcasted_iota(jnp.int32, route.shape, 1).astype(F32)
    pick1 = jnp.where(valid, (lane == route[:, 0:1]).astype(F32), 0.0)
    pick2 = jnp.where(valid, (lane == route[:, 1:2]).astype(F32), 0.0)
    both = pick1 + pick2
    before = (lax.broadcasted_iota(jnp.int32, (tm, tm), 0) > lax.broadcasted_iota(jnp.int32, (tm, tm), 1))
    prefix = jnp.dot(before.astype(BF16), both.astype(BF16), preferred_element_type=F32) + cnt_ref[...]
    rank1 = jnp.sum(prefix * pick1, axis=-1, keepdims=True)
    rank2 = jnp.sum(prefix * pick2, axis=-1, keepdims=True)
    cnt_ref[...] += jnp.sum(both, axis=0, keepdims=True)
    return jnp.where(lane == 4, rank1, jnp.where(lane == 5, rank2, route))


def _merge_body(*refs, n_lat, alpha, moe, n_rows_tok):
    (a_ref, n_ref, m_ref, u_ref, x_ref, wbp_ref, wbn_ref, wbm_ref, wg_ref, wo_ref,
     g1l_ref, g1c_ref, shl_ref, shc_ref, scl_ref, scc_ref, lng_ref, lnb_ref) = refs[:18]
    if moe:
        rw_ref, rb_ref, x_out, tok_out, route_out, cnt_out, cnt_ref = refs[18:]
    else:
        x_out, tok_out = refs[18:]
    tm = x_ref.shape[1]
    d = x_ref.shape[2]
    is_ctx = _is_ctx_rows(pl.program_id(1), tm, n_lat)
    u = u_ref[0]
    y = jnp.zeros((tm, d), F32)
    for k, (br_ref, w_ref) in enumerate(((a_ref, wbp_ref), (n_ref, wbn_ref), (m_ref, wbm_ref))):
        gate = _sigmoid(jnp.dot(u, wg_ref[:, k * d:(k + 1) * d], preferred_element_type=F32))
        y = y + gate * jnp.dot(br_ref[0], w_ref[...], preferred_element_type=F32)
    y = jnp.dot(y.astype(BF16), wo_ref[...], preferred_element_type=F32)
    z = alpha * x_ref[0] + _tile_mod(g1l_ref, g1c_ref, is_ctx) * y
    xn = _ln_plain(z) * lng_ref[...] + lnb_ref[...]
    x_out[0] = xn
    tok = _ln_plain(xn) * (1.0 + _tile_mod(scl_ref, scc_ref, is_ctx)) + _tile_mod(shl_ref, shc_ref, is_ctx)
    tok_out[0] = tok.astype(tok_out.dtype)
    if moe:
        @pl.when((pl.program_id(0) == 0) & (pl.program_id(1) == 0))
        def _():
            cnt_ref[...] = jnp.zeros_like(cnt_ref)

        lane = lax.broadcasted_iota(jnp.int32, (tm, LANES), 1)
        logits = jnp.zeros((tm, LANES), F32)
        for e in range(N_EXPERTS):
            col = jnp.sum(tok * rw_ref[e:e + 1, :], axis=-1, keepdims=True)
            logits = jnp.where(lane == e, col, logits)
        valid = (pl.program_id(1) * tm + lax.broadcasted_iota(jnp.int32, (tm, 1), 0)) < n_rows_tok
        route_out[0] = _route_ranks(_top2(logits + rb_ref[...], N_EXPERTS), valid, cnt_ref)
        cnt_out[...] = cnt_ref[...]


def _merge(a, n, m, u, x_all, w_bp, w_bn, w_bm, w_gate, w_o, g1, sh2, sc2, ln_g, ln_b, n_lat, alpha,
           router=None, n_rows_tok=0):
    b, s, d = x_all.shape
    tm = TM_MERGE
    moe = router is not None

    def tok(w):
        return pl.BlockSpec((1, tm, w), lambda i, j: (i, j, 0))

    def full(arr):
        return pl.BlockSpec(arr.shape, lambda i, j: (0,) * arr.ndim, pipeline_mode=pl.Buffered(1))

    in_specs = ([tok(a.shape[2]), tok(n.shape[2]), tok(m.shape[2]), tok(d), tok(d),
                 full(w_bp), full(w_bn), full(w_bm), full(w_gate), full(w_o)]
                + _row_specs(d, b) * 3 + [full(ln_g), full(ln_b)])
    args = [a, n, m, u, x_all, w_bp, w_bn, w_bm, w_gate, w_o, g1, g1, sh2, sh2, sc2, sc2, ln_g, ln_b]
    out_specs = [tok(d), tok(d)]
    out_shape = [jax.ShapeDtypeStruct((b, s, d), F32), jax.ShapeDtypeStruct((b, s, d), F32 if moe else BF16)]
    scratch = []
    if moe:
        in_specs += [full(router[0]), full(router[1])]
        args += list(router)
        out_specs += [tok(LANES), pl.BlockSpec((1, LANES), lambda i, j: (0, 0))]
        out_shape += [jax.ShapeDtypeStruct((b, s, LANES), F32), jax.ShapeDtypeStruct((1, LANES), F32)]
        scratch = [pltpu.VMEM((1, LANES), F32)]
    return pl.pallas_call(
        functools.partial(_merge_body, n_lat=n_lat, alpha=alpha, moe=moe, n_rows_tok=n_rows_tok),
        grid=(b, s // tm),
        in_specs=in_specs,
        out_specs=out_specs,
        out_shape=out_shape,
        scratch_shapes=scratch,
        compiler_params=_cparams(("arbitrary", "arbitrary")),
        name="merge",
    )(*args)


def _finish_rows(f, x, g2, lng, lnb, alpha):
    return _ln_plain(alpha * x + g2 * f) * lng + lnb


def _ffn_body(*refs, n_lat, alpha, with_next):
    (t_ref, x_ref, w1_ref, w3_ref, w2_ref, g2l_ref, g2c_ref, lng_ref, lnb_ref) = refs[:9]
    if with_next:
        shl_ref, shc_ref, scl_ref, scc_ref, x_out, u_out, acc_ref = refs[9:]
    else:
        x_out, acc_ref = refs[9:]
    f = pl.program_id(2)
    t = t_ref[0]
    h1 = jnp.dot(t, w1_ref[...], preferred_element_type=F32)
    h3 = jnp.dot(t, w3_ref[...], preferred_element_type=F32)
    contrib = jnp.dot((_silu(h1) * h3).astype(BF16), w2_ref[...], preferred_element_type=F32)

    @pl.when(f == 0)
    def _():
        acc_ref[...] = contrib

    @pl.when(f > 0)
    def _():
        acc_ref[...] += contrib

    @pl.when(f == pl.num_programs(2) - 1)
    def _():
        tm = x_ref.shape[1]
        is_ctx = _is_ctx_rows(pl.program_id(1), tm, n_lat)
        xn = _finish_rows(acc_ref[...], x_ref[0], _tile_mod(g2l_ref, g2c_ref, is_ctx), lng_ref[...], lnb_ref[...], alpha)
        x_out[0] = xn
        if with_next:
            u = _ln_plain(xn) * (1.0 + _tile_mod(scl_ref, scc_ref, is_ctx)) + _tile_mod(shl_ref, shc_ref, is_ctx)
            u_out[0] = u.astype(u_out.dtype)


def _ffn(tok, x_all, w1, w3, w2, g2, ln_g, ln_b, n_lat, alpha, nxt=None):
    b, s, d = x_all.shape
    dff = w1.shape[1]
    tm, tf = TM_TOKEN, TF_FFN
    with_next = nxt is not None

    def tokspec():
        return pl.BlockSpec((1, tm, d), lambda i, j, f: (i, j, 0))

    def rows():
        return [pl.BlockSpec((1, 1, d), lambda i, j, f: (i, 0, 0)),
                pl.BlockSpec((1, 1, d), lambda i, j, f: (b, 0, 0))]

    vec = pl.BlockSpec((1, d), lambda i, j, f: (0, 0))
    in_specs = [tokspec(), tokspec(),
                pl.BlockSpec((d, tf), lambda i, j, f: (0, f)),
                pl.BlockSpec((d, tf), lambda i, j, f: (0, f)),
                pl.BlockSpec((tf, d), lambda i, j, f: (f, 0))] + rows() + [vec, vec]
    args = [tok, x_all, w1, w3, w2, g2, g2, ln_g, ln_b]
    out_specs = [tokspec()]
    out_shape = [jax.ShapeDtypeStruct((b, s, d), F32)]
    if with_next:
        in_specs += rows() + rows()
        args += [nxt[0], nxt[0], nxt[1], nxt[1]]
        out_specs.append(tokspec())
        out_shape.append(jax.ShapeDtypeStruct((b, s, d), BF16))
    res = pl.pallas_call(
        functools.partial(_ffn_body, n_lat=n_lat, alpha=alpha, with_next=with_next),
        grid=(b, s // tm, dff // tf),
        in_specs=in_specs,
        out_specs=out_specs,
        out_shape=out_shape,
        scratch_shapes=[pltpu.VMEM((tm, d), F32)],
        compiler_params=_cparams(("arbitrary", "arbitrary", "arbitrary")),
        name="ffn_dense",
    )(*args)
    return res if with_next else (res[0], None)


def _moe_tables(route, counts, n_rows_tok):
    r = MOE_ROWS
    tm = TM_MOE
    b = route.shape[0]
    cnt = counts[0, :N_EXPERTS].astype(jnp.int32)
    nblk = (cnt + r - 1) // r
    bend = jnp.cumsum(nblk)
    bstart = bend - nblk
    total = bend[-1]
    n_blocks = -(-(b * n_rows_tok * TOP_K) // r) + N_EXPERTS
    part = route[:, :n_rows_tok]
    experts = jnp.arange(N_EXPERTS, dtype=F32)
    seg_row0 = jnp.sum((part[:, :, 0:TOP_K, None] == experts).astype(jnp.int32) * (bstart * r), axis=-1)
    dest = seg_row0 + part[:, :, 4:4 + TOP_K].astype(jnp.int32)
    nt = n_rows_tok // tm
    dest_tiles = dest.reshape(b, nt, tm, TOP_K).transpose(0, 1, 3, 2).reshape(b * nt, 1, TOP_K * tm)
    blk = jnp.arange(n_blocks, dtype=jnp.int32)
    e_of = jnp.sum((blk[:, None] >= bend[None, :]).astype(jnp.int32), axis=1)
    e_last = jnp.sum((total - 1 >= bend).astype(jnp.int32))
    active = (blk < total).astype(jnp.int32)
    block_e = jnp.minimum(e_of, e_last)
    spare = total + jnp.arange(N_EXPERTS, dtype=jnp.int32)
    zero_blk = jnp.concatenate([bend - 1, spare])
    zero_on = jnp.concatenate([nblk > 0, spare < n_blocks]).astype(jnp.int32)
    zero_blk = jnp.clip(zero_blk, 0, n_blocks - 1)
    return dest_tiles, zero_blk, zero_on, block_e, active


def _dispatch_body(zblk_ref, zon_ref, dest_ref, tok_ref, xs_hbm, zbuf, zsem, dsem):
    i = pl.program_id(0)
    tm = tok_ref.shape[1]
    r = zbuf.shape[0]

    def zero_copy(z):
        start = pl.multiple_of(zblk_ref[z] * r, r)
        return pltpu.make_async_copy(zbuf, xs_hbm.at[pl.ds(start, r)], zsem)

    @pl.when(i == 0)
    def _():
        zbuf[...] = jnp.zeros_like(zbuf)
        for z in range(2 * N_EXPERTS):
            @pl.when(zon_ref[z] > 0)
            def _():
                zero_copy(z).start()
        for z in range(2 * N_EXPERTS):
            @pl.when(zon_ref[z] > 0)
            def _():
                zero_copy(z).wait()

    def row_copy(j, k):
        return pltpu.make_async_copy(tok_ref.at[0, pl.ds(j, 1)],
                                     xs_hbm.at[pl.ds(dest_ref[0, 0, k * tm + j], 1)], dsem)

    def issue(j, c):
        for k in range(TOP_K):
            row_copy(j, k).start()
        return c

    lax.fori_loop(0, tm, issue, 0, unroll=8)
    for _ in range(TOP_K):
        pltpu.make_async_copy(tok_ref.at[0], xs_hbm.at[pl.ds(0, tm)], dsem).wait()


def _moe_dispatch(tok, zero_blk, zero_on, dest_tiles, n_blocks, n_rows_tok):
    d = tok.shape[2]
    tm = TM_MOE
    nt = n_rows_tok // tm
    grid_spec = pltpu.PrefetchScalarGridSpec(
        num_scalar_prefetch=2,
        grid=(dest_tiles.shape[0],),
        in_specs=[pl.BlockSpec((1, 1, TOP_K * tm), lambda i, zb, zo: (i, 0, 0), memory_space=pltpu.SMEM),
                  pl.BlockSpec((1, tm, d), lambda i, zb, zo: (i // nt, i % nt, 0))],
        out_specs=pl.BlockSpec(memory_space=pl.ANY),
        scratch_shapes=[pltpu.VMEM((MOE_ROWS, d), F32),
                        pltpu.SemaphoreType.DMA(()), pltpu.SemaphoreType.DMA(())],
    )
    return pl.pallas_call(
        _dispatch_body,
        grid_spec=grid_spec,
        out_shape=jax.ShapeDtypeStruct((n_blocks * MOE_ROWS, d), F32),
        compiler_params=_cparams(("arbitrary",)),
        name="moe_dispatch",
    )(zero_blk, zero_on, dest_tiles, tok)


def _experts_body(be_ref, act_ref, x_ref, w1_ref, w3_ref, w2_ref, y_ref, xb_ref):
    i = pl.program_id(0)
    f = pl.program_id(1)

    @pl.when((act_ref[i] == 0) & (f == 0))
    def _():
        y_ref[...] = jnp.zeros_like(y_ref)

    @pl.when(act_ref[i] > 0)
    def _():
        @pl.when(f == 0)
        def _():
            xb_ref[...] = x_ref[...].astype(BF16)

        x = xb_ref[...]
        h1 = jnp.dot(x, w1_ref[0, 0].astype(BF16), preferred_element_type=F32)
        h3 = jnp.dot(x, w3_ref[0, 0].astype(BF16), preferred_element_type=F32)
        contrib = jnp.dot((_silu(h1) * h3).astype(BF16), w2_ref[0, 0].astype(BF16), preferred_element_type=F32)

        @pl.when(f == 0)
        def _():
            y_ref[...] = contrib

        @pl.when(f > 0)
        def _():
            y_ref[...] += contrib


def _moe_experts(xs, block_e, active, w1, w3, w2, li):
    d = xs.shape[1]
    dff = w1.shape[3]
    r, tf = MOE_ROWS, TF_MOE
    nf = dff // tf
    n_blocks = block_e.shape[0]

    def f_eff(i, f, act):
        return jnp.where(act[i] > 0, f, nf - 1)

    grid_spec = pltpu.PrefetchScalarGridSpec(
        num_scalar_prefetch=2,
        grid=(n_blocks, nf),
        in_specs=[pl.BlockSpec((r, d), lambda i, f, be, act: (i, 0)),
                  pl.BlockSpec((1, 1, d, tf), lambda i, f, be, act: (li, be[i], 0, f_eff(i, f, act))),
                  pl.BlockSpec((1, 1, d, tf), lambda i, f, be, act: (li, be[i], 0, f_eff(i, f, act))),
                  pl.BlockSpec((1, 1, tf, d), lambda i, f, be, act: (li, be[i], f_eff(i, f, act), 0))],
        out_specs=pl.BlockSpec((r, d), lambda i, f, be, act: (i, 0)),
        scratch_shapes=[pltpu.VMEM((r, d), BF16)],
    )
    return pl.pallas_call(
        _experts_body,
        grid_spec=grid_spec,
        out_shape=jax.ShapeDtypeStruct(xs.shape, F32),
        compiler_params=_cparams(("arbitrary", "arbitrary")),
        name="moe_experts",
    )(block_e, active, xs, w1, w3, w2)


def _combine_body(*refs, n_lat, alpha, with_next):
    (dest_ref, dnext_ref, ys_hbm, p_ref, x_ref, g2l_ref, g2c_ref, lng_ref, lnb_ref) = refs[:9]
    if with_next:
        shl_ref, shc_ref, scl_ref, scc_ref, x_out, u_out, ybuf, sem = refs[9:]
    else:
        x_out, ybuf, sem = refs[9:]
    tm = x_ref.shape[1]
    t = pl.program_id(0) * pl.num_programs(1) + pl.program_id(1)
    n_tiles = pl.num_programs(0) * pl.num_programs(1)
    slot = t % 2

    def gather_tile(table_ref, to_slot):
        def issue(j, c):
            for k in range(TOP_K):
                pltpu.make_async_copy(ys_hbm.at[pl.ds(table_ref[0, 0, k * tm + j], 1)],
                                      ybuf.at[to_slot, k, pl.ds(j, 1)], sem.at[to_slot]).start()
            return c

        lax.fori_loop(0, tm, issue, 0, unroll=8)

    @pl.when(t == 0)
    def _():
        gather_tile(dest_ref, 0)

    @pl.when(t + 1 < n_tiles)
    def _():
        gather_tile(dnext_ref, 1 - slot)

    for k in range(TOP_K):
        pltpu.make_async_copy(ys_hbm.at[pl.ds(0, tm)], ybuf.at[slot, k], sem.at[slot]).wait()

    is_ctx = _is_ctx_rows(pl.program_id(1), tm, n_lat)
    p = p_ref[0]
    f = p[:, 2:3] * ybuf[slot, 0] + p[:, 3:4] * ybuf[slot, 1]
    xn = _finish_rows(f, x_ref[0], _tile_mod(g2l_ref, g2c_ref, is_ctx), lng_ref[...], lnb_ref[...], alpha)
    x_out[0] = xn
    if with_next:
        u = _ln_plain(xn) * (1.0 + _tile_mod(scl_ref, scc_ref, is_ctx)) + _tile_mod(shl_ref, shc_ref, is_ctx)
        u_out[0] = u.astype(u_out.dtype)


def _combine(ys, dest_tiles, route, x_all, g2, ln_g, ln_b, n_lat, alpha, n_tok_rows, nxt=None):
    b, s, d = x_all.shape
    tm = TM_MOE
    nt = n_tok_rows // tm
    with_next = nxt is not None

    def tok(w):
        return pl.BlockSpec((1, tm, w), lambda i, j: (i, j, 0))

    vec = pl.BlockSpec((1, d), lambda i, j: (0, 0))
    last_tile = b * nt - 1
    in_specs = [pl.BlockSpec((1, 1, TOP_K * tm), lambda i, j: (i * nt + j, 0, 0), memory_space=pltpu.SMEM),
                pl.BlockSpec((1, 1, TOP_K * tm), lambda i, j: (jnp.minimum(i * nt + j + 1, last_tile), 0, 0),
                             memory_space=pltpu.SMEM),
                pl.BlockSpec(memory_space=pl.ANY), tok(LANES), tok(d)] + _row_specs(d, b) + [vec, vec]
    args = [dest_tiles, dest_tiles, ys, route, x_all, g2, g2, ln_g, ln_b]
    out_specs = [tok(d)]
    out_shape = [jax.ShapeDtypeStruct((b, n_tok_rows, d), F32)]
    if with_next:
        in_specs += _row_specs(d, b) * 2
        args += [nxt[0], nxt[0], nxt[1], nxt[1]]
        out_specs.append(tok(d))
        out_shape.append(jax.ShapeDtypeStruct((b, n_tok_rows, d), BF16))
    res = pl.pallas_call(
        functools.partial(_combine_body, n_lat=n_lat, alpha=alpha, with_next=with_next),
        grid=(b, nt),
        in_specs=in_specs,
        out_specs=out_specs,
        out_shape=out_shape,
        scratch_shapes=[pltpu.VMEM((2, TOP_K, tm, d), F32), pltpu.SemaphoreType.DMA((2,))],
        compiler_params=_cparams(("arbitrary", "arbitrary")),
        name="moe_combine",
    )(*args)
    return res if with_next else (res[0], None)


def _block_diag(pool_w):
    g, c, _ = pool_w.shape
    out = jnp.zeros((g * c, g * c), pool_w.dtype)
    for i in range(g):
        out = out.at[i * c:(i + 1) * c, i * c:(i + 1) * c].set(pool_w[i])
    return out


def kernel(x, c, ctx, c_ctx, w_mod, b_mod, w_in, ml_gate_bias, ml_conv_w, ml_norm_w, pool_w, pool_scale,
           na_rpb, w_branch_pool, w_branch_na, w_branch_ml, w_out, ln1_g, ln1_b, ln2_g, ln2_b,
           ffn_w1, ffn_w3, ffn_w2, moe_router_w, moe_router_b, moe_w1, moe_w3, moe_w2):
    b, n_lat, d = x.shape
    n_ctx = ctx.shape[1]
    s = n_lat + n_ctx
    depth = w_in.shape[0]
    alpha = (2 * depth) ** 0.25
    off_g = POOL_WIDTH + 3 * NA_WIDTH + 4 * ML_WIDTH
    n_gate_cols = 4 * ML_HEADS

    x_all = jnp.concatenate([x, ctx], axis=1)
    mod = _mod_vectors(c, c_ctx, w_mod, b_mod)

    def mod_part(layer, k):
        return mod[layer, :, k * d:(k + 1) * d].reshape(MOD_ROWS, 1, d)

    rope = _rope_tables(n_lat)
    u = _ln_mod(x_all, mod_part(0, 0), mod_part(0, 1), n_lat)

    for layer in range(depth):
        last = layer == depth - 1
        w_l = w_in[layer]
        w_main = w_l[:, :off_g].astype(BF16)
        w_g = jnp.zeros((d, LANES), BF16).at[:, :n_gate_cols].set(w_l[:, off_g:off_g + n_gate_cols].astype(BF16))
        w_gate = w_l[:, off_g + n_gate_cols:].astype(BF16)
        pool_in, na_qkv, mlqk, mlv, mlo, mlg = _proj_main(u, w_main, w_g)

        a = _pool(pool_in, _block_diag(pool_w[layer]).astype(BF16), pool_scale[layer].reshape(1, -1), n_lat)
        n = _na(na_qkv, _na_bias_classes(na_rpb[layer]), n_lat)
        k_ml, qt_ml, vt_ml = _ml_prep(mlqk, mlv, ml_conv_w[layer], rope, n_lat)
        gcol, grow = _ml_gates(mlg, ml_gate_bias[layer])
        m = _ml_scan(k_ml, qt_ml, vt_ml, gcol, grow, mlo, ml_norm_w[layer].reshape(1, -1), n_lat)

        is_moe = layer % 2 == 1
        i = layer // 2
        router = None
        if is_moe:
            rw = moe_router_w[i].T
            rb = jnp.zeros((1, LANES), F32).at[0, :N_EXPERTS].set(moe_router_b[i])
            router = (rw, rb)
        n_rows_tok = n_lat if last else s
        merged = _merge(a, n, m, u, x_all,
                        w_branch_pool[layer].astype(BF16), w_branch_na[layer].astype(BF16),
                        w_branch_ml[layer].astype(BF16), w_gate, w_out[layer].astype(BF16),
                        mod_part(layer, 2), mod_part(layer, 3), mod_part(layer, 4),
                        ln1_g[layer].reshape(1, d), ln1_b[layer].reshape(1, d), n_lat, alpha, router, n_rows_tok)
        nxt = None if last else (mod_part(layer + 1, 0), mod_part(layer + 1, 1))
        g2 = mod_part(layer, 5)
        lng, lnb = ln2_g[layer].reshape(1, d), ln2_b[layer].reshape(1, d)
        if not is_moe:
            x_mid, tok = merged
            x_all, u = _ffn(tok, x_mid, ffn_w1[i].astype(BF16), ffn_w3[i].astype(BF16), ffn_w2[i].astype(BF16),
                            g2, lng, lnb, n_lat, alpha, nxt)
        else:
            x_mid, tok, route, counts = merged
            x_all, u = _moe_layer(tok, route, counts, x_mid, moe_w1, moe_w3, moe_w2, i, g2, lng, lnb,
                                  n_lat, alpha, n_rows_tok, nxt)
    return x_all[:, :n_lat]


def _moe_layer(tok, route, counts, x_mid, w1, w3, w2, li, g2, lng, lnb, n_lat, alpha, n_rows_tok, nxt):
    dest_tiles, zero_blk, zero_on, block_e, active = _moe_tables(route, counts, n_rows_tok)
    xs = _moe_dispatch(tok, zero_blk, zero_on, dest_tiles, block_e.shape[0], n_rows_tok)
    ys = _moe_experts(xs, block_e, active, w1, w3, w2, li)
    return _combine(ys, dest_tiles, route, x_mid, g2, lng, lnb, n_lat, alpha, n_rows_tok, nxt)
```

```python
import functools

import numpy as np
import jax
import jax.numpy as jnp
from jax import lax
from jax.experimental import pallas as pl
from jax.experimental.pallas import tpu as pltpu

F32 = jnp.float32
BF16 = jnp.bfloat16
HIGHEST = lax.Precision.HIGHEST

GRID_W = 64
POOL_GROUPS = 4
POOL_GROUP_DIM = 64
POOL_WIDTH = POOL_GROUPS * POOL_GROUP_DIM
POOL_HALF_WINDOWS = (1, 2, 4, 8)
NA_HEADS = 4
NA_HEAD_DIM = 64
NA_WIDTH = NA_HEADS * NA_HEAD_DIM
NA_WIN_ROWS = 8
NA_WIN_COLS = 16
ML_HEADS = 4
ML_HEAD_DIM = 128
ML_WIDTH = ML_HEADS * ML_HEAD_DIM
ML_CHUNK = 128
ML_M_INIT = -1e30
ROPE_THETA = 10000.0
N_EXPERTS = 8
TOP_K = 2
N_BRANCHES = 3
LN_EPS = 1e-6

LANES = 128
MOD_ROWS = 16
VMEM_LIMIT = 56 * 1024 * 1024

TM_TOKEN = 768
TM_MERGE = 768
TF_FFN = 1792
TF_MOE = 512
MOE_ROWS = 1024
TM_MOE = 256


def _cparams(sem):
    return pltpu.CompilerParams(dimension_semantics=sem, vmem_limit_bytes=VMEM_LIMIT)


def _sigmoid(x):
    return 0.5 + 0.5 * jnp.tanh(0.5 * x)


def _silu(x):
    t = 0.5 * x
    return t + t * jnp.tanh(t)


def _ln_plain(x):
    xc = x - jnp.mean(x, axis=-1, keepdims=True)
    return xc * lax.rsqrt(jnp.mean(xc * xc, axis=-1, keepdims=True) + LN_EPS)


def _is_ctx_rows(tile_idx, tm, n_lat):
    row = tile_idx * tm + lax.broadcasted_iota(jnp.int32, (tm, 1), 0)
    return row >= n_lat


def _tile_mod(lat_ref, ctx_ref, is_ctx):
    return jnp.where(is_ctx, ctx_ref[0], lat_ref[0])


def _mod_body(s_ref, w_ref, b_ref, o_ref):
    s = _silu(s_ref[...])
    o_ref[0] = jnp.dot(s, w_ref[0], precision=HIGHEST, preferred_element_type=F32) + b_ref[0]


def _mod_vectors(c, c_ctx, w_mod, b_mod):
    depth, d, d6 = w_mod.shape
    b = c.shape[0]
    s = jnp.zeros((MOD_ROWS, d), F32).at[:b].set(c).at[b].set(c_ctx)
    tn = 1536
    return pl.pallas_call(
        _mod_body,
        grid=(depth, d6 // tn),
        in_specs=[pl.BlockSpec((MOD_ROWS, d), lambda l, j: (0, 0)),
                  pl.BlockSpec((1, d, tn), lambda l, j: (l, 0, j)),
                  pl.BlockSpec((1, 1, tn), lambda l, j: (l, 0, j))],
        out_specs=pl.BlockSpec((1, MOD_ROWS, tn), lambda l, j: (l, 0, j)),
        out_shape=jax.ShapeDtypeStruct((depth, MOD_ROWS, d6), F32),
        compiler_params=_cparams(("arbitrary", "arbitrary")),
        name="mod_vectors",
    )(s, w_mod, b_mod.reshape(depth, 1, d6))


def _ln_mod_body(x_ref, shl_ref, shc_ref, scl_ref, scc_ref, u_ref, *, n_lat):
    tm = x_ref.shape[1]
    is_ctx = _is_ctx_rows(pl.program_id(1), tm, n_lat)
    sh = _tile_mod(shl_ref, shc_ref, is_ctx)
    sc = _tile_mod(scl_ref, scc_ref, is_ctx)
    u_ref[0] = (_ln_plain(x_ref[0]) * (1.0 + sc) + sh).astype(u_ref.dtype)


def _row_specs(d, nb):
    return [pl.BlockSpec((1, 1, d), lambda b, j: (b, 0, 0)),
            pl.BlockSpec((1, 1, d), lambda b, j: (nb, 0, 0))]


def _ln_mod(x_all, shift, scale, n_lat):
    b, s, d = x_all.shape
    tm = TM_TOKEN
    return pl.pallas_call(
        functools.partial(_ln_mod_body, n_lat=n_lat),
        grid=(b, s // tm),
        in_specs=[pl.BlockSpec((1, tm, d), lambda i, j: (i, j, 0))] + _row_specs(d, b) + _row_specs(d, b),
        out_specs=pl.BlockSpec((1, tm, d), lambda i, j: (i, j, 0)),
        out_shape=jax.ShapeDtypeStruct((b, s, d), BF16),
        compiler_params=_cparams(("arbitrary", "arbitrary")),
        name="ln_mod",
    )(x_all, shift, shift, scale, scale)


def _proj_main_body(u_ref, w_ref, wg_ref, pool_ref, na_ref, mlqk_ref, mlv_ref, mlo_ref, mlg_ref):
    u = u_ref[0]

    def seg(lo, hi):
        return jnp.dot(u, w_ref[:, lo:hi], preferred_element_type=F32)

    off_na = POOL_WIDTH
    off_ml = off_na + 3 * NA_WIDTH
    off_mlv = off_ml + 2 * ML_WIDTH
    off_mlo = off_ml + 3 * ML_WIDTH
    pool_ref[0] = seg(0, off_na)
    na_ref[0, :, :NA_WIDTH] = (seg(off_na, off_na + NA_WIDTH) * NA_HEAD_DIM ** -0.5).astype(na_ref.dtype)
    na_ref[0, :, NA_WIDTH:] = seg(off_na + NA_WIDTH, off_ml).astype(na_ref.dtype)
    mlqk_ref[0] = seg(off_ml, off_mlv)
    mlv_ref[0] = seg(off_mlv, off_mlo).astype(mlv_ref.dtype)
    mlo_ref[0] = seg(off_mlo, off_mlo + ML_WIDTH).astype(mlo_ref.dtype)
    mlg_ref[0] = jnp.dot(u, wg_ref[...], preferred_element_type=F32)


def _proj_main(u, w_main, w_g):
    b, s, d = u.shape
    tm = TM_TOKEN
    widths = (POOL_WIDTH, 3 * NA_WIDTH, 2 * ML_WIDTH, ML_WIDTH, ML_WIDTH, LANES)
    dtypes = (F32, BF16, F32, BF16, BF16, F32)
    return pl.pallas_call(
        _proj_main_body,
        grid=(b, s // tm),
        in_specs=[pl.BlockSpec((1, tm, d), lambda i, j: (i, j, 0)),
                  pl.BlockSpec(w_main.shape, lambda i, j: (0, 0)),
                  pl.BlockSpec(w_g.shape, lambda i, j: (0, 0))],
        out_specs=[pl.BlockSpec((1, tm, w), lambda i, j: (i, j, 0)) for w in widths],
        out_shape=[jax.ShapeDtypeStruct((b, s, w), dt) for w, dt in zip(widths, dtypes)],
        compiler_params=_cparams(("arbitrary", "arbitrary")),
        name="proj_main",
    )(u, w_main, w_g)


def _pool_segment(a, w_bd, scale):
    n = a.shape[0]
    t = lax.broadcasted_iota(jnp.int32, a.shape, 0)
    grp = lax.broadcasted_iota(jnp.int32, a.shape, 1) // POOL_GROUP_DIM

    def up(x, k):
        return jnp.where(t < n - k, pltpu.roll(x, n - k, 0), 0.0)

    def down(x, k):
        return jnp.where(t >= k, pltpu.roll(x, k, 0), 0.0)

    ahead = a
    behind = down(a, 1)
    wsum = jnp.zeros_like(a)
    half = jnp.zeros_like(t)
    for g, h in enumerate(POOL_HALF_WINDOWS):
        if g > 0:
            ahead = ahead + up(ahead, h // 2)
            behind = behind + down(behind, h // 2)
        wsum = jnp.where(grp == g, ahead + behind, wsum)
        half = jnp.where(grp == g, h, half)
    cnt = jnp.minimum(t + half, n) - jnp.maximum(t - half, 0)
    mean = wsum / cnt.astype(F32)
    out = jnp.dot((mean - a).astype(BF16), w_bd, preferred_element_type=F32)
    return out * scale


def _pool_body(a_ref, w_ref, sc_ref, o_ref, *, n_lat):
    w_bd = w_ref[...]
    scale = sc_ref[...]
    s = a_ref.shape[1]
    o_ref[0, :n_lat] = _pool_segment(a_ref[0, :n_lat], w_bd, scale).astype(o_ref.dtype)
    o_ref[0, n_lat:] = _pool_segment(a_ref[0, n_lat:s], w_bd, scale).astype(o_ref.dtype)


def _pool(pool_in, w_bd, scale, n_lat):
    b, s, w = pool_in.shape
    return pl.pallas_call(
        functools.partial(_pool_body, n_lat=n_lat),
        grid=(b,),
        in_specs=[pl.BlockSpec((1, s, w), lambda i: (i, 0, 0)),
                  pl.BlockSpec((w, w), lambda i: (0, 0)),
                  pl.BlockSpec((1, w), lambda i: (0, 0))],
        out_specs=pl.BlockSpec((1, s, w), lambda i: (i, 0, 0)),
        out_shape=jax.ShapeDtypeStruct((b, s, w), BF16),
        compiler_params=_cparams(("arbitrary",)),
        name="pool",
    )(pool_in, w_bd, scale)


NA_ROWS_PER_STEP = 4


def _na_body(q_ref, k_ref, v_ref, bias_ref, o_ref, *, n_lat, n_rows):
    step = pl.program_id(1)
    s_tot = k_ref.shape[1]
    kc = k_ref[0, n_lat:s_tot, :]
    vc = v_ref[0, n_lat:s_tot, :]
    nt = (((1,), (1,)), ((), ()))
    head = lax.broadcasted_iota(jnp.int32, (GRID_W, NA_WIDTH), 1) // NA_HEAD_DIM
    wr = NA_WIN_ROWS

    def one_row(g, local):
        rows = slice(g * GRID_W, (g + 1) * GRID_W)
        q = q_ref[0, rows, :]
        zero = jnp.zeros_like(q)
        q4 = jnp.concatenate([jnp.where(head == h, q, zero) for h in range(NA_HEADS)], axis=0)
        s_ctx = lax.dot_general(q4, kc, nt, preferred_element_type=F32)
        m = jnp.max(s_ctx, axis=-1, keepdims=True)
        if local:
            r = step * NA_ROWS_PER_STEP + g
            rs = jnp.clip(r - wr // 2, 0, n_rows - wr)
            start = pl.multiple_of(rs * GRID_W, GRID_W)
            kw = k_ref[0, pl.ds(start, wr * GRID_W), :]
            vw = v_ref[0, pl.ds(start, wr * GRID_W), :]
            s_loc = lax.dot_general(q4, kw, nt, preferred_element_type=F32) + bias_ref[rs - r + wr - 1]
            m = jnp.maximum(m, jnp.max(s_loc, axis=-1, keepdims=True))
            p_loc = jnp.exp(s_loc - m)
        p_ctx = jnp.exp(s_ctx - m)
        denom = jnp.sum(p_ctx, axis=-1, keepdims=True)
        o4 = jnp.dot(p_ctx.astype(BF16), vc, preferred_element_type=F32)
        if local:
            denom = denom + jnp.sum(p_loc, axis=-1, keepdims=True)
            o4 = o4 + jnp.dot(p_loc.astype(BF16), vw, preferred_element_type=F32)
        o4 = o4 / denom
        out = jnp.zeros((GRID_W, NA_WIDTH), F32)
        for h in range(NA_HEADS):
            out = out + jnp.where(head == h, o4[h * GRID_W:(h + 1) * GRID_W], 0.0)
        o_ref[0, rows, :] = out.astype(o_ref.dtype)

    @pl.when(step * NA_ROWS_PER_STEP < n_rows)
    def _():
        for g in range(NA_ROWS_PER_STEP):
            one_row(g, True)

    @pl.when(step * NA_ROWS_PER_STEP >= n_rows)
    def _():
        for g in range(NA_ROWS_PER_STEP):
            one_row(g, False)


def _na_bias_classes(rpb):
    col = np.arange(GRID_W)
    col_start = np.clip(col - NA_WIN_COLS // 2, 0, GRID_W - NA_WIN_COLS)
    in_window = (col[None, :] >= col_start[:, None]) & (col[None, :] < col_start[:, None] + NA_WIN_COLS)
    col_off = np.clip(col[None, :] - col[:, None], -(NA_WIN_COLS - 1), NA_WIN_COLS - 1) + NA_WIN_COLS - 1
    n_off = 2 * NA_WIN_COLS - 1
    pick = (col_off[None, :, :] == np.arange(n_off)[:, None, None]).astype(np.float32)
    by_col = jnp.einsum('hro,oqk->hrqk', rpb.astype(F32), jnp.asarray(pick), precision=HIGHEST)
    by_col = jnp.where(in_window[None, None], by_col, -jnp.inf)
    b = jnp.stack([by_col[:, o:o + NA_WIN_ROWS] for o in range(NA_WIN_ROWS)], axis=0)
    b = b.transpose(0, 1, 3, 2, 4)
    return b.reshape(NA_WIN_ROWS, NA_HEADS * GRID_W, NA_WIN_ROWS * GRID_W)


def _na(na_qkv, bias_cls, n_lat):
    b, s, _ = na_qkv.shape
    n_rows = n_lat // GRID_W
    tq = NA_ROWS_PER_STEP * GRID_W

    return pl.pallas_call(
        functools.partial(_na_body, n_lat=n_lat, n_rows=n_rows),
        grid=(b, s // tq),
        in_specs=[pl.BlockSpec((1, tq, NA_WIDTH), lambda i, r: (i, r, 0)),
                  pl.BlockSpec((1, s, NA_WIDTH), lambda i, r: (i, 0, 1)),
                  pl.BlockSpec((1, s, NA_WIDTH), lambda i, r: (i, 0, 2)),
                  pl.BlockSpec(bias_cls.shape, lambda i, r: (0, 0, 0))],
        out_specs=pl.BlockSpec((1, tq, NA_WIDTH), lambda i, r: (i, r, 0)),
        out_shape=jax.ShapeDtypeStruct((b, s, NA_WIDTH), BF16),
        compiler_params=_cparams(("arbitrary", "arbitrary")),
        name="nbr_attention",
    )(na_qkv, na_qkv, na_qkv, bias_cls)


def _rope_tables(n_lat):
    half = ML_HEAD_DIM // 2
    nf = half // 2
    inv = ROPE_THETA ** (-np.arange(nf, dtype=np.float64) / nf)
    pos = np.arange(n_lat)
    ang = np.concatenate([(pos // GRID_W)[:, None] * inv[None, :]] * 2
                         + [(pos % GRID_W)[:, None] * inv[None, :]] * 2, axis=1)
    lane = np.arange(ML_HEAD_DIM)
    second = (lane % half) >= nf
    cos = np.cos(ang)
    sin = np.sin(ang)
    sin_from_lo = np.where(second[None, :], sin, 0.0)
    sin_from_hi = np.where(second[None, :], 0.0, -sin)
    return (jnp.asarray(cos, F32), jnp.asarray(sin_from_lo, F32), jnp.asarray(sin_from_hi, F32))


def _conv_silu(x, w):
    n = x.shape[0]
    t = lax.broadcasted_iota(jnp.int32, x.shape, 0)
    prev = jnp.where(t >= 1, pltpu.roll(x, 1, 0), 0.0)
    nxt = jnp.where(t < n - 1, pltpu.roll(x, n - 1, 0), 0.0)
    return _silu(w[0:1] * prev + w[1:2] * x + w[2:3] * nxt)


def _rope(x, cos, sin_lo, sin_hi):
    nf = ML_HEAD_DIM // 4
    return x * cos + pltpu.roll(x, nf, 1) * sin_lo + pltpu.roll(x, ML_HEAD_DIM - nf, 1) * sin_hi


def _ml_prep_body(xq_ref, xk_ref, v_ref, wq_ref, wk_ref, cos_ref, slo_ref, shi_ref, k_ref, qt_ref, vt_ref, *, n_lat):
    s = xq_ref.shape[1]
    wq = wq_ref[...]
    wk = wk_ref[...]
    cos, slo, shi = cos_ref[...], slo_ref[...], shi_ref[...]
    kscale = ML_HEAD_DIM ** -0.5
    for lo, hi, rotary in ((0, n_lat, True), (n_lat, s, False)):
        q = _conv_silu(xq_ref[0, lo:hi], wq)
        k = _conv_silu(xk_ref[0, lo:hi], wk) * kscale
        if rotary:
            q = _rope(q, cos, slo, shi)
            k = _rope(k, cos, slo, shi)
        k_ref[0, lo:hi] = k.astype(k_ref.dtype)
        v = v_ref[0, lo:hi].astype(F32)
        for j in range((hi - lo) // ML_CHUNK):
            rows = slice(j * ML_CHUNK, (j + 1) * ML_CHUNK)
            qt_ref[0, 0, lo // ML_CHUNK + j] = q[rows].T.astype(qt_ref.dtype)
            vt_ref[0, 0, lo // ML_CHUNK + j] = v[rows].T.astype(vt_ref.dtype)


def _ml_prep(mlqk, mlv, conv_w, rope, n_lat):
    b, s, _ = mlqk.shape
    hd = ML_HEAD_DIM
    nc = s // ML_CHUNK
    tab = pl.BlockSpec((n_lat, hd), lambda i, h: (0, 0))
    chunks = pl.BlockSpec((1, 1, nc, hd, ML_CHUNK), lambda i, h: (i, h, 0, 0, 0))
    chunk_shape = jax.ShapeDtypeStruct((b, ML_HEADS, nc, hd, ML_CHUNK), BF16)
    return pl.pallas_call(
        functools.partial(_ml_prep_body, n_lat=n_lat),
        grid=(b, ML_HEADS),
        in_specs=[pl.BlockSpec((1, s, hd), lambda i, h: (i, 0, h)),
                  pl.BlockSpec((1, s, hd), lambda i, h: (i, 0, ML_HEADS + h)),
                  pl.BlockSpec((1, s, hd), lambda i, h: (i, 0, h)),
                  pl.BlockSpec((3, hd), lambda i, h: (0, h)),
                  pl.BlockSpec((3, hd), lambda i, h: (0, ML_HEADS + h)),
                  tab, tab, tab],
        out_specs=[pl.BlockSpec((1, s, hd), lambda i, h: (i, 0, h)), chunks, chunks],
        out_shape=[jax.ShapeDtypeStruct((b, s, ML_WIDTH), BF16), chunk_shape, chunk_shape],
        compiler_params=_cparams(("arbitrary", "arbitrary")),
        name="mlstm_prep",
    )(mlqk, mlqk, mlv, conv_w, conv_w, *rope)


def _log_sigmoid(x):
    return jnp.minimum(x, 0.0) - jnp.log1p(jnp.exp(-jnp.abs(x)))


def _ml_gates_body(gc_ref, gr_ref, bc_ref, br_ref, oc_ref, or_ref):
    nc = gr_ref.shape[1]
    c = ML_CHUNK
    ii = lax.broadcasted_iota(jnp.int32, (c, c), 0)
    jj = lax.broadcasted_iota(jnp.int32, (c, c), 1)
    lower = (ii >= jj).astype(F32)
    upper = (ii <= jj).astype(F32)
    lane_kind = lax.broadcasted_iota(jnp.int32, (c, LANES), 1) // ML_HEADS
    row_kind = lax.broadcasted_iota(jnp.int32, (4 * ML_HEADS, c), 0) // ML_HEADS
    for j in range(nc):
        g = gc_ref[0, j * c:(j + 1) * c, :] + bc_ref[...]
        lf = _log_sigmoid(g)
        pre = jnp.dot(lower, lf, precision=HIGHEST, preferred_element_type=F32)
        suf = jnp.dot(upper, lf, precision=HIGHEST, preferred_element_type=F32)
        oc_ref[0, j * c:(j + 1) * c, :] = jnp.where(lane_kind == 1, pre, jnp.where(lane_kind == 3, suf, g))
        gr = gr_ref[0, j] + br_ref[...]
        lfr = _log_sigmoid(gr)
        pre_r = jnp.dot(lfr, upper, precision=HIGHEST, preferred_element_type=F32)
        suf_r = jnp.dot(lfr, lower, precision=HIGHEST, preferred_element_type=F32)
        or_ref[0, j] = jnp.where(row_kind == 1, pre_r, jnp.where(row_kind == 3, suf_r, gr))


def _ml_gates(mlg, gate_bias):
    b, s, _ = mlg.shape
    nc = s // ML_CHUNK
    ng = 4 * ML_HEADS
    g_rows = mlg[:, :, :ng].reshape(b, nc, ML_CHUNK, ng).transpose(0, 1, 3, 2)
    bias_c = jnp.zeros((1, LANES), F32).at[0, :ng].set(gate_bias)
    bias_r = gate_bias.reshape(ng, 1)
    return pl.pallas_call(
        _ml_gates_body,
        grid=(b,),
        in_specs=[pl.BlockSpec((1, s, LANES), lambda i: (i, 0, 0)),
                  pl.BlockSpec((1, nc, ng, ML_CHUNK), lambda i: (i, 0, 0, 0)),
                  pl.BlockSpec((1, LANES), lambda i: (0, 0)),
                  pl.BlockSpec((ng, 1), lambda i: (0, 0))],
        out_specs=[pl.BlockSpec((1, s, LANES), lambda i: (i, 0, 0)),
                   pl.BlockSpec((1, nc, ng, ML_CHUNK), lambda i: (i, 0, 0, 0))],
        out_shape=[jax.ShapeDtypeStruct((b, s, LANES), F32),
                   jax.ShapeDtypeStruct((b, nc, ng, ML_CHUNK), F32)],
        compiler_params=_cparams(("arbitrary",)),
        name="mlstm_gates",
    )(mlg, g_rows, bias_c, bias_r)


def _ml_step(k, qt, vt_aug, a_c, b_r, state_t, m, *, forward):
    c = ML_CHUNK
    d = ML_HEAD_DIM
    si = lax.broadcasted_iota(jnp.int32, (c, c), 0)
    ti = lax.broadcasted_iota(jnp.int32, (c, c), 1)
    allowed = (si <= ti) if forward else (si >= ti)
    log_d = jnp.where(allowed, b_r + a_c, -jnp.inf)
    inter = b_r + m
    m_t = jnp.maximum(jnp.max(log_d, axis=0, keepdims=True), inter)
    kq = jnp.dot(k, qt, preferred_element_type=F32)
    s_t = (kq * jnp.exp(log_d - m_t)).astype(BF16)
    w_inter = jnp.exp(inter - m_t)
    tot = (jnp.dot(vt_aug, s_t, preferred_element_type=F32)
           + w_inter * jnp.dot(state_t.astype(BF16), qt, preferred_element_type=F32))
    num = tot[:d]
    den = tot[d:d + 1]
    h_t = num / jnp.maximum(jnp.abs(den), jnp.exp(-m_t))
    b_end = b_r[:, c - 1:c] if forward else b_r[:, 0:1]
    log_w = b_end + a_c
    m_new = jnp.maximum(b_end + m, jnp.max(log_w, axis=0, keepdims=True))
    w_c = jnp.exp(log_w - m_new)
    decay = jnp.exp(b_end + m - m_new)
    kw = (k.astype(F32) * w_c).astype(BF16)
    state_new = decay * state_t + jnp.dot(vt_aug, kw, preferred_element_type=F32)
    return h_t, state_new, m_new


ML_HEADS_PER_STEP = 4


def _ml_scan_body(k_ref, qt_ref, vt_ref, gc_ref, gr_ref, o_ref, nw_ref, out_ref, hsum_ref, *, n_lat):
    c = ML_CHUNK
    d = ML_HEAD_DIM
    s = k_ref.shape[1]
    heads = qt_ref.shape[1]
    nc_lat = n_lat // c
    nc_all = s // c
    ones_row = (lax.broadcasted_iota(jnp.int32, (d, c), 0) == 0).astype(BF16)

    def rows(j):
        return pl.ds(j * c if isinstance(j, int) else pl.multiple_of(j * c, c), c)

    def run(j, h, state, m, forward):
        vt_aug = jnp.concatenate([vt_ref[0, h, j], ones_row], axis=0)
        gc = gc_ref[0, h, rows(j), :]
        gr = gr_ref[0, h, j]
        k0 = 0 if forward else 2
        a_c = gc[:, k0:k0 + 1] - gc[:, k0 + 1:k0 + 2]
        return _ml_step(k_ref[0, rows(j), h * d:(h + 1) * d], qt_ref[0, h, j], vt_aug, a_c, gr[k0 + 1:k0 + 2],
                        state, m, forward=forward)

    def finalize(j, h, hsum_t):
        cols = slice(h * d, (h + 1) * d)
        hc = hsum_t - jnp.mean(hsum_t, axis=0, keepdims=True)
        hn = (hc * lax.rsqrt(jnp.mean(hc * hc, axis=0, keepdims=True) + LN_EPS)).T
        gate = _sigmoid(o_ref[0, rows(j), cols].astype(F32))
        out_ref[0, rows(j), cols] = (gate * hn * nw_ref[:, cols]).astype(out_ref.dtype)

    zero_state = jnp.zeros((2 * d, d), F32)
    m0 = jnp.full((1, 1), ML_M_INIT, F32)
    carry = []
    for h in range(heads):
        sf, mf = zero_state, m0
        hc_f = []
        for j in range(nc_lat, nc_all):
            out, sf, mf = run(j, h, sf, mf, True)
            hc_f.append(out)
        sb, mb = zero_state, m0
        hc_b = {}
        for j in range(nc_all - 1, nc_lat - 1, -1):
            out, sb, mb = run(j, h, sb, mb, False)
            hc_b[j] = out
        for idx, j in enumerate(range(nc_lat, nc_all)):
            finalize(j, h, hc_f[idx] + hc_b[j])
        carry += [sf, mf, sb, mb]

    half = nc_lat // 2

    def first_half(i, carry):
        jb = nc_lat - 1 - i
        new = []
        for h in range(heads):
            sf, mf, sb, mb = carry[4 * h:4 * h + 4]
            hf, sf, mf = run(i, h, sf, mf, True)
            hb, sb, mb = run(jb, h, sb, mb, False)
            hsum_ref[h, i] = hf
            hsum_ref[h, jb] = hb
            new += [sf, mf, sb, mb]
        return tuple(new)

    carry = lax.fori_loop(0, half, first_half, tuple(carry))

    def second_half(i, carry):
        jb = nc_lat - 1 - i
        new = []
        for h in range(heads):
            sf, mf, sb, mb = carry[4 * h:4 * h + 4]
            hf, sf, mf = run(i, h, sf, mf, True)
            hb, sb, mb = run(jb, h, sb, mb, False)
            finalize(i, h, hf + hsum_ref[h, i])
            finalize(jb, h, hb + hsum_ref[h, jb])
            new += [sf, mf, sb, mb]
        return tuple(new)

    lax.fori_loop(half, nc_lat, second_half, carry)


def _ml_scan(k, qt, vt, gcol, grow, mlo, norm_w, n_lat):
    b, s, _ = k.shape
    hd = ML_HEAD_DIM
    hb = ML_HEADS_PER_STEP
    nc = s // ML_CHUNK
    ng = 4 * ML_HEADS
    gc_h = gcol[:, :, :ng].reshape(b, s, 4, ML_HEADS).transpose(0, 3, 1, 2)
    gr_h = grow.reshape(b, nc, 4, ML_HEADS, ML_CHUNK).transpose(0, 3, 1, 2, 4)
    tok = pl.BlockSpec((1, s, hb * hd), lambda i, h: (i, 0, h))
    chunks = pl.BlockSpec((1, hb, nc, hd, ML_CHUNK), lambda i, h: (i, h, 0, 0, 0))
    return pl.pallas_call(
        functools.partial(_ml_scan_body, n_lat=n_lat),
        grid=(b, ML_HEADS // hb),
        in_specs=[tok, chunks, chunks,
                  pl.BlockSpec((1, hb, s, 4), lambda i, h: (i, h, 0, 0)),
                  pl.BlockSpec((1, hb, nc, 4, ML_CHUNK), lambda i, h: (i, h, 0, 0, 0)),
                  tok,
                  pl.BlockSpec((1, hb * hd), lambda i, h: (0, h))],
        out_specs=tok,
        out_shape=jax.ShapeDtypeStruct((b, s, ML_WIDTH), BF16),
        scratch_shapes=[pltpu.VMEM((hb, n_lat // ML_CHUNK, hd, ML_CHUNK), F32)],
        compiler_params=_cparams(("arbitrary", "arbitrary")),
        name="mlstm_scan",
    )(k, qt, vt, gc_h, gr_h, mlo, norm_w)


def _top2(logits, n_valid):
    lane = lax.broadcasted_iota(jnp.int32, logits.shape, 1).astype(F32)
    neg = jnp.float32(-jnp.inf)
    l = jnp.where(lane < n_valid, logits, neg)
    m1 = jnp.max(l, axis=-1, keepdims=True)
    e1 = jnp.min(jnp.where(l == m1, lane, float(LANES)), axis=-1, keepdims=True)
    l2 = jnp.where(lane == e1, neg, l)
    m2 = jnp.max(l2, axis=-1, keepdims=True)
    e2 = jnp.min(jnp.where(l2 == m2, lane, float(LANES)), axis=-1, keepdims=True)
    x2 = jnp.exp(m2 - m1)
    p1 = 1.0 / (1.0 + x2)
    p2 = x2 / (1.0 + x2)
    return jnp.where(lane == 0, e1, jnp.where(lane == 1, e2, jnp.where(lane == 2, p1, jnp.where(lane == 3, p2, 0.0))))


def _route_ranks(route, valid, cnt_ref):
    tm = route.shape[0]
    lane = lax.broadcasted_iota(jnp.int32, route.shape, 1).astype(F32)
    pick1 = jnp.where(valid, (lane == route[:, 0:1]).astype(F32), 0.0)
    pick2 = jnp.where(valid, (lane == route[:, 1:2]).astype(F32), 0.0)
    both = pick1 + pick2
    before = (lax.broadcasted_iota(jnp.int32, (tm, tm), 0) > lax.broadcasted_iota(jnp.int32, (tm, tm), 1))
    prefix = jnp.dot(before.astype(BF16), both.astype(BF16), preferred_element_type=F32) + cnt_ref[...]
    rank1 = jnp.sum(prefix * pick1, axis=-1, keepdims=True)
    rank2 = jnp.sum(prefix * pick2, axis=-1, keepdims=True)
    cnt_ref[...] += jnp.sum(both, axis=0, keepdims=True)
    return jnp.where(lane == 4, rank1, jnp.where(lane == 5, rank2, route))


def _merge_body(*refs, n_lat, alpha, moe, n_rows_tok):
    (a_ref, n_ref, m_ref, u_ref, x_ref, wbp_ref, wbn_ref, wbm_ref, wg_ref, wo_ref,
     g1l_ref, g1c_ref, shl_ref, shc_ref, scl_ref, scc_ref, lng_ref, lnb_ref) = refs[:18]
    if moe:
        rw_ref, rb_ref, x_out, tok_out, route_out, cnt_out, cnt_ref = refs[18:]
    else:
        x_out, tok_out = refs[18:]
    tm = x_ref.shape[1]
    d = x_ref.shape[2]
    is_ctx = _is_ctx_rows(pl.program_id(1), tm, n_lat)
    u = u_ref[0]
    y = jnp.zeros((tm, d), F32)
    for k, (br_ref, w_ref) in enumerate(((a_ref, wbp_ref), (n_ref, wbn_ref), (m_ref, wbm_ref))):
        t = jnp.tanh(jnp.dot(u, wg_ref[:, k * d:(k + 1) * d], preferred_element_type=F32))
        half_v = jnp.dot(br_ref[0], w_ref[...], preferred_element_type=F32)
        y = y + (half_v + half_v * t)
    y = jnp.dot(y.astype(BF16), wo_ref[...], preferred_element_type=F32)
    z = alpha * x_ref[0] + _tile_mod(g1l_ref, g1c_ref, is_ctx) * y
    xn = _ln_plain(z) * lng_ref[...] + lnb_ref[...]
    x_out[0] = xn
    tok = _ln_plain(xn) * (1.0 + _tile_mod(scl_ref, scc_ref, is_ctx)) + _tile_mod(shl_ref, shc_ref, is_ctx)
    tok_out[0] = tok.astype(tok_out.dtype)
    if moe:
        @pl.when((pl.program_id(0) == 0) & (pl.program_id(1) == 0))
        def _():
            cnt_ref[...] = jnp.zeros_like(cnt_ref)

        tok_hi = tok.astype(BF16)
        tok_lo = (tok - tok_hi.astype(F32)).astype(BF16)
        both = jnp.dot(tok_hi, rw_ref[...], preferred_element_type=F32)
        logits = (both[:, :LANES] + both[:, LANES:]
                  + jnp.dot(tok_lo, rw_ref[:, :LANES], preferred_element_type=F32))
        valid = (pl.program_id(1) * tm + lax.broadcasted_iota(jnp.int32, (tm, 1), 0)) < n_rows_tok
        route_out[0] = _route_ranks(_top2(logits + rb_ref[...], N_EXPERTS), valid, cnt_ref)
        cnt_out[...] = cnt_ref[...]


def _merge(a, n, m, u, x_all, w_bp, w_bn, w_bm, w_gate, w_o, g1, sh2, sc2, ln_g, ln_b, n_lat, alpha,
           router=None, n_rows_tok=0):
    b, s, d = x_all.shape
    tm = TM_MERGE
    moe = router is not None

    def tok(w):
        return pl.BlockSpec((1, tm, w), lambda i, j: (i, j, 0))

    def full(arr):
        return pl.BlockSpec(arr.shape, lambda i, j: (0,) * arr.ndim, pipeline_mode=pl.Buffered(1))

    in_specs = ([tok(a.shape[2]), tok(n.shape[2]), tok(m.shape[2]), tok(d), tok(d),
                 full(w_bp), full(w_bn), full(w_bm), full(w_gate), full(w_o)]
                + _row_specs(d, b) * 3 + [full(ln_g), full(ln_b)])
    args = [a, n, m, u, x_all, w_bp, w_bn, w_bm, w_gate, w_o, g1, g1, sh2, sh2, sc2, sc2, ln_g, ln_b]
    out_specs = [tok(d), tok(d)]
    out_shape = [jax.ShapeDtypeStruct((b, s, d), F32), jax.ShapeDtypeStruct((b, s, d), F32 if moe else BF16)]
    scratch = []
    if moe:
        in_specs += [full(router[0]), full(router[1])]
        args += list(router)
        out_specs += [tok(LANES), pl.BlockSpec((1, LANES), lambda i, j: (0, 0))]
        out_shape += [jax.ShapeDtypeStruct((b, s, LANES), F32), jax.ShapeDtypeStruct((1, LANES), F32)]
        scratch = [pltpu.VMEM((1, LANES), F32)]
    return pl.pallas_call(
        functools.partial(_merge_body, n_lat=n_lat, alpha=alpha, moe=moe, n_rows_tok=n_rows_tok),
        grid=(b, s // tm),
        in_specs=in_specs,
        out_specs=out_specs,
        out_shape=out_shape,
        scratch_shapes=scratch,
        compiler_params=_cparams(("arbitrary", "arbitrary")),
        name="merge",
    )(*args)


def _finish_rows(f, x, g2, lng, lnb, alpha):
    return _ln_plain(alpha * x + g2 * f) * lng + lnb


def _ffn_body(*refs, n_lat, alpha, with_next):
    (t_ref, x_ref, w1_ref, w3_ref, w2_ref, g2l_ref, g2c_ref, lng_ref, lnb_ref) = refs[:9]
    if with_next:
        shl_ref, shc_ref, scl_ref, scc_ref, x_out, u_out, acc_ref = refs[9:]
    else:
        x_out, acc_ref = refs[9:]
    f = pl.program_id(2)
    t = t_ref[0]
    h1 = jnp.dot(t, w1_ref[...], preferred_element_type=F32)
    h3 = jnp.dot(t, w3_ref[...], preferred_element_type=F32)
    contrib = jnp.dot((_silu(h1) * h3).astype(BF16), w2_ref[...], preferred_element_type=F32)

    @pl.when(f == 0)
    def _():
        acc_ref[...] = contrib

    @pl.when(f > 0)
    def _():
        acc_ref[...] += contrib

    @pl.when(f == pl.num_programs(2) - 1)
    def _():
        tm = x_ref.shape[1]
        is_ctx = _is_ctx_rows(pl.program_id(1), tm, n_lat)
        xn = _finish_rows(acc_ref[...], x_ref[0], _tile_mod(g2l_ref, g2c_ref, is_ctx), lng_ref[...], lnb_ref[...], alpha)
        x_out[0] = xn
        if with_next:
            u = _ln_plain(xn) * (1.0 + _tile_mod(scl_ref, scc_ref, is_ctx)) + _tile_mod(shl_ref, shc_ref, is_ctx)
            u_out[0] = u.astype(u_out.dtype)


def _ffn(tok, x_all, w1, w3, w2, g2, ln_g, ln_b, n_lat, alpha, nxt=None):
    b, s, d = x_all.shape
    dff = w1.shape[1]
    tm, tf = TM_TOKEN, TF_FFN
    with_next = nxt is not None

    def tokspec():
        return pl.BlockSpec((1, tm, d), lambda i, j, f: (i, j, 0))

    def rows():
        return [pl.BlockSpec((1, 1, d), lambda i, j, f: (i, 0, 0)),
                pl.BlockSpec((1, 1, d), lambda i, j, f: (b, 0, 0))]

    vec = pl.BlockSpec((1, d), lambda i, j, f: (0, 0))
    in_specs = [tokspec(), tokspec(),
                pl.BlockSpec((d, tf), lambda i, j, f: (0, f)),
                pl.BlockSpec((d, tf), lambda i, j, f: (0, f)),
                pl.BlockSpec((tf, d), lambda i, j, f: (f, 0))] + rows() + [vec, vec]
    args = [tok, x_all, w1, w3, w2, g2, g2, ln_g, ln_b]
    out_specs = [tokspec()]
    out_shape = [jax.ShapeDtypeStruct((b, s, d), F32)]
    if with_next:
        in_specs += rows() + rows()
        args += [nxt[0], nxt[0], nxt[1], nxt[1]]
        out_specs.append(tokspec())
        out_shape.append(jax.ShapeDtypeStruct((b, s, d), BF16))
    res = pl.pallas_call(
        functools.partial(_ffn_body, n_lat=n_lat, alpha=alpha, with_next=with_next),
        grid=(b, s // tm, dff // tf),
        in_specs=in_specs,
        out_specs=out_specs,
        out_shape=out_shape,
        scratch_shapes=[pltpu.VMEM((tm, d), F32)],
        compiler_params=_cparams(("arbitrary", "arbitrary", "arbitrary")),
        name="ffn_dense",
    )(*args)
    return res if with_next else (res[0], None)


def _moe_tables(route, counts, n_rows_tok):
    r = MOE_ROWS
    tm = TM_MOE
    b = route.shape[0]
    cnt = counts[0, :N_EXPERTS].astype(jnp.int32)
    nblk = (cnt + r - 1) // r
    bend = jnp.cumsum(nblk)
    bstart = bend - nblk
    total = bend[-1]
    n_blocks = -(-(b * n_rows_tok * TOP_K) // r) + N_EXPERTS
    part = route[:, :n_rows_tok]
    experts = jnp.arange(N_EXPERTS, dtype=F32)
    seg_row0 = jnp.sum((part[:, :, 0:TOP_K, None] == experts).astype(jnp.int32) * (bstart * r), axis=-1)
    dest = seg_row0 + part[:, :, 4:4 + TOP_K].astype(jnp.int32)
    nt = n_rows_tok // tm
    dest_tiles = dest.reshape(b, nt, tm, TOP_K).transpose(0, 1, 3, 2).reshape(b * nt, 1, TOP_K * tm)
    blk = jnp.arange(n_blocks, dtype=jnp.int32)
    e_of = jnp.sum((blk[:, None] >= bend[None, :]).astype(jnp.int32), axis=1)
    e_last = jnp.sum((total - 1 >= bend).astype(jnp.int32))
    active = (blk < total).astype(jnp.int32)
    block_e = jnp.minimum(e_of, e_last)
    spare = total + jnp.arange(N_EXPERTS, dtype=jnp.int32)
    zero_blk = jnp.concatenate([bend - 1, spare])
    zero_on = jnp.concatenate([nblk > 0, spare < n_blocks]).astype(jnp.int32)
    zero_blk = jnp.clip(zero_blk, 0, n_blocks - 1)
    return dest_tiles, zero_blk, zero_on, block_e, active


def _dispatch_body(zblk_ref, zon_ref, dest_ref, tok_ref, xs_hbm, zbuf, zsem, dsem):
    i = pl.program_id(0)
    tm = tok_ref.shape[1]
    r = zbuf.shape[0]

    def zero_copy(z):
        start = pl.multiple_of(zblk_ref[z] * r, r)
        return pltpu.make_async_copy(zbuf, xs_hbm.at[pl.ds(start, r)], zsem)

    @pl.when(i == 0)
    def _():
        zbuf[...] = jnp.zeros_like(zbuf)
        for z in range(2 * N_EXPERTS):
            @pl.when(zon_ref[z] > 0)
            def _():
                zero_copy(z).start()
        for z in range(2 * N_EXPERTS):
            @pl.when(zon_ref[z] > 0)
            def _():
                zero_copy(z).wait()

    def row_copy(j, k):
        return pltpu.make_async_copy(tok_ref.at[0, pl.ds(j, 1)],
                                     xs_hbm.at[pl.ds(dest_ref[0, 0, k * tm + j], 1)], dsem)

    def issue(j, c):
        for k in range(TOP_K):
            row_copy(j, k).start()
        return c

    lax.fori_loop(0, tm, issue, 0, unroll=8)
    for _ in range(TOP_K):
        pltpu.make_async_copy(tok_ref.at[0], xs_hbm.at[pl.ds(0, tm)], dsem).wait()


def _moe_dispatch(tok, zero_blk, zero_on, dest_tiles, n_blocks, n_rows_tok):
    d = tok.shape[2]
    tm = TM_MOE
    nt = n_rows_tok // tm
    grid_spec = pltpu.PrefetchScalarGridSpec(
        num_scalar_prefetch=2,
        grid=(dest_tiles.shape[0],),
        in_specs=[pl.BlockSpec((1, 1, TOP_K * tm), lambda i, zb, zo: (i, 0, 0), memory_space=pltpu.SMEM),
                  pl.BlockSpec((1, tm, d), lambda i, zb, zo: (i // nt, i % nt, 0))],
        out_specs=pl.BlockSpec(memory_space=pl.ANY),
        scratch_shapes=[pltpu.VMEM((MOE_ROWS, d), F32),
                        pltpu.SemaphoreType.DMA(()), pltpu.SemaphoreType.DMA(())],
    )
    return pl.pallas_call(
        _dispatch_body,
        grid_spec=grid_spec,
        out_shape=jax.ShapeDtypeStruct((n_blocks * MOE_ROWS, d), F32),
        compiler_params=_cparams(("arbitrary",)),
        name="moe_dispatch",
    )(zero_blk, zero_on, dest_tiles, tok)


def _experts_body(be_ref, act_ref, x_ref, w1_ref, w3_ref, w2_ref, y_ref, xb_ref):
    i = pl.program_id(0)
    f = pl.program_id(1)

    @pl.when((act_ref[i] == 0) & (f == 0))
    def _():
        y_ref[...] = jnp.zeros_like(y_ref)

    @pl.when(act_ref[i] > 0)
    def _():
        @pl.when(f == 0)
        def _():
            xb_ref[...] = x_ref[...].astype(BF16)

        x = xb_ref[...]
        h1 = jnp.dot(x, w1_ref[0, 0].astype(BF16), preferred_element_type=F32)
        h3 = jnp.dot(x, w3_ref[0, 0].astype(BF16), preferred_element_type=F32)
        contrib = jnp.dot((_silu(h1) * h3).astype(BF16), w2_ref[0, 0].astype(BF16), preferred_element_type=F32)

        @pl.when(f == 0)
        def _():
            y_ref[...] = contrib

        @pl.when(f > 0)
        def _():
            y_ref[...] += contrib


def _moe_experts(xs, block_e, active, w1, w3, w2, li):
    d = xs.shape[1]
    dff = w1.shape[3]
    r, tf = MOE_ROWS, TF_MOE
    nf = dff // tf
    n_blocks = block_e.shape[0]

    def f_eff(i, f, act):
        return jnp.where(act[i] > 0, f, nf - 1)

    grid_spec = pltpu.PrefetchScalarGridSpec(
        num_scalar_prefetch=2,
        grid=(n_blocks, nf),
        in_specs=[pl.BlockSpec((r, d), lambda i, f, be, act: (i, 0)),
                  pl.BlockSpec((1, 1, d, tf), lambda i, f, be, act: (li, be[i], 0, f_eff(i, f, act))),
                  pl.BlockSpec((1, 1, d, tf), lambda i, f, be, act: (li, be[i], 0, f_eff(i, f, act))),
                  pl.BlockSpec((1, 1, tf, d), lambda i, f, be, act: (li, be[i], f_eff(i, f, act), 0))],
        out_specs=pl.BlockSpec((r, d), lambda i, f, be, act: (i, 0)),
        scratch_shapes=[pltpu.VMEM((r, d), BF16)],
    )
    return pl.pallas_call(
        _experts_body,
        grid_spec=grid_spec,
        out_shape=jax.ShapeDtypeStruct(xs.shape, F32),
        compiler_params=_cparams(("arbitrary", "arbitrary")),
        name="moe_experts",
    )(block_e, active, xs, w1, w3, w2)


def _combine_body(*refs, n_lat, alpha, with_next):
    (dest_ref, dnext_ref, ys_hbm, p_ref, x_ref, g2l_ref, g2c_ref, lng_ref, lnb_ref) = refs[:9]
    if with_next:
        shl_ref, shc_ref, scl_ref, scc_ref, x_out, u_out, ybuf, sem = refs[9:]
    else:
        x_out, ybuf, sem = refs[9:]
    tm = x_ref.shape[1]
    t = pl.program_id(0) * pl.num_programs(1) + pl.program_id(1)
    n_tiles = pl.num_programs(0) * pl.num_programs(1)
    slot = t % 2

    def gather_tile(table_ref, to_slot):
        def issue(j, c):
            for k in range(TOP_K):
                pltpu.make_async_copy(ys_hbm.at[pl.ds(table_ref[0, 0, k * tm + j], 1)],
                                      ybuf.at[to_slot, k, pl.ds(j, 1)], sem.at[to_slot]).start()
            return c

        lax.fori_loop(0, tm, issue, 0, unroll=8)

    @pl.when(t == 0)
    def _():
        gather_tile(dest_ref, 0)

    @pl.when(t + 1 < n_tiles)
    def _():
        gather_tile(dnext_ref, 1 - slot)

    for k in range(TOP_K):
        pltpu.make_async_copy(ys_hbm.at[pl.ds(0, tm)], ybuf.at[slot, k], sem.at[slot]).wait()

    is_ctx = _is_ctx_rows(pl.program_id(1), tm, n_lat)
    p = p_ref[0]
    f = p[:, 2:3] * ybuf[slot, 0] + p[:, 3:4] * ybuf[slot, 1]
    xn = _finish_rows(f, x_ref[0], _tile_mod(g2l_ref, g2c_ref, is_ctx), lng_ref[...], lnb_ref[...], alpha)
    x_out[0] = xn
    if with_next:
        u = _ln_plain(xn) * (1.0 + _tile_mod(scl_ref, scc_ref, is_ctx)) + _tile_mod(shl_ref, shc_ref, is_ctx)
        u_out[0] = u.astype(u_out.dtype)


def _combine(ys, dest_tiles, route, x_all, g2, ln_g, ln_b, n_lat, alpha, n_tok_rows, nxt=None):
    b, s, d = x_all.shape
    tm = TM_MOE
    nt = n_tok_rows // tm
    with_next = nxt is not None

    def tok(w):
        return pl.BlockSpec((1, tm, w), lambda i, j: (i, j, 0))

    vec = pl.BlockSpec((1, d), lambda i, j: (0, 0))
    last_tile = b * nt - 1
    in_specs = [pl.BlockSpec((1, 1, TOP_K * tm), lambda i, j: (i * nt + j, 0, 0), memory_space=pltpu.SMEM),
                pl.BlockSpec((1, 1, TOP_K * tm), lambda i, j: (jnp.minimum(i * nt + j + 1, last_tile), 0, 0),
                             memory_space=pltpu.SMEM),
                pl.BlockSpec(memory_space=pl.ANY), tok(LANES), tok(d)] + _row_specs(d, b) + [vec, vec]
    args = [dest_tiles, dest_tiles, ys, route, x_all, g2, g2, ln_g, ln_b]
    out_specs = [tok(d)]
    out_shape = [jax.ShapeDtypeStruct((b, n_tok_rows, d), F32)]
    if with_next:
        in_specs += _row_specs(d, b) * 2
        args += [nxt[0], nxt[0], nxt[1], nxt[1]]
        out_specs.append(tok(d))
        out_shape.append(jax.ShapeDtypeStruct((b, n_tok_rows, d), BF16))
    res = pl.pallas_call(
        functools.partial(_combine_body, n_lat=n_lat, alpha=alpha, with_next=with_next),
        grid=(b, nt),
        in_specs=in_specs,
        out_specs=out_specs,
        out_shape=out_shape,
        scratch_shapes=[pltpu.VMEM((2, TOP_K, tm, d), F32), pltpu.SemaphoreType.DMA((2,))],
        compiler_params=_cparams(("arbitrary", "arbitrary")),
        name="moe_combine",
    )(*args)
    return res if with_next else (res[0], None)


def _block_diag(pool_w):
    g, c, _ = pool_w.shape
    out = jnp.zeros((g * c, g * c), pool_w.dtype)
    for i in range(g):
        out = out.at[i * c:(i + 1) * c, i * c:(i + 1) * c].set(pool_w[i])
    return out


def kernel(x, c, ctx, c_ctx, w_mod, b_mod, w_in, ml_gate_bias, ml_conv_w, ml_norm_w, pool_w, pool_scale,
           na_rpb, w_branch_pool, w_branch_na, w_branch_ml, w_out, ln1_g, ln1_b, ln2_g, ln2_b,
           ffn_w1, ffn_w3, ffn_w2, moe_router_w, moe_router_b, moe_w1, moe_w3, moe_w2):
    b, n_lat, d = x.shape
    n_ctx = ctx.shape[1]
    s = n_lat + n_ctx
    depth = w_in.shape[0]
    alpha = (2 * depth) ** 0.25
    off_g = POOL_WIDTH + 3 * NA_WIDTH + 4 * ML_WIDTH
    n_gate_cols = 4 * ML_HEADS

    x_all = jnp.concatenate([x, ctx], axis=1)
    mod = _mod_vectors(c, c_ctx, w_mod, b_mod)

    def mod_part(layer, k):
        return mod[layer, :, k * d:(k + 1) * d].reshape(MOD_ROWS, 1, d)

    rope = _rope_tables(n_lat)
    u = _ln_mod(x_all, mod_part(0, 0), mod_part(0, 1), n_lat)

    for layer in range(depth):
        last = layer == depth - 1
        w_l = w_in[layer]
        w_main = w_l[:, :off_g].astype(BF16)
        w_g = jnp.zeros((d, LANES), BF16).at[:, :n_gate_cols].set(w_l[:, off_g:off_g + n_gate_cols].astype(BF16))
        w_gate = (0.5 * w_l[:, off_g + n_gate_cols:]).astype(BF16)
        pool_in, na_qkv, mlqk, mlv, mlo, mlg = _proj_main(u, w_main, w_g)

        a = _pool(pool_in, _block_diag(pool_w[layer]).astype(BF16), pool_scale[layer].reshape(1, -1), n_lat)
        n = _na(na_qkv, _na_bias_classes(na_rpb[layer]), n_lat)
        k_ml, qt_ml, vt_ml = _ml_prep(mlqk, mlv, ml_conv_w[layer], rope, n_lat)
        gcol, grow = _ml_gates(mlg, ml_gate_bias[layer])
        m = _ml_scan(k_ml, qt_ml, vt_ml, gcol, grow, mlo, ml_norm_w[layer].reshape(1, -1), n_lat)

        is_moe = layer % 2 == 1
        i = layer // 2
        router = None
        if is_moe:
            rw_f = jnp.zeros((d, LANES), F32).at[:, :N_EXPERTS].set(moe_router_w[i])
            rw_hi = rw_f.astype(BF16)
            rw = jnp.concatenate([rw_hi, (rw_f - rw_hi.astype(F32)).astype(BF16)], axis=1)
            rb = jnp.zeros((1, LANES), F32).at[0, :N_EXPERTS].set(moe_router_b[i])
            router = (rw, rb)
        n_rows_tok = n_lat if last else s
        merged = _merge(a, n, m, u, x_all,
                        (0.5 * w_branch_pool[layer]).astype(BF16), (0.5 * w_branch_na[layer]).astype(BF16),
                        (0.5 * w_branch_ml[layer]).astype(BF16), w_gate, w_out[layer].astype(BF16),
                        mod_part(layer, 2), mod_part(layer, 3), mod_part(layer, 4),
                        ln1_g[layer].reshape(1, d), ln1_b[layer].reshape(1, d), n_lat, alpha, router, n_rows_tok)
        nxt = None if last else (mod_part(layer + 1, 0), mod_part(layer + 1, 1))
        g2 = mod_part(layer, 5)
        lng, lnb = ln2_g[layer].reshape(1, d), ln2_b[layer].reshape(1, d)
        if not is_moe:
            x_mid, tok = merged
            x_all, u = _ffn(tok, x_mid, ffn_w1[i].astype(BF16), ffn_w3[i].astype(BF16), ffn_w2[i].astype(BF16),
                            g2, lng, lnb, n_lat, alpha, nxt)
        else:
            x_mid, tok, route, counts = merged
            x_all, u = _moe_layer(tok, route, counts, x_mid, moe_w1, moe_w3, moe_w2, i, g2, lng, lnb,
                                  n_lat, alpha, n_rows_tok, nxt)
    return x_all[:, :n_lat]


def _moe_layer(tok, route, counts, x_mid, w1, w3, w2, li, g2, lng, lnb, n_lat, alpha, n_rows_tok, nxt):
    dest_tiles, zero_blk, zero_on, block_e, active = _moe_tables(route, counts, n_rows_tok)
    xs = _moe_dispatch(tok, zero_blk, zero_on, dest_tiles, block_e.shape[0], n_rows_tok)
    ys = _moe_experts(xs, block_e, active, w1, w3, w2, li)
    return _combine(ys, dest_tiles, route, x_mid, g2, lng, lnb, n_lat, alpha, n_rows_tok, nxt)
```

```python
import functools

import numpy as np
import jax
import jax.numpy as jnp
from jax import lax
from jax.experimental import pallas as pl
from jax.experimental.pallas import tpu as pltpu

F32 = jnp.float32
BF16 = jnp.bfloat16
HIGHEST = lax.Precision.HIGHEST

GRID_W = 64
POOL_GROUPS = 4
POOL_GROUP_DIM = 64
POOL_WIDTH = POOL_GROUPS * POOL_GROUP_DIM
POOL_HALF_WINDOWS = (1, 2, 4, 8)
NA_HEADS = 4
NA_HEAD_DIM = 64
NA_WIDTH = NA_HEADS * NA_HEAD_DIM
NA_WIN_ROWS = 8
NA_WIN_COLS = 16
ML_HEADS = 4
ML_HEAD_DIM = 128
ML_WIDTH = ML_HEADS * ML_HEAD_DIM
ML_CHUNK = 128
ML_M_INIT = -1e30
ROPE_THETA = 10000.0
N_EXPERTS = 8
TOP_K = 2
N_BRANCHES = 3
LN_EPS = 1e-6

LANES = 128
MOD_ROWS = 16
VMEM_LIMIT = 56 * 1024 * 1024

TM_TOKEN = 768
TM_MERGE = 768
TM_FFN = 384
TF_MOE = 512
MOE_ROWS = 1024
TM_MOE = 256


def _cparams(sem):
    return pltpu.CompilerParams(dimension_semantics=sem, vmem_limit_bytes=VMEM_LIMIT)


def _sigmoid(x):
    return 0.5 + 0.5 * jnp.tanh(0.5 * x)


def _silu(x):
    t = 0.5 * x
    return t + t * jnp.tanh(t)


def _ln_plain(x):
    xc = x - jnp.mean(x, axis=-1, keepdims=True)
    return xc * lax.rsqrt(jnp.mean(xc * xc, axis=-1, keepdims=True) + LN_EPS)


def _is_ctx_rows(tile_idx, tm, n_lat):
    row = tile_idx * tm + lax.broadcasted_iota(jnp.int32, (tm, 1), 0)
    return row >= n_lat


def _tile_mod(lat_ref, ctx_ref, is_ctx):
    return jnp.where(is_ctx, ctx_ref[0], lat_ref[0])


def _mod_body(s_ref, w_ref, b_ref, o_ref):
    s = _silu(s_ref[...])
    o_ref[0] = jnp.dot(s, w_ref[0], precision=HIGHEST, preferred_element_type=F32) + b_ref[0]


def _mod_vectors(c, c_ctx, w_mod, b_mod):
    depth, d, d6 = w_mod.shape
    b = c.shape[0]
    s = jnp.zeros((MOD_ROWS, d), F32).at[:b].set(c).at[b].set(c_ctx)
    tn = 1536
    return pl.pallas_call(
        _mod_body,
        grid=(depth, d6 // tn),
        in_specs=[pl.BlockSpec((MOD_ROWS, d), lambda l, j: (0, 0)),
                  pl.BlockSpec((1, d, tn), lambda l, j: (l, 0, j)),
                  pl.BlockSpec((1, 1, tn), lambda l, j: (l, 0, j))],
        out_specs=pl.BlockSpec((1, MOD_ROWS, tn), lambda l, j: (l, 0, j)),
        out_shape=jax.ShapeDtypeStruct((depth, MOD_ROWS, d6), F32),
        compiler_params=_cparams(("arbitrary", "arbitrary")),
        name="mod_vectors",
    )(s, w_mod, b_mod.reshape(depth, 1, d6))


def _ln_mod_body(x_ref, shl_ref, shc_ref, scl_ref, scc_ref, u_ref, *, n_lat):
    tm = x_ref.shape[1]
    is_ctx = _is_ctx_rows(pl.program_id(1), tm, n_lat)
    sh = _tile_mod(shl_ref, shc_ref, is_ctx)
    sc = _tile_mod(scl_ref, scc_ref, is_ctx)
    u_ref[0] = (_ln_plain(x_ref[0]) * (1.0 + sc) + sh).astype(u_ref.dtype)


def _row_specs(d, nb):
    return [pl.BlockSpec((1, 1, d), lambda b, j: (b, 0, 0)),
            pl.BlockSpec((1, 1, d), lambda b, j: (nb, 0, 0))]


def _ln_mod(x_all, shift, scale, n_lat):
    b, s, d = x_all.shape
    tm = TM_TOKEN
    return pl.pallas_call(
        functools.partial(_ln_mod_body, n_lat=n_lat),
        grid=(b, s // tm),
        in_specs=[pl.BlockSpec((1, tm, d), lambda i, j: (i, j, 0))] + _row_specs(d, b) + _row_specs(d, b),
        out_specs=pl.BlockSpec((1, tm, d), lambda i, j: (i, j, 0)),
        out_shape=jax.ShapeDtypeStruct((b, s, d), BF16),
        compiler_params=_cparams(("arbitrary", "arbitrary")),
        name="ln_mod",
    )(x_all, shift, shift, scale, scale)


def _proj_main_body(u_ref, w_ref, wg_ref, pool_ref, na_ref, mlqk_ref, mlv_ref, mlo_ref, mlg_ref):
    u = u_ref[0]

    def seg(lo, hi):
        return jnp.dot(u, w_ref[:, lo:hi], preferred_element_type=F32)

    off_na = POOL_WIDTH
    off_ml = off_na + 3 * NA_WIDTH
    off_mlv = off_ml + 2 * ML_WIDTH
    off_mlo = off_ml + 3 * ML_WIDTH
    pool_ref[0] = seg(0, off_na)
    na_ref[0, :, :NA_WIDTH] = (seg(off_na, off_na + NA_WIDTH) * NA_HEAD_DIM ** -0.5).astype(na_ref.dtype)
    na_ref[0, :, NA_WIDTH:] = seg(off_na + NA_WIDTH, off_ml).astype(na_ref.dtype)
    mlqk_ref[0] = seg(off_ml, off_mlv)
    mlv_ref[0] = seg(off_mlv, off_mlo).astype(mlv_ref.dtype)
    mlo_ref[0] = seg(off_mlo, off_mlo + ML_WIDTH).astype(mlo_ref.dtype)
    mlg_ref[0] = jnp.dot(u, wg_ref[...], preferred_element_type=F32)


def _proj_main(u, w_main, w_g):
    b, s, d = u.shape
    tm = TM_TOKEN
    widths = (POOL_WIDTH, 3 * NA_WIDTH, 2 * ML_WIDTH, ML_WIDTH, ML_WIDTH, LANES)
    dtypes = (F32, BF16, F32, BF16, BF16, F32)
    return pl.pallas_call(
        _proj_main_body,
        grid=(b, s // tm),
        in_specs=[pl.BlockSpec((1, tm, d), lambda i, j: (i, j, 0)),
                  pl.BlockSpec(w_main.shape, lambda i, j: (0, 0)),
                  pl.BlockSpec(w_g.shape, lambda i, j: (0, 0))],
        out_specs=[pl.BlockSpec((1, tm, w), lambda i, j: (i, j, 0)) for w in widths],
        out_shape=[jax.ShapeDtypeStruct((b, s, w), dt) for w, dt in zip(widths, dtypes)],
        compiler_params=_cparams(("arbitrary", "arbitrary")),
        name="proj_main",
    )(u, w_main, w_g)


def _pool_segment(a, w_bd, scale):
    n = a.shape[0]
    t = lax.broadcasted_iota(jnp.int32, a.shape, 0)
    grp = lax.broadcasted_iota(jnp.int32, a.shape, 1) // POOL_GROUP_DIM

    def up(x, k):
        return jnp.where(t < n - k, pltpu.roll(x, n - k, 0), 0.0)

    def down(x, k):
        return jnp.where(t >= k, pltpu.roll(x, k, 0), 0.0)

    ahead = a
    behind = down(a, 1)
    wsum = jnp.zeros_like(a)
    half = jnp.zeros_like(t)
    for g, h in enumerate(POOL_HALF_WINDOWS):
        if g > 0:
            ahead = ahead + up(ahead, h // 2)
            behind = behind + down(behind, h // 2)
        wsum = jnp.where(grp == g, ahead + behind, wsum)
        half = jnp.where(grp == g, h, half)
    cnt = jnp.minimum(t + half, n) - jnp.maximum(t - half, 0)
    mean = wsum / cnt.astype(F32)
    out = jnp.dot((mean - a).astype(BF16), w_bd, preferred_element_type=F32)
    return out * scale


def _pool_body(a_ref, w_ref, sc_ref, o_ref, *, n_lat):
    w_bd = w_ref[...]
    scale = sc_ref[...]
    s = a_ref.shape[1]
    o_ref[0, :n_lat] = _pool_segment(a_ref[0, :n_lat], w_bd, scale).astype(o_ref.dtype)
    o_ref[0, n_lat:] = _pool_segment(a_ref[0, n_lat:s], w_bd, scale).astype(o_ref.dtype)


def _pool(pool_in, w_bd, scale, n_lat):
    b, s, w = pool_in.shape
    return pl.pallas_call(
        functools.partial(_pool_body, n_lat=n_lat),
        grid=(b,),
        in_specs=[pl.BlockSpec((1, s, w), lambda i: (i, 0, 0)),
                  pl.BlockSpec((w, w), lambda i: (0, 0)),
                  pl.BlockSpec((1, w), lambda i: (0, 0))],
        out_specs=pl.BlockSpec((1, s, w), lambda i: (i, 0, 0)),
        out_shape=jax.ShapeDtypeStruct((b, s, w), BF16),
        compiler_params=_cparams(("arbitrary",)),
        name="pool",
    )(pool_in, w_bd, scale)


NA_ROWS_PER_STEP = 4


def _na_body(q_ref, k_ref, v_ref, bias_ref, o_ref, *, n_lat, n_rows):
    step = pl.program_id(1)
    s_tot = k_ref.shape[1]
    kc = k_ref[0, n_lat:s_tot, :]
    vc = v_ref[0, n_lat:s_tot, :]
    nt = (((1,), (1,)), ((), ()))
    head = lax.broadcasted_iota(jnp.int32, (GRID_W, NA_WIDTH), 1) // NA_HEAD_DIM
    wr = NA_WIN_ROWS

    def one_row(g, local):
        rows = slice(g * GRID_W, (g + 1) * GRID_W)
        q = q_ref[0, rows, :]
        zero = jnp.zeros_like(q)
        q4 = jnp.concatenate([jnp.where(head == h, q, zero) for h in range(NA_HEADS)], axis=0)
        s_ctx = lax.dot_general(q4, kc, nt, preferred_element_type=F32)
        m = jnp.max(s_ctx, axis=-1, keepdims=True)
        if local:
            r = step * NA_ROWS_PER_STEP + g
            rs = jnp.clip(r - wr // 2, 0, n_rows - wr)
            start = pl.multiple_of(rs * GRID_W, GRID_W)
            kw = k_ref[0, pl.ds(start, wr * GRID_W), :]
            vw = v_ref[0, pl.ds(start, wr * GRID_W), :]
            s_loc = lax.dot_general(q4, kw, nt, preferred_element_type=F32) + bias_ref[rs - r + wr - 1]
            m = jnp.maximum(m, jnp.max(s_loc, axis=-1, keepdims=True))
            p_loc = jnp.exp(s_loc - m)
        p_ctx = jnp.exp(s_ctx - m)
        denom = jnp.sum(p_ctx, axis=-1, keepdims=True)
        o4 = jnp.dot(p_ctx.astype(BF16), vc, preferred_element_type=F32)
        if local:
            denom = denom + jnp.sum(p_loc, axis=-1, keepdims=True)
            o4 = o4 + jnp.dot(p_loc.astype(BF16), vw, preferred_element_type=F32)
        o4 = o4 / denom
        out = jnp.zeros((GRID_W, NA_WIDTH), F32)
        for h in range(NA_HEADS):
            out = out + jnp.where(head == h, o4[h * GRID_W:(h + 1) * GRID_W], 0.0)
        o_ref[0, rows, :] = out.astype(o_ref.dtype)

    @pl.when(step * NA_ROWS_PER_STEP < n_rows)
    def _():
        for g in range(NA_ROWS_PER_STEP):
            one_row(g, True)

    @pl.when(step * NA_ROWS_PER_STEP >= n_rows)
    def _():
        for g in range(NA_ROWS_PER_STEP):
            one_row(g, False)


def _na_bias_classes(rpb):
    col = np.arange(GRID_W)
    col_start = np.clip(col - NA_WIN_COLS // 2, 0, GRID_W - NA_WIN_COLS)
    in_window = (col[None, :] >= col_start[:, None]) & (col[None, :] < col_start[:, None] + NA_WIN_COLS)
    col_off = np.clip(col[None, :] - col[:, None], -(NA_WIN_COLS - 1), NA_WIN_COLS - 1) + NA_WIN_COLS - 1
    n_off = 2 * NA_WIN_COLS - 1
    pick = (col_off[None, :, :] == np.arange(n_off)[:, None, None]).astype(np.float32)
    by_col = jnp.einsum('hro,oqk->hrqk', rpb.astype(F32), jnp.asarray(pick), precision=HIGHEST)
    by_col = jnp.where(in_window[None, None], by_col, -jnp.inf)
    b = jnp.stack([by_col[:, o:o + NA_WIN_ROWS] for o in range(NA_WIN_ROWS)], axis=0)
    b = b.transpose(0, 1, 3, 2, 4)
    return b.reshape(NA_WIN_ROWS, NA_HEADS * GRID_W, NA_WIN_ROWS * GRID_W)


def _na(na_qkv, bias_cls, n_lat):
    b, s, _ = na_qkv.shape
    n_rows = n_lat // GRID_W
    tq = NA_ROWS_PER_STEP * GRID_W

    return pl.pallas_call(
        functools.partial(_na_body, n_lat=n_lat, n_rows=n_rows),
        grid=(b, s // tq),
        in_specs=[pl.BlockSpec((1, tq, NA_WIDTH), lambda i, r: (i, r, 0)),
                  pl.BlockSpec((1, s, NA_WIDTH), lambda i, r: (i, 0, 1)),
                  pl.BlockSpec((1, s, NA_WIDTH), lambda i, r: (i, 0, 2)),
                  pl.BlockSpec(bias_cls.shape, lambda i, r: (0, 0, 0))],
        out_specs=pl.BlockSpec((1, tq, NA_WIDTH), lambda i, r: (i, r, 0)),
        out_shape=jax.ShapeDtypeStruct((b, s, NA_WIDTH), BF16),
        compiler_params=_cparams(("arbitrary", "arbitrary")),
        name="nbr_attention",
    )(na_qkv, na_qkv, na_qkv, bias_cls)


def _rope_tables(n_lat):
    half = ML_HEAD_DIM // 2
    nf = half // 2
    inv = ROPE_THETA ** (-np.arange(nf, dtype=np.float64) / nf)
    pos = np.arange(n_lat)
    ang = np.concatenate([(pos // GRID_W)[:, None] * inv[None, :]] * 2
                         + [(pos % GRID_W)[:, None] * inv[None, :]] * 2, axis=1)
    lane = np.arange(ML_HEAD_DIM)
    second = (lane % half) >= nf
    cos = np.cos(ang)
    sin = np.sin(ang)
    sin_from_lo = np.where(second[None, :], sin, 0.0)
    sin_from_hi = np.where(second[None, :], 0.0, -sin)
    return (jnp.asarray(cos, F32), jnp.asarray(sin_from_lo, F32), jnp.asarray(sin_from_hi, F32))


def _conv_silu(x, w):
    n = x.shape[0]
    t = lax.broadcasted_iota(jnp.int32, x.shape, 0)
    prev = jnp.where(t >= 1, pltpu.roll(x, 1, 0), 0.0)
    nxt = jnp.where(t < n - 1, pltpu.roll(x, n - 1, 0), 0.0)
    return _silu(w[0:1] * prev + w[1:2] * x + w[2:3] * nxt)


def _rope(x, cos, sin_lo, sin_hi):
    nf = ML_HEAD_DIM // 4
    return x * cos + pltpu.roll(x, nf, 1) * sin_lo + pltpu.roll(x, ML_HEAD_DIM - nf, 1) * sin_hi


def _ml_prep_body(xq_ref, xk_ref, v_ref, wq_ref, wk_ref, cos_ref, slo_ref, shi_ref, k_ref, qt_ref, vt_ref, *, n_lat):
    s = xq_ref.shape[1]
    wq = wq_ref[...]
    wk = wk_ref[...]
    cos, slo, shi = cos_ref[...], slo_ref[...], shi_ref[...]
    kscale = ML_HEAD_DIM ** -0.5
    for lo, hi, rotary in ((0, n_lat, True), (n_lat, s, False)):
        q = _conv_silu(xq_ref[0, lo:hi], wq)
        k = _conv_silu(xk_ref[0, lo:hi], wk) * kscale
        if rotary:
            q = _rope(q, cos, slo, shi)
            k = _rope(k, cos, slo, shi)
        k_ref[0, lo:hi] = k.astype(k_ref.dtype)
        v = v_ref[0, lo:hi].astype(F32)
        for j in range((hi - lo) // ML_CHUNK):
            rows = slice(j * ML_CHUNK, (j + 1) * ML_CHUNK)
            qt_ref[0, 0, lo // ML_CHUNK + j] = q[rows].T.astype(qt_ref.dtype)
            vt_ref[0, 0, lo // ML_CHUNK + j] = v[rows].T.astype(vt_ref.dtype)


def _ml_prep(mlqk, mlv, conv_w, rope, n_lat):
    b, s, _ = mlqk.shape
    hd = ML_HEAD_DIM
    nc = s // ML_CHUNK
    tab = pl.BlockSpec((n_lat, hd), lambda i, h: (0, 0))
    chunks = pl.BlockSpec((1, 1, nc, hd, ML_CHUNK), lambda i, h: (i, h, 0, 0, 0))
    chunk_shape = jax.ShapeDtypeStruct((b, ML_HEADS, nc, hd, ML_CHUNK), BF16)
    return pl.pallas_call(
        functools.partial(_ml_prep_body, n_lat=n_lat),
        grid=(b, ML_HEADS),
        in_specs=[pl.BlockSpec((1, s, hd), lambda i, h: (i, 0, h)),
                  pl.BlockSpec((1, s, hd), lambda i, h: (i, 0, ML_HEADS + h)),
                  pl.BlockSpec((1, s, hd), lambda i, h: (i, 0, h)),
                  pl.BlockSpec((3, hd), lambda i, h: (0, h)),
                  pl.BlockSpec((3, hd), lambda i, h: (0, ML_HEADS + h)),
                  tab, tab, tab],
        out_specs=[pl.BlockSpec((1, s, hd), lambda i, h: (i, 0, h)), chunks, chunks],
        out_shape=[jax.ShapeDtypeStruct((b, s, ML_WIDTH), BF16), chunk_shape, chunk_shape],
        compiler_params=_cparams(("arbitrary", "arbitrary")),
        name="mlstm_prep",
    )(mlqk, mlqk, mlv, conv_w, conv_w, *rope)


def _log_sigmoid(x):
    return jnp.minimum(x, 0.0) - jnp.log1p(jnp.exp(-jnp.abs(x)))


def _ml_gates_body(gc_ref, gr_ref, bc_ref, br_ref, oc_ref, or_ref):
    nc = gr_ref.shape[1]
    c = ML_CHUNK
    ii = lax.broadcasted_iota(jnp.int32, (c, c), 0)
    jj = lax.broadcasted_iota(jnp.int32, (c, c), 1)
    lower = (ii >= jj).astype(F32)
    upper = (ii <= jj).astype(F32)
    lane_kind = lax.broadcasted_iota(jnp.int32, (c, LANES), 1) // ML_HEADS
    row_kind = lax.broadcasted_iota(jnp.int32, (4 * ML_HEADS, c), 0) // ML_HEADS
    for j in range(nc):
        g = gc_ref[0, j * c:(j + 1) * c, :] + bc_ref[...]
        lf = _log_sigmoid(g)
        pre = jnp.dot(lower, lf, precision=HIGHEST, preferred_element_type=F32)
        suf = jnp.dot(upper, lf, precision=HIGHEST, preferred_element_type=F32)
        oc_ref[0, j * c:(j + 1) * c, :] = jnp.where(lane_kind == 1, pre, jnp.where(lane_kind == 3, suf, g))
        gr = gr_ref[0, j] + br_ref[...]
        lfr = _log_sigmoid(gr)
        pre_r = jnp.dot(lfr, upper, precision=HIGHEST, preferred_element_type=F32)
        suf_r = jnp.dot(lfr, lower, precision=HIGHEST, preferred_element_type=F32)
        or_ref[0, j] = jnp.where(row_kind == 1, pre_r, jnp.where(row_kind == 3, suf_r, gr))


def _ml_gates(mlg, gate_bias):
    b, s, _ = mlg.shape
    nc = s // ML_CHUNK
    ng = 4 * ML_HEADS
    g_rows = mlg[:, :, :ng].reshape(b, nc, ML_CHUNK, ng).transpose(0, 1, 3, 2)
    bias_c = jnp.zeros((1, LANES), F32).at[0, :ng].set(gate_bias)
    bias_r = gate_bias.reshape(ng, 1)
    return pl.pallas_call(
        _ml_gates_body,
        grid=(b,),
        in_specs=[pl.BlockSpec((1, s, LANES), lambda i: (i, 0, 0)),
                  pl.BlockSpec((1, nc, ng, ML_CHUNK), lambda i: (i, 0, 0, 0)),
                  pl.BlockSpec((1, LANES), lambda i: (0, 0)),
                  pl.BlockSpec((ng, 1), lambda i: (0, 0))],
        out_specs=[pl.BlockSpec((1, s, LANES), lambda i: (i, 0, 0)),
                   pl.BlockSpec((1, nc, ng, ML_CHUNK), lambda i: (i, 0, 0, 0))],
        out_shape=[jax.ShapeDtypeStruct((b, s, LANES), F32),
                   jax.ShapeDtypeStruct((b, nc, ng, ML_CHUNK), F32)],
        compiler_params=_cparams(("arbitrary",)),
        name="mlstm_gates",
    )(mlg, g_rows, bias_c, bias_r)


def _ml_step(k, qt, vt_aug, a_c, b_r, state_t, m, *, forward):
    c = ML_CHUNK
    d = ML_HEAD_DIM
    si = lax.broadcasted_iota(jnp.int32, (c, c), 0)
    ti = lax.broadcasted_iota(jnp.int32, (c, c), 1)
    allowed = (si <= ti) if forward else (si >= ti)
    log_d = jnp.where(allowed, b_r + a_c, -jnp.inf)
    inter = b_r + m
    m_t = jnp.maximum(jnp.max(log_d, axis=0, keepdims=True), inter)
    kq = jnp.dot(k, qt, preferred_element_type=F32)
    s_t = (kq * jnp.exp(log_d - m_t)).astype(BF16)
    w_inter = jnp.exp(inter - m_t)
    tot = (jnp.dot(vt_aug, s_t, preferred_element_type=F32)
           + w_inter * jnp.dot(state_t.astype(BF16), qt, preferred_element_type=F32))
    num = tot[:d]
    den = tot[d:d + 1]
    h_t = num / jnp.maximum(jnp.abs(den), jnp.exp(-m_t))
    b_end = b_r[:, c - 1:c] if forward else b_r[:, 0:1]
    log_w = b_end + a_c
    m_new = jnp.maximum(b_end + m, jnp.max(log_w, axis=0, keepdims=True))
    w_c = jnp.exp(log_w - m_new)
    decay = jnp.exp(b_end + m - m_new)
    kw = (k.astype(F32) * w_c).astype(BF16)
    state_new = decay * state_t + jnp.dot(vt_aug, kw, preferred_element_type=F32)
    return h_t, state_new, m_new


ML_HEADS_PER_STEP = 4


def _ml_scan_body(k_ref, qt_ref, vt_ref, gc_ref, gr_ref, o_ref, nw_ref, out_ref, hsum_ref, *, n_lat):
    c = ML_CHUNK
    d = ML_HEAD_DIM
    s = k_ref.shape[1]
    heads = qt_ref.shape[1]
    nc_lat = n_lat // c
    nc_all = s // c
    ones_row = (lax.broadcasted_iota(jnp.int32, (d, c), 0) == 0).astype(BF16)

    def rows(j):
        return pl.ds(j * c if isinstance(j, int) else pl.multiple_of(j * c, c), c)

    def run(j, h, state, m, forward):
        vt_aug = jnp.concatenate([vt_ref[0, h, j], ones_row], axis=0)
        gc = gc_ref[0, h, rows(j), :]
        gr = gr_ref[0, h, j]
        k0 = 0 if forward else 2
        a_c = gc[:, k0:k0 + 1] - gc[:, k0 + 1:k0 + 2]
        return _ml_step(k_ref[0, rows(j), h * d:(h + 1) * d], qt_ref[0, h, j], vt_aug, a_c, gr[k0 + 1:k0 + 2],
                        state, m, forward=forward)

    def finalize(j, h, hsum_t):
        cols = slice(h * d, (h + 1) * d)
        hc = hsum_t - jnp.mean(hsum_t, axis=0, keepdims=True)
        hn = (hc * lax.rsqrt(jnp.mean(hc * hc, axis=0, keepdims=True) + LN_EPS)).T
        gate = _sigmoid(o_ref[0, rows(j), cols].astype(F32))
        out_ref[0, rows(j), cols] = (gate * hn * nw_ref[:, cols]).astype(out_ref.dtype)

    zero_state = jnp.zeros((2 * d, d), F32)
    m0 = jnp.full((1, 1), ML_M_INIT, F32)
    carry = []
    for h in range(heads):
        sf, mf = zero_state, m0
        hc_f = []
        for j in range(nc_lat, nc_all):
            out, sf, mf = run(j, h, sf, mf, True)
            hc_f.append(out)
        sb, mb = zero_state, m0
        hc_b = {}
        for j in range(nc_all - 1, nc_lat - 1, -1):
            out, sb, mb = run(j, h, sb, mb, False)
            hc_b[j] = out
        for idx, j in enumerate(range(nc_lat, nc_all)):
            finalize(j, h, hc_f[idx] + hc_b[j])
        carry += [sf, mf, sb, mb]

    half = nc_lat // 2

    def first_half(i, carry):
        jb = nc_lat - 1 - i
        new = []
        for h in range(heads):
            sf, mf, sb, mb = carry[4 * h:4 * h + 4]
            hf, sf, mf = run(i, h, sf, mf, True)
            hb, sb, mb = run(jb, h, sb, mb, False)
            hsum_ref[h, i] = hf
            hsum_ref[h, jb] = hb
            new += [sf, mf, sb, mb]
        return tuple(new)

    carry = lax.fori_loop(0, half, first_half, tuple(carry))

    def second_half(i, carry):
        jb = nc_lat - 1 - i
        new = []
        for h in range(heads):
            sf, mf, sb, mb = carry[4 * h:4 * h + 4]
            hf, sf, mf = run(i, h, sf, mf, True)
            hb, sb, mb = run(jb, h, sb, mb, False)
            finalize(i, h, hf + hsum_ref[h, i])
            finalize(jb, h, hb + hsum_ref[h, jb])
            new += [sf, mf, sb, mb]
        return tuple(new)

    lax.fori_loop(half, nc_lat, second_half, carry)


def _ml_scan(k, qt, vt, gcol, grow, mlo, norm_w, n_lat):
    b, s, _ = k.shape
    hd = ML_HEAD_DIM
    hb = ML_HEADS_PER_STEP
    nc = s // ML_CHUNK
    ng = 4 * ML_HEADS
    gc_h = gcol[:, :, :ng].reshape(b, s, 4, ML_HEADS).transpose(0, 3, 1, 2)
    gr_h = grow.reshape(b, nc, 4, ML_HEADS, ML_CHUNK).transpose(0, 3, 1, 2, 4)
    tok = pl.BlockSpec((1, s, hb * hd), lambda i, h: (i, 0, h))
    chunks = pl.BlockSpec((1, hb, nc, hd, ML_CHUNK), lambda i, h: (i, h, 0, 0, 0))
    return pl.pallas_call(
        functools.partial(_ml_scan_body, n_lat=n_lat),
        grid=(b, ML_HEADS // hb),
        in_specs=[tok, chunks, chunks,
                  pl.BlockSpec((1, hb, s, 4), lambda i, h: (i, h, 0, 0)),
                  pl.BlockSpec((1, hb, nc, 4, ML_CHUNK), lambda i, h: (i, h, 0, 0, 0)),
                  tok,
                  pl.BlockSpec((1, hb * hd), lambda i, h: (0, h))],
        out_specs=tok,
        out_shape=jax.ShapeDtypeStruct((b, s, ML_WIDTH), BF16),
        scratch_shapes=[pltpu.VMEM((hb, n_lat // ML_CHUNK, hd, ML_CHUNK), F32)],
        compiler_params=_cparams(("arbitrary", "arbitrary")),
        name="mlstm_scan",
    )(k, qt, vt, gc_h, gr_h, mlo, norm_w)


def _top2(logits, n_valid):
    lane = lax.broadcasted_iota(jnp.int32, logits.shape, 1).astype(F32)
    neg = jnp.float32(-jnp.inf)
    l = jnp.where(lane < n_valid, logits, neg)
    m1 = jnp.max(l, axis=-1, keepdims=True)
    e1 = jnp.min(jnp.where(l == m1, lane, float(LANES)), axis=-1, keepdims=True)
    l2 = jnp.where(lane == e1, neg, l)
    m2 = jnp.max(l2, axis=-1, keepdims=True)
    e2 = jnp.min(jnp.where(l2 == m2, lane, float(LANES)), axis=-1, keepdims=True)
    x2 = jnp.exp(m2 - m1)
    p1 = 1.0 / (1.0 + x2)
    p2 = x2 / (1.0 + x2)
    return jnp.where(lane == 0, e1, jnp.where(lane == 1, e2, jnp.where(lane == 2, p1, jnp.where(lane == 3, p2, 0.0))))


def _route_ranks(route, valid, cnt_ref):
    tm = route.shape[0]
    lane = lax.broadcasted_iota(jnp.int32, route.shape, 1).astype(F32)
    pick1 = jnp.where(valid, (lane == route[:, 0:1]).astype(F32), 0.0)
    pick2 = jnp.where(valid, (lane == route[:, 1:2]).astype(F32), 0.0)
    both = pick1 + pick2
    before = (lax.broadcasted_iota(jnp.int32, (tm, tm), 0) > lax.broadcasted_iota(jnp.int32, (tm, tm), 1))
    prefix = jnp.dot(before.astype(BF16), both.astype(BF16), preferred_element_type=F32) + cnt_ref[...]
    rank1 = jnp.sum(prefix * pick1, axis=-1, keepdims=True)
    rank2 = jnp.sum(prefix * pick2, axis=-1, keepdims=True)
    cnt_ref[...] += jnp.sum(both, axis=0, keepdims=True)
    return jnp.where(lane == 4, rank1, jnp.where(lane == 5, rank2, route))


def _merge_body(*refs, n_lat, alpha, moe, n_rows_tok):
    (a_ref, n_ref, m_ref, u_ref, x_ref, wbp_ref, wbn_ref, wbm_ref, wg_ref, wo_ref,
     g1l_ref, g1c_ref, shl_ref, shc_ref, scl_ref, scc_ref, lng_ref, lnb_ref) = refs[:18]
    if moe:
        rw_ref, rb_ref, x_out, tok_out, route_out, cnt_out, cnt_ref = refs[18:]
    else:
        x_out, tok_out = refs[18:]
    tm = x_ref.shape[1]
    d = x_ref.shape[2]
    is_ctx = _is_ctx_rows(pl.program_id(1), tm, n_lat)
    u = u_ref[0]
    y = jnp.zeros((tm, d), F32)
    for k, (br_ref, w_ref) in enumerate(((a_ref, wbp_ref), (n_ref, wbn_ref), (m_ref, wbm_ref))):
        t = jnp.tanh(jnp.dot(u, wg_ref[:, k * d:(k + 1) * d], preferred_element_type=F32))
        half_v = jnp.dot(br_ref[0], w_ref[...], preferred_element_type=F32)
        y = y + (half_v + half_v * t)
    y = jnp.dot(y.astype(BF16), wo_ref[...], preferred_element_type=F32)
    z = alpha * x_ref[0] + _tile_mod(g1l_ref, g1c_ref, is_ctx) * y
    xn = _ln_plain(z) * lng_ref[...] + lnb_ref[...]
    x_out[0] = xn
    tok = _ln_plain(xn) * (1.0 + _tile_mod(scl_ref, scc_ref, is_ctx)) + _tile_mod(shl_ref, shc_ref, is_ctx)
    tok_out[0] = tok.astype(tok_out.dtype)
    if moe:
        @pl.when((pl.program_id(0) == 0) & (pl.program_id(1) == 0))
        def _():
            cnt_ref[...] = jnp.zeros_like(cnt_ref)

        tok_hi = tok.astype(BF16)
        tok_lo = (tok - tok_hi.astype(F32)).astype(BF16)
        both = jnp.dot(tok_hi, rw_ref[...], preferred_element_type=F32)
        logits = (both[:, :LANES] + both[:, LANES:]
                  + jnp.dot(tok_lo, rw_ref[:, :LANES], preferred_element_type=F32))
        valid = (pl.program_id(1) * tm + lax.broadcasted_iota(jnp.int32, (tm, 1), 0)) < n_rows_tok
        route_out[0] = _route_ranks(_top2(logits + rb_ref[...], N_EXPERTS), valid, cnt_ref)
        cnt_out[...] = cnt_ref[...]


def _merge(a, n, m, u, x_all, w_bp, w_bn, w_bm, w_gate, w_o, g1, sh2, sc2, ln_g, ln_b, n_lat, alpha,
           router=None, n_rows_tok=0):
    b, s, d = x_all.shape
    tm = TM_MERGE
    moe = router is not None

    def tok(w):
        return pl.BlockSpec((1, tm, w), lambda i, j: (i, j, 0))

    def full(arr):
        return pl.BlockSpec(arr.shape, lambda i, j: (0,) * arr.ndim, pipeline_mode=pl.Buffered(1))

    in_specs = ([tok(a.shape[2]), tok(n.shape[2]), tok(m.shape[2]), tok(d), tok(d),
                 full(w_bp), full(w_bn), full(w_bm), full(w_gate), full(w_o)]
                + _row_specs(d, b) * 3 + [full(ln_g), full(ln_b)])
    args = [a, n, m, u, x_all, w_bp, w_bn, w_bm, w_gate, w_o, g1, g1, sh2, sh2, sc2, sc2, ln_g, ln_b]
    out_specs = [tok(d), tok(d)]
    out_shape = [jax.ShapeDtypeStruct((b, s, d), F32), jax.ShapeDtypeStruct((b, s, d), F32 if moe else BF16)]
    scratch = []
    if moe:
        in_specs += [full(router[0]), full(router[1])]
        args += list(router)
        out_specs += [tok(LANES), pl.BlockSpec((1, LANES), lambda i, j: (0, 0))]
        out_shape += [jax.ShapeDtypeStruct((b, s, LANES), F32), jax.ShapeDtypeStruct((1, LANES), F32)]
        scratch = [pltpu.VMEM((1, LANES), F32)]
    return pl.pallas_call(
        functools.partial(_merge_body, n_lat=n_lat, alpha=alpha, moe=moe, n_rows_tok=n_rows_tok),
        grid=(b, s // tm),
        in_specs=in_specs,
        out_specs=out_specs,
        out_shape=out_shape,
        scratch_shapes=scratch,
        compiler_params=_cparams(("arbitrary", "arbitrary")),
        name="merge",
    )(*args)


def _finish_rows(f, x, g2, lng, lnb, alpha):
    return _ln_plain(alpha * x + g2 * f) * lng + lnb


def _ffn_body(*refs, n_lat, alpha, with_next):
    (t_ref, x_ref, w1_ref, w3_ref, w2_ref, g2l_ref, g2c_ref, lng_ref, lnb_ref) = refs[:9]
    if with_next:
        shl_ref, shc_ref, scl_ref, scc_ref, x_out, u_out = refs[9:]
    else:
        (x_out,) = refs[9:]
    t = t_ref[0]
    h1 = jnp.dot(t, w1_ref[...], preferred_element_type=F32)
    h3 = jnp.dot(t, w3_ref[...], preferred_element_type=F32)
    f = jnp.dot((_silu(h1) * h3).astype(BF16), w2_ref[...], preferred_element_type=F32)
    tm = x_ref.shape[1]
    is_ctx = _is_ctx_rows(pl.program_id(1), tm, n_lat)
    xn = _finish_rows(f, x_ref[0], _tile_mod(g2l_ref, g2c_ref, is_ctx), lng_ref[...], lnb_ref[...], alpha)
    x_out[0] = xn
    if with_next:
        u = _ln_plain(xn) * (1.0 + _tile_mod(scl_ref, scc_ref, is_ctx)) + _tile_mod(shl_ref, shc_ref, is_ctx)
        u_out[0] = u.astype(u_out.dtype)


def _ffn(tok, x_all, w1, w3, w2, g2, ln_g, ln_b, n_lat, alpha, nxt=None):
    b, s, d = x_all.shape
    tm = TM_FFN
    with_next = nxt is not None

    def tokspec():
        return pl.BlockSpec((1, tm, d), lambda i, j: (i, j, 0))

    def resident(arr):
        return pl.BlockSpec(arr.shape, lambda i, j: (0, 0), pipeline_mode=pl.Buffered(1))

    vec = pl.BlockSpec((1, d), lambda i, j: (0, 0))
    in_specs = [tokspec(), tokspec(), resident(w1), resident(w3), resident(w2)] + _row_specs(d, b) + [vec, vec]
    args = [tok, x_all, w1, w3, w2, g2, g2, ln_g, ln_b]
    out_specs = [tokspec()]
    out_shape = [jax.ShapeDtypeStruct((b, s, d), F32)]
    if with_next:
        in_specs += _row_specs(d, b) * 2
        args += [nxt[0], nxt[0], nxt[1], nxt[1]]
        out_specs.append(tokspec())
        out_shape.append(jax.ShapeDtypeStruct((b, s, d), BF16))
    res = pl.pallas_call(
        functools.partial(_ffn_body, n_lat=n_lat, alpha=alpha, with_next=with_next),
        grid=(b, s // tm),
        in_specs=in_specs,
        out_specs=out_specs,
        out_shape=out_shape,
        compiler_params=_cparams(("arbitrary", "arbitrary")),
        name="ffn_dense",
    )(*args)
    return res if with_next else (res[0], None)


def _moe_tables(route, counts, n_rows_tok):
    r = MOE_ROWS
    tm = TM_MOE
    b = route.shape[0]
    cnt = counts[0, :N_EXPERTS].astype(jnp.int32)
    nblk = (cnt + r - 1) // r
    bend = jnp.cumsum(nblk)
    bstart = bend - nblk
    total = bend[-1]
    n_blocks = -(-(b * n_rows_tok * TOP_K) // r) + N_EXPERTS
    part = route[:, :n_rows_tok]
    experts = jnp.arange(N_EXPERTS, dtype=F32)
    seg_row0 = jnp.sum((part[:, :, 0:TOP_K, None] == experts).astype(jnp.int32) * (bstart * r), axis=-1)
    dest = seg_row0 + part[:, :, 4:4 + TOP_K].astype(jnp.int32)
    nt = n_rows_tok // tm
    dest_tiles = dest.reshape(b, nt, tm, TOP_K).transpose(0, 1, 3, 2).reshape(b * nt, 1, TOP_K * tm)
    blk = jnp.arange(n_blocks, dtype=jnp.int32)
    e_of = jnp.sum((blk[:, None] >= bend[None, :]).astype(jnp.int32), axis=1)
    e_last = jnp.sum((total - 1 >= bend).astype(jnp.int32))
    active = (blk < total).astype(jnp.int32)
    block_e = jnp.minimum(e_of, e_last)
    spare = total + jnp.arange(N_EXPERTS, dtype=jnp.int32)
    zero_blk = jnp.concatenate([bend - 1, spare])
    zero_on = jnp.concatenate([nblk > 0, spare < n_blocks]).astype(jnp.int32)
    zero_blk = jnp.clip(zero_blk, 0, n_blocks - 1)
    return dest_tiles, zero_blk, zero_on, block_e, active


def _dispatch_body(zblk_ref, zon_ref, dest_ref, tok_ref, xs_hbm, zbuf, zsem, dsem):
    i = pl.program_id(0)
    tm = tok_ref.shape[1]
    r = zbuf.shape[0]

    def zero_copy(z):
        start = pl.multiple_of(zblk_ref[z] * r, r)
        return pltpu.make_async_copy(zbuf, xs_hbm.at[pl.ds(start, r)], zsem)

    @pl.when(i == 0)
    def _():
        zbuf[...] = jnp.zeros_like(zbuf)
        for z in range(2 * N_EXPERTS):
            @pl.when(zon_ref[z] > 0)
            def _():
                zero_copy(z).start()
        for z in range(2 * N_EXPERTS):
            @pl.when(zon_ref[z] > 0)
            def _():
                zero_copy(z).wait()

    def row_copy(j, k):
        return pltpu.make_async_copy(tok_ref.at[0, pl.ds(j, 1)],
                                     xs_hbm.at[pl.ds(dest_ref[0, 0, k * tm + j], 1)], dsem)

    def issue(j, c):
        for k in range(TOP_K):
            row_copy(j, k).start(priority=k % 2)
        return c

    lax.fori_loop(0, tm, issue, 0, unroll=8)
    for _ in range(TOP_K):
        pltpu.make_async_copy(tok_ref.at[0], xs_hbm.at[pl.ds(0, tm)], dsem).wait()


def _moe_dispatch(tok, zero_blk, zero_on, dest_tiles, n_blocks, n_rows_tok):
    d = tok.shape[2]
    tm = TM_MOE
    nt = n_rows_tok // tm
    grid_spec = pltpu.PrefetchScalarGridSpec(
        num_scalar_prefetch=2,
        grid=(dest_tiles.shape[0],),
        in_specs=[pl.BlockSpec((1, 1, TOP_K * tm), lambda i, zb, zo: (i, 0, 0), memory_space=pltpu.SMEM),
                  pl.BlockSpec((1, tm, d), lambda i, zb, zo: (i // nt, i % nt, 0))],
        out_specs=pl.BlockSpec(memory_space=pl.ANY),
        scratch_shapes=[pltpu.VMEM((MOE_ROWS, d), F32),
                        pltpu.SemaphoreType.DMA(()), pltpu.SemaphoreType.DMA(())],
    )
    return pl.pallas_call(
        _dispatch_body,
        grid_spec=grid_spec,
        out_shape=jax.ShapeDtypeStruct((n_blocks * MOE_ROWS, d), F32),
        compiler_params=_cparams(("arbitrary",)),
        name="moe_dispatch",
    )(zero_blk, zero_on, dest_tiles, tok)


def _experts_body(be_ref, act_ref, x_ref, w1_ref, w3_ref, w2_ref, y_ref, xb_ref):
    i = pl.program_id(0)
    f = pl.program_id(1)

    @pl.when((act_ref[i] == 0) & (f == 0))
    def _():
        y_ref[...] = jnp.zeros_like(y_ref)

    @pl.when(act_ref[i] > 0)
    def _():
        @pl.when(f == 0)
        def _():
            xb_ref[...] = x_ref[...].astype(BF16)

        x = xb_ref[...]
        h1 = jnp.dot(x, w1_ref[0, 0].astype(BF16), preferred_element_type=F32)
        h3 = jnp.dot(x, w3_ref[0, 0].astype(BF16), preferred_element_type=F32)
        contrib = jnp.dot((_silu(h1) * h3).astype(BF16), w2_ref[0, 0].astype(BF16), preferred_element_type=F32)

        @pl.when(f == 0)
        def _():
            y_ref[...] = contrib

        @pl.when(f > 0)
        def _():
            y_ref[...] += contrib


def _moe_experts(xs, block_e, active, w1, w3, w2, li):
    d = xs.shape[1]
    dff = w1.shape[3]
    r, tf = MOE_ROWS, TF_MOE
    nf = dff // tf
    n_blocks = block_e.shape[0]

    def f_eff(i, f, act):
        return jnp.where(act[i] > 0, f, nf - 1)

    grid_spec = pltpu.PrefetchScalarGridSpec(
        num_scalar_prefetch=2,
        grid=(n_blocks, nf),
        in_specs=[pl.BlockSpec((r, d), lambda i, f, be, act: (i, 0)),
                  pl.BlockSpec((1, 1, d, tf), lambda i, f, be, act: (li, be[i], 0, f_eff(i, f, act))),
                  pl.BlockSpec((1, 1, d, tf), lambda i, f, be, act: (li, be[i], 0, f_eff(i, f, act))),
                  pl.BlockSpec((1, 1, tf, d), lambda i, f, be, act: (li, be[i], f_eff(i, f, act), 0))],
        out_specs=pl.BlockSpec((r, d), lambda i, f, be, act: (i, 0)),
        scratch_shapes=[pltpu.VMEM((r, d), BF16)],
    )
    return pl.pallas_call(
        _experts_body,
        grid_spec=grid_spec,
        out_shape=jax.ShapeDtypeStruct(xs.shape, F32),
        compiler_params=_cparams(("arbitrary", "arbitrary")),
        name="moe_experts",
    )(block_e, active, xs, w1, w3, w2)


def _combine_body(*refs, n_lat, alpha, with_next):
    (dest_ref, dnext_ref, ys_hbm, p_ref, x_ref, g2l_ref, g2c_ref, lng_ref, lnb_ref) = refs[:9]
    if with_next:
        shl_ref, shc_ref, scl_ref, scc_ref, x_out, u_out, ybuf, sem = refs[9:]
    else:
        x_out, ybuf, sem = refs[9:]
    tm = x_ref.shape[1]
    t = pl.program_id(0) * pl.num_programs(1) + pl.program_id(1)
    n_tiles = pl.num_programs(0) * pl.num_programs(1)
    slot = t % 2

    def gather_tile(table_ref, to_slot):
        def issue(j, c):
            for k in range(TOP_K):
                pltpu.make_async_copy(ys_hbm.at[pl.ds(table_ref[0, 0, k * tm + j], 1)],
                                      ybuf.at[to_slot, k, pl.ds(j, 1)], sem.at[to_slot]).start(priority=k % 2)
            return c

        lax.fori_loop(0, tm, issue, 0, unroll=8)

    @pl.when(t == 0)
    def _():
        gather_tile(dest_ref, 0)

    @pl.when(t + 1 < n_tiles)
    def _():
        gather_tile(dnext_ref, 1 - slot)

    for k in range(TOP_K):
        pltpu.make_async_copy(ys_hbm.at[pl.ds(0, tm)], ybuf.at[slot, k], sem.at[slot]).wait()

    is_ctx = _is_ctx_rows(pl.program_id(1), tm, n_lat)
    p = p_ref[0]
    f = p[:, 2:3] * ybuf[slot, 0] + p[:, 3:4] * ybuf[slot, 1]
    xn = _finish_rows(f, x_ref[0], _tile_mod(g2l_ref, g2c_ref, is_ctx), lng_ref[...], lnb_ref[...], alpha)
    x_out[0] = xn
    if with_next:
        u = _ln_plain(xn) * (1.0 + _tile_mod(scl_ref, scc_ref, is_ctx)) + _tile_mod(shl_ref, shc_ref, is_ctx)
        u_out[0] = u.astype(u_out.dtype)


def _combine(ys, dest_tiles, route, x_all, g2, ln_g, ln_b, n_lat, alpha, n_tok_rows, nxt=None):
    b, s, d = x_all.shape
    tm = TM_MOE
    nt = n_tok_rows // tm
    with_next = nxt is not None

    def tok(w):
        return pl.BlockSpec((1, tm, w), lambda i, j: (i, j, 0))

    vec = pl.BlockSpec((1, d), lambda i, j: (0, 0))
    last_tile = b * nt - 1
    in_specs = [pl.BlockSpec((1, 1, TOP_K * tm), lambda i, j: (i * nt + j, 0, 0), memory_space=pltpu.SMEM),
                pl.BlockSpec((1, 1, TOP_K * tm), lambda i, j: (jnp.minimum(i * nt + j + 1, last_tile), 0, 0),
                             memory_space=pltpu.SMEM),
                pl.BlockSpec(memory_space=pl.ANY), tok(LANES), tok(d)] + _row_specs(d, b) + [vec, vec]
    args = [dest_tiles, dest_tiles, ys, route, x_all, g2, g2, ln_g, ln_b]
    out_specs = [tok(d)]
    out_shape = [jax.ShapeDtypeStruct((b, n_tok_rows, d), F32)]
    if with_next:
        in_specs += _row_specs(d, b) * 2
        args += [nxt[0], nxt[0], nxt[1], nxt[1]]
        out_specs.append(tok(d))
        out_shape.append(jax.ShapeDtypeStruct((b, n_tok_rows, d), BF16))
    res = pl.pallas_call(
        functools.partial(_combine_body, n_lat=n_lat, alpha=alpha, with_next=with_next),
        grid=(b, nt),
        in_specs=in_specs,
        out_specs=out_specs,
        out_shape=out_shape,
        scratch_shapes=[pltpu.VMEM((2, TOP_K, tm, d), F32), pltpu.SemaphoreType.DMA((2,))],
        compiler_params=_cparams(("arbitrary", "arbitrary")),
        name="moe_combine",
    )(*args)
    return res if with_next else (res[0], None)


def _block_diag(pool_w):
    g, c, _ = pool_w.shape
    out = jnp.zeros((g * c, g * c), pool_w.dtype)
    for i in range(g):
        out = out.at[i * c:(i + 1) * c, i * c:(i + 1) * c].set(pool_w[i])
    return out


def kernel(x, c, ctx, c_ctx, w_mod, b_mod, w_in, ml_gate_bias, ml_conv_w, ml_norm_w, pool_w, pool_scale,
           na_rpb, w_branch_pool, w_branch_na, w_branch_ml, w_out, ln1_g, ln1_b, ln2_g, ln2_b,
           ffn_w1, ffn_w3, ffn_w2, moe_router_w, moe_router_b, moe_w1, moe_w3, moe_w2):
    b, n_lat, d = x.shape
    n_ctx = ctx.shape[1]
    s = n_lat + n_ctx
    depth = w_in.shape[0]
    alpha = (2 * depth) ** 0.25
    off_g = POOL_WIDTH + 3 * NA_WIDTH + 4 * ML_WIDTH
    n_gate_cols = 4 * ML_HEADS

    x_all = jnp.concatenate([x, ctx], axis=1)
    mod = _mod_vectors(c, c_ctx, w_mod, b_mod)

    def mod_part(layer, k):
        return mod[layer, :, k * d:(k + 1) * d].reshape(MOD_ROWS, 1, d)

    rope = _rope_tables(n_lat)
    u = _ln_mod(x_all, mod_part(0, 0), mod_part(0, 1), n_lat)

    for layer in range(depth):
        last = layer == depth - 1
        w_l = w_in[layer]
        w_main = w_l[:, :off_g].astype(BF16)
        w_g = jnp.zeros((d, LANES), BF16).at[:, :n_gate_cols].set(w_l[:, off_g:off_g + n_gate_cols].astype(BF16))
        w_gate = (0.5 * w_l[:, off_g + n_gate_cols:]).astype(BF16)
        pool_in, na_qkv, mlqk, mlv, mlo, mlg = _proj_main(u, w_main, w_g)

        a = _pool(pool_in, _block_diag(pool_w[layer]).astype(BF16), pool_scale[layer].reshape(1, -1), n_lat)
        n = _na(na_qkv, _na_bias_classes(na_rpb[layer]), n_lat)
        k_ml, qt_ml, vt_ml = _ml_prep(mlqk, mlv, ml_conv_w[layer], rope, n_lat)
        gcol, grow = _ml_gates(mlg, ml_gate_bias[layer])
        m = _ml_scan(k_ml, qt_ml, vt_ml, gcol, grow, mlo, ml_norm_w[layer].reshape(1, -1), n_lat)

        is_moe = layer % 2 == 1
        i = layer // 2
        router = None
        if is_moe:
            rw_f = jnp.zeros((d, LANES), F32).at[:, :N_EXPERTS].set(moe_router_w[i])
            rw_hi = rw_f.astype(BF16)
            rw = jnp.concatenate([rw_hi, (rw_f - rw_hi.astype(F32)).astype(BF16)], axis=1)
            rb = jnp.zeros((1, LANES), F32).at[0, :N_EXPERTS].set(moe_router_b[i])
            router = (rw, rb)
        n_rows_tok = n_lat if last else s
        merged = _merge(a, n, m, u, x_all,
                        (0.5 * w_branch_pool[layer]).astype(BF16), (0.5 * w_branch_na[layer]).astype(BF16),
                        (0.5 * w_branch_ml[layer]).astype(BF16), w_gate, w_out[layer].astype(BF16),
                        mod_part(layer, 2), mod_part(layer, 3), mod_part(layer, 4),
                        ln1_g[layer].reshape(1, d), ln1_b[layer].reshape(1, d), n_lat, alpha, router, n_rows_tok)
        nxt = None if last else (mod_part(layer + 1, 0), mod_part(layer + 1, 1))
        g2 = mod_part(layer, 5)
        lng, lnb = ln2_g[layer].reshape(1, d), ln2_b[layer].reshape(1, d)
        if not is_moe:
            x_mid, tok = merged
            x_all, u = _ffn(tok, x_mid, ffn_w1[i].astype(BF16), ffn_w3[i].astype(BF16), ffn_w2[i].astype(BF16),
                            g2, lng, lnb, n_lat, alpha, nxt)
        else:
            x_mid, tok, route, counts = merged
            x_all, u = _moe_layer(tok, route, counts, x_mid, moe_w1, moe_w3, moe_w2, i, g2, lng, lnb,
                                  n_lat, alpha, n_rows_tok, nxt)
    return x_all[:, :n_lat]


def _moe_layer(tok, route, counts, x_mid, w1, w3, w2, li, g2, lng, lnb, n_lat, alpha, n_rows_tok, nxt):
    dest_tiles, zero_blk, zero_on, block_e, active = _moe_tables(route, counts, n_rows_tok)
    xs = _moe_dispatch(tok, zero_blk, zero_on, dest_tiles, block_e.shape[0], n_rows_tok)
    ys = _moe_experts(xs, block_e, active, w1, w3, w2, li)
    return _combine(ys, dest_tiles, route, x_mid, g2, lng, lnb, n_lat, alpha, n_rows_tok, nxt)
```

```python
import functools

import numpy as np
import jax
import jax.numpy as jnp
from jax import lax
from jax.experimental import pallas as pl
from jax.experimental.pallas import tpu as pltpu

F32 = jnp.float32
BF16 = jnp.bfloat16
HIGHEST = lax.Precision.HIGHEST

GRID_W = 64
POOL_GROUPS = 4
POOL_GROUP_DIM = 64
POOL_WIDTH = POOL_GROUPS * POOL_GROUP_DIM
POOL_HALF_WINDOWS = (1, 2, 4, 8)
NA_HEADS = 4
NA_HEAD_DIM = 64
NA_WIDTH = NA_HEADS * NA_HEAD_DIM
NA_WIN_ROWS = 8
NA_WIN_COLS = 16
ML_HEADS = 4
ML_HEAD_DIM = 128
ML_WIDTH = ML_HEADS * ML_HEAD_DIM
ML_CHUNK = 128
ML_M_INIT = -1e30
ROPE_THETA = 10000.0
N_EXPERTS = 8
TOP_K = 2
N_BRANCHES = 3
LN_EPS = 1e-6

LANES = 128
MOD_ROWS = 16
VMEM_LIMIT = 56 * 1024 * 1024

TM_TOKEN = 768
TM_MERGE = 768
TM_FFN = 384
TF_MOE = 512
MOE_ROWS = 1024
TM_MOE = 256


def _cparams(sem):
    return pltpu.CompilerParams(dimension_semantics=sem, vmem_limit_bytes=VMEM_LIMIT)


def _sigmoid(x):
    return 0.5 + 0.5 * jnp.tanh(0.5 * x)


def _silu(x):
    t = 0.5 * x
    return t + t * jnp.tanh(t)


def _ln_plain(x):
    xc = x - jnp.mean(x, axis=-1, keepdims=True)
    return xc * lax.rsqrt(jnp.mean(xc * xc, axis=-1, keepdims=True) + LN_EPS)


def _is_ctx_rows(tile_idx, tm, n_lat):
    row = tile_idx * tm + lax.broadcasted_iota(jnp.int32, (tm, 1), 0)
    return row >= n_lat


def _tile_mod(lat_ref, ctx_ref, is_ctx):
    return jnp.where(is_ctx, ctx_ref[0], lat_ref[0])


def _mod_body(s_ref, w_ref, b_ref, o_ref):
    s = _silu(s_ref[...])
    o_ref[0] = jnp.dot(s, w_ref[0], precision=HIGHEST, preferred_element_type=F32) + b_ref[0]


def _mod_vectors(c, c_ctx, w_mod, b_mod):
    depth, d, d6 = w_mod.shape
    b = c.shape[0]
    s = jnp.zeros((MOD_ROWS, d), F32).at[:b].set(c).at[b].set(c_ctx)
    tn = 1536
    return pl.pallas_call(
        _mod_body,
        grid=(depth, d6 // tn),
        in_specs=[pl.BlockSpec((MOD_ROWS, d), lambda l, j: (0, 0)),
                  pl.BlockSpec((1, d, tn), lambda l, j: (l, 0, j)),
                  pl.BlockSpec((1, 1, tn), lambda l, j: (l, 0, j))],
        out_specs=pl.BlockSpec((1, MOD_ROWS, tn), lambda l, j: (l, 0, j)),
        out_shape=jax.ShapeDtypeStruct((depth, MOD_ROWS, d6), F32),
        compiler_params=_cparams(("arbitrary", "arbitrary")),
        name="mod_vectors",
    )(s, w_mod, b_mod.reshape(depth, 1, d6))


def _ln_mod_body(x_ref, shl_ref, shc_ref, scl_ref, scc_ref, u_ref, *, n_lat):
    tm = x_ref.shape[1]
    is_ctx = _is_ctx_rows(pl.program_id(1), tm, n_lat)
    sh = _tile_mod(shl_ref, shc_ref, is_ctx)
    sc = _tile_mod(scl_ref, scc_ref, is_ctx)
    u_ref[0] = (_ln_plain(x_ref[0]) * (1.0 + sc) + sh).astype(u_ref.dtype)


def _row_specs(d, nb):
    return [pl.BlockSpec((1, 1, d), lambda b, j: (b, 0, 0)),
            pl.BlockSpec((1, 1, d), lambda b, j: (nb, 0, 0))]


def _ln_mod(x_all, shift, scale, n_lat):
    b, s, d = x_all.shape
    tm = TM_TOKEN
    return pl.pallas_call(
        functools.partial(_ln_mod_body, n_lat=n_lat),
        grid=(b, s // tm),
        in_specs=[pl.BlockSpec((1, tm, d), lambda i, j: (i, j, 0))] + _row_specs(d, b) + _row_specs(d, b),
        out_specs=pl.BlockSpec((1, tm, d), lambda i, j: (i, j, 0)),
        out_shape=jax.ShapeDtypeStruct((b, s, d), BF16),
        compiler_params=_cparams(("arbitrary", "arbitrary")),
        name="ln_mod",
    )(x_all, shift, shift, scale, scale)


def _proj_main_body(u_ref, w_ref, wg_ref, pool_ref, na_ref, mlqk_ref, mlv_ref, mlo_ref, mlg_ref):
    u = u_ref[0]

    def seg(lo, hi):
        return jnp.dot(u, w_ref[:, lo:hi], preferred_element_type=F32)

    off_na = POOL_WIDTH
    off_ml = off_na + 3 * NA_WIDTH
    off_mlv = off_ml + 2 * ML_WIDTH
    off_mlo = off_ml + 3 * ML_WIDTH
    pool_ref[0] = seg(0, off_na)
    na_ref[0, :, :NA_WIDTH] = (seg(off_na, off_na + NA_WIDTH) * NA_HEAD_DIM ** -0.5).astype(na_ref.dtype)
    na_ref[0, :, NA_WIDTH:] = seg(off_na + NA_WIDTH, off_ml).astype(na_ref.dtype)
    mlqk_ref[0] = seg(off_ml, off_mlv)
    mlv_ref[0] = seg(off_mlv, off_mlo).astype(mlv_ref.dtype)
    mlo_ref[0] = seg(off_mlo, off_mlo + ML_WIDTH).astype(mlo_ref.dtype)
    mlg_ref[0] = jnp.dot(u, wg_ref[...], preferred_element_type=F32)


def _proj_main(u, w_main, w_g):
    b, s, d = u.shape
    tm = TM_TOKEN
    widths = (POOL_WIDTH, 3 * NA_WIDTH, 2 * ML_WIDTH, ML_WIDTH, ML_WIDTH, LANES)
    dtypes = (F32, BF16, F32, BF16, BF16, F32)
    return pl.pallas_call(
        _proj_main_body,
        grid=(b, s // tm),
        in_specs=[pl.BlockSpec((1, tm, d), lambda i, j: (i, j, 0)),
                  pl.BlockSpec(w_main.shape, lambda i, j: (0, 0)),
                  pl.BlockSpec(w_g.shape, lambda i, j: (0, 0))],
        out_specs=[pl.BlockSpec((1, tm, w), lambda i, j: (i, j, 0)) for w in widths],
        out_shape=[jax.ShapeDtypeStruct((b, s, w), dt) for w, dt in zip(widths, dtypes)],
        compiler_params=_cparams(("arbitrary", "arbitrary")),
        name="proj_main",
    )(u, w_main, w_g)


def _pool_segment(a, w_bd, scale):
    n = a.shape[0]
    t = lax.broadcasted_iota(jnp.int32, a.shape, 0)
    grp = lax.broadcasted_iota(jnp.int32, a.shape, 1) // POOL_GROUP_DIM

    def up(x, k):
        return jnp.where(t < n - k, pltpu.roll(x, n - k, 0), 0.0)

    def down(x, k):
        return jnp.where(t >= k, pltpu.roll(x, k, 0), 0.0)

    ahead = a
    behind = down(a, 1)
    wsum = jnp.zeros_like(a)
    half = jnp.zeros_like(t)
    for g, h in enumerate(POOL_HALF_WINDOWS):
        if g > 0:
            ahead = ahead + up(ahead, h // 2)
            behind = behind + down(behind, h // 2)
        wsum = jnp.where(grp == g, ahead + behind, wsum)
        half = jnp.where(grp == g, h, half)
    cnt = jnp.minimum(t + half, n) - jnp.maximum(t - half, 0)
    mean = wsum / cnt.astype(F32)
    out = jnp.dot((mean - a).astype(BF16), w_bd, preferred_element_type=F32)
    return out * scale


def _pool_body(a_ref, w_ref, sc_ref, o_ref, *, n_lat):
    w_bd = w_ref[...]
    scale = sc_ref[...]
    s = a_ref.shape[1]
    o_ref[0, :n_lat] = _pool_segment(a_ref[0, :n_lat], w_bd, scale).astype(o_ref.dtype)
    o_ref[0, n_lat:] = _pool_segment(a_ref[0, n_lat:s], w_bd, scale).astype(o_ref.dtype)


def _pool(pool_in, w_bd, scale, n_lat):
    b, s, w = pool_in.shape
    return pl.pallas_call(
        functools.partial(_pool_body, n_lat=n_lat),
        grid=(b,),
        in_specs=[pl.BlockSpec((1, s, w), lambda i: (i, 0, 0)),
                  pl.BlockSpec((w, w), lambda i: (0, 0)),
                  pl.BlockSpec((1, w), lambda i: (0, 0))],
        out_specs=pl.BlockSpec((1, s, w), lambda i: (i, 0, 0)),
        out_shape=jax.ShapeDtypeStruct((b, s, w), BF16),
        compiler_params=_cparams(("arbitrary",)),
        name="pool",
    )(pool_in, w_bd, scale)


NA_ROWS_PER_STEP = 4


def _na_body(q_ref, k_ref, v_ref, bias_ref, o_ref, *, n_lat, n_rows):
    step = pl.program_id(1)
    s_tot = k_ref.shape[1]
    kc = k_ref[0, n_lat:s_tot, :]
    vc = v_ref[0, n_lat:s_tot, :]
    nt = (((1,), (1,)), ((), ()))
    head = lax.broadcasted_iota(jnp.int32, (GRID_W, NA_WIDTH), 1) // NA_HEAD_DIM
    wr = NA_WIN_ROWS

    def one_row(g, local):
        rows = slice(g * GRID_W, (g + 1) * GRID_W)
        q = q_ref[0, rows, :]
        zero = jnp.zeros_like(q)
        q4 = jnp.concatenate([jnp.where(head == h, q, zero) for h in range(NA_HEADS)], axis=0)
        s_ctx = lax.dot_general(q4, kc, nt, preferred_element_type=F32)
        m = jnp.max(s_ctx, axis=-1, keepdims=True)
        if local:
            r = step * NA_ROWS_PER_STEP + g
            rs = jnp.clip(r - wr // 2, 0, n_rows - wr)
            start = pl.multiple_of(rs * GRID_W, GRID_W)
            kw = k_ref[0, pl.ds(start, wr * GRID_W), :]
            vw = v_ref[0, pl.ds(start, wr * GRID_W), :]
            s_loc = lax.dot_general(q4, kw, nt, preferred_element_type=F32) + bias_ref[rs - r + wr - 1]
            m = jnp.maximum(m, jnp.max(s_loc, axis=-1, keepdims=True))
            p_loc = jnp.exp(s_loc - m)
        p_ctx = jnp.exp(s_ctx - m)
        denom = jnp.sum(p_ctx, axis=-1, keepdims=True)
        o4 = jnp.dot(p_ctx.astype(BF16), vc, preferred_element_type=F32)
        if local:
            denom = denom + jnp.sum(p_loc, axis=-1, keepdims=True)
            o4 = o4 + jnp.dot(p_loc.astype(BF16), vw, preferred_element_type=F32)
        o4 = o4 / denom
        out = jnp.zeros((GRID_W, NA_WIDTH), F32)
        for h in range(NA_HEADS):
            out = out + jnp.where(head == h, o4[h * GRID_W:(h + 1) * GRID_W], 0.0)
        o_ref[0, rows, :] = out.astype(o_ref.dtype)

    @pl.when(step * NA_ROWS_PER_STEP < n_rows)
    def _():
        for g in range(NA_ROWS_PER_STEP):
            one_row(g, True)

    @pl.when(step * NA_ROWS_PER_STEP >= n_rows)
    def _():
        for g in range(NA_ROWS_PER_STEP):
            one_row(g, False)


def _na_bias_classes(rpb):
    col = np.arange(GRID_W)
    col_start = np.clip(col - NA_WIN_COLS // 2, 0, GRID_W - NA_WIN_COLS)
    in_window = (col[None, :] >= col_start[:, None]) & (col[None, :] < col_start[:, None] + NA_WIN_COLS)
    col_off = np.clip(col[None, :] - col[:, None], -(NA_WIN_COLS - 1), NA_WIN_COLS - 1) + NA_WIN_COLS - 1
    n_off = 2 * NA_WIN_COLS - 1
    pick = (col_off[None, :, :] == np.arange(n_off)[:, None, None]).astype(np.float32)
    by_col = jnp.einsum('hro,oqk->hrqk', rpb.astype(F32), jnp.asarray(pick), precision=HIGHEST)
    by_col = jnp.where(in_window[None, None], by_col, -jnp.inf)
    b = jnp.stack([by_col[:, o:o + NA_WIN_ROWS] for o in range(NA_WIN_ROWS)], axis=0)
    b = b.transpose(0, 1, 3, 2, 4)
    return b.reshape(NA_WIN_ROWS, NA_HEADS * GRID_W, NA_WIN_ROWS * GRID_W)


def _na(na_qkv, bias_cls, n_lat):
    b, s, _ = na_qkv.shape
    n_rows = n_lat // GRID_W
    tq = NA_ROWS_PER_STEP * GRID_W

    return pl.pallas_call(
        functools.partial(_na_body, n_lat=n_lat, n_rows=n_rows),
        grid=(b, s // tq),
        in_specs=[pl.BlockSpec((1, tq, NA_WIDTH), lambda i, r: (i, r, 0)),
                  pl.BlockSpec((1, s, NA_WIDTH), lambda i, r: (i, 0, 1)),
                  pl.BlockSpec((1, s, NA_WIDTH), lambda i, r: (i, 0, 2)),
                  pl.BlockSpec(bias_cls.shape, lambda i, r: (0, 0, 0))],
        out_specs=pl.BlockSpec((1, tq, NA_WIDTH), lambda i, r: (i, r, 0)),
        out_shape=jax.ShapeDtypeStruct((b, s, NA_WIDTH), BF16),
        compiler_params=_cparams(("arbitrary", "arbitrary")),
        name="nbr_attention",
    )(na_qkv, na_qkv, na_qkv, bias_cls)


def _rope_tables(n_lat):
    half = ML_HEAD_DIM // 2
    nf = half // 2
    inv = ROPE_THETA ** (-np.arange(nf, dtype=np.float64) / nf)
    pos = np.arange(n_lat)
    ang = np.concatenate([(pos // GRID_W)[:, None] * inv[None, :]] * 2
                         + [(pos % GRID_W)[:, None] * inv[None, :]] * 2, axis=1)
    lane = np.arange(ML_HEAD_DIM)
    second = (lane % half) >= nf
    cos = np.cos(ang)
    sin = np.sin(ang)
    sin_from_lo = np.where(second[None, :], sin, 0.0)
    sin_from_hi = np.where(second[None, :], 0.0, -sin)
    return (jnp.asarray(cos, F32), jnp.asarray(sin_from_lo, F32), jnp.asarray(sin_from_hi, F32))


def _conv_silu(x, w):
    n = x.shape[0]
    t = lax.broadcasted_iota(jnp.int32, x.shape, 0)
    prev = jnp.where(t >= 1, pltpu.roll(x, 1, 0), 0.0)
    nxt = jnp.where(t < n - 1, pltpu.roll(x, n - 1, 0), 0.0)
    return _silu(w[0:1] * prev + w[1:2] * x + w[2:3] * nxt)


def _rope(x, cos, sin_lo, sin_hi):
    nf = ML_HEAD_DIM // 4
    return x * cos + pltpu.roll(x, nf, 1) * sin_lo + pltpu.roll(x, ML_HEAD_DIM - nf, 1) * sin_hi


def _ml_prep_body(xq_ref, xk_ref, v_ref, wq_ref, wk_ref, cos_ref, slo_ref, shi_ref, k_ref, qt_ref, vt_ref, *, n_lat):
    s = xq_ref.shape[1]
    wq = wq_ref[...]
    wk = wk_ref[...]
    cos, slo, shi = cos_ref[...], slo_ref[...], shi_ref[...]
    kscale = ML_HEAD_DIM ** -0.5
    for lo, hi, rotary in ((0, n_lat, True), (n_lat, s, False)):
        q = _conv_silu(xq_ref[0, lo:hi], wq)
        k = _conv_silu(xk_ref[0, lo:hi], wk) * kscale
        if rotary:
            q = _rope(q, cos, slo, shi)
            k = _rope(k, cos, slo, shi)
        k_ref[0, lo:hi] = k.astype(k_ref.dtype)
        v = v_ref[0, lo:hi].astype(F32)
        for j in range((hi - lo) // ML_CHUNK):
            rows = slice(j * ML_CHUNK, (j + 1) * ML_CHUNK)
            qt_ref[0, 0, lo // ML_CHUNK + j] = q[rows].T.astype(qt_ref.dtype)
            vt_ref[0, 0, lo // ML_CHUNK + j] = v[rows].T.astype(vt_ref.dtype)


def _ml_prep(mlqk, mlv, conv_w, rope, n_lat):
    b, s, _ = mlqk.shape
    hd = ML_HEAD_DIM
    nc = s // ML_CHUNK
    tab = pl.BlockSpec((n_lat, hd), lambda i, h: (0, 0))
    chunks = pl.BlockSpec((1, 1, nc, hd, ML_CHUNK), lambda i, h: (i, h, 0, 0, 0))
    chunk_shape = jax.ShapeDtypeStruct((b, ML_HEADS, nc, hd, ML_CHUNK), BF16)
    return pl.pallas_call(
        functools.partial(_ml_prep_body, n_lat=n_lat),
        grid=(b, ML_HEADS),
        in_specs=[pl.BlockSpec((1, s, hd), lambda i, h: (i, 0, h)),
                  pl.BlockSpec((1, s, hd), lambda i, h: (i, 0, ML_HEADS + h)),
                  pl.BlockSpec((1, s, hd), lambda i, h: (i, 0, h)),
                  pl.BlockSpec((3, hd), lambda i, h: (0, h)),
                  pl.BlockSpec((3, hd), lambda i, h: (0, ML_HEADS + h)),
                  tab, tab, tab],
        out_specs=[pl.BlockSpec((1, s, hd), lambda i, h: (i, 0, h)), chunks, chunks],
        out_shape=[jax.ShapeDtypeStruct((b, s, ML_WIDTH), BF16), chunk_shape, chunk_shape],
        compiler_params=_cparams(("arbitrary", "arbitrary")),
        name="mlstm_prep",
    )(mlqk, mlqk, mlv, conv_w, conv_w, *rope)


def _log_sigmoid(x):
    return jnp.minimum(x, 0.0) - jnp.log1p(jnp.exp(-jnp.abs(x)))


def _ml_gates_body(gc_ref, bc_ref, oc_ref, or_ref):
    nc = or_ref.shape[1]
    ng = or_ref.shape[2]
    c = ML_CHUNK
    ii = lax.broadcasted_iota(jnp.int32, (c, c), 0)
    jj = lax.broadcasted_iota(jnp.int32, (c, c), 1)
    lower = (ii >= jj).astype(F32)
    upper = (ii <= jj).astype(F32)
    lane_kind = lax.broadcasted_iota(jnp.int32, (c, LANES), 1) // ML_HEADS
    row_kind = lax.broadcasted_iota(jnp.int32, (ng, c), 0) // ML_HEADS
    for j in range(nc):
        g = gc_ref[0, j * c:(j + 1) * c, :] + bc_ref[...]
        lf = _log_sigmoid(g)
        pre = jnp.dot(lower, lf, precision=HIGHEST, preferred_element_type=F32)
        suf = jnp.dot(upper, lf, precision=HIGHEST, preferred_element_type=F32)
        oc_ref[0, j * c:(j + 1) * c, :] = jnp.where(lane_kind == 1, pre, jnp.where(lane_kind == 3, suf, g))
        gr = g.T[:ng]
        lfr = lf.T[:ng]
        pre_r = jnp.dot(lfr, upper, precision=HIGHEST, preferred_element_type=F32)
        suf_r = jnp.dot(lfr, lower, precision=HIGHEST, preferred_element_type=F32)
        or_ref[0, j] = jnp.where(row_kind == 1, pre_r, jnp.where(row_kind == 3, suf_r, gr))


def _ml_gates(mlg, gate_bias):
    b, s, _ = mlg.shape
    nc = s // ML_CHUNK
    ng = 4 * ML_HEADS
    bias_c = jnp.zeros((1, LANES), F32).at[0, :ng].set(gate_bias)
    return pl.pallas_call(
        _ml_gates_body,
        grid=(b,),
        in_specs=[pl.BlockSpec((1, s, LANES), lambda i: (i, 0, 0)),
                  pl.BlockSpec((1, LANES), lambda i: (0, 0))],
        out_specs=[pl.BlockSpec((1, s, LANES), lambda i: (i, 0, 0)),
                   pl.BlockSpec((1, nc, ng, ML_CHUNK), lambda i: (i, 0, 0, 0))],
        out_shape=[jax.ShapeDtypeStruct((b, s, LANES), F32),
                   jax.ShapeDtypeStruct((b, nc, ng, ML_CHUNK), F32)],
        compiler_params=_cparams(("arbitrary",)),
        name="mlstm_gates",
    )(mlg, bias_c)


def _ml_step(k, qt, vt_aug, a_c, b_r, state_t, m, *, forward):
    c = ML_CHUNK
    d = ML_HEAD_DIM
    si = lax.broadcasted_iota(jnp.int32, (c, c), 0)
    ti = lax.broadcasted_iota(jnp.int32, (c, c), 1)
    allowed = (si <= ti) if forward else (si >= ti)
    log_d = jnp.where(allowed, b_r + a_c, -jnp.inf)
    inter = b_r + m
    m_t = jnp.maximum(jnp.max(log_d, axis=0, keepdims=True), inter)
    kq = jnp.dot(k, qt, preferred_element_type=F32)
    s_t = (kq * jnp.exp(log_d - m_t)).astype(BF16)
    w_inter = jnp.exp(inter - m_t)
    tot = (jnp.dot(vt_aug, s_t, preferred_element_type=F32)
           + w_inter * jnp.dot(state_t.astype(BF16), qt, preferred_element_type=F32))
    num = tot[:d]
    den = tot[d:d + 1]
    h_t = num / jnp.maximum(jnp.abs(den), jnp.exp(-m_t))
    b_end = b_r[:, c - 1:c] if forward else b_r[:, 0:1]
    log_w = b_end + a_c
    m_new = jnp.maximum(b_end + m, jnp.max(log_w, axis=0, keepdims=True))
    w_c = jnp.exp(log_w - m_new)
    decay = jnp.exp(b_end + m - m_new)
    kw = (k.astype(F32) * w_c).astype(BF16)
    state_new = decay * state_t + jnp.dot(vt_aug, kw, preferred_element_type=F32)
    return h_t, state_new, m_new


def _ml_scan_body(k_ref, qt_ref, vt_ref, gc_ref, gr_ref, o_ref, nw_ref, out_ref, hsum_ref, *, n_lat):
    c = ML_CHUNK
    d = ML_HEAD_DIM
    s = k_ref.shape[1]
    heads = qt_ref.shape[1]
    nc_lat = n_lat // c
    nc_all = s // c
    ones_row = (lax.broadcasted_iota(jnp.int32, (d, c), 0) == 0).astype(BF16)

    def rows(j):
        return pl.ds(j * c if isinstance(j, int) else pl.multiple_of(j * c, c), c)

    def run(j, h, state, m, forward):
        vt_aug = jnp.concatenate([vt_ref[0, h, j], ones_row], axis=0)
        ig = (0 if forward else 2) * heads + h
        bc = ig + heads
        gc = gc_ref[0, rows(j), :]
        a_c = gc[:, ig:ig + 1] - gc[:, bc:bc + 1]
        b_r = gr_ref[0, j][bc:bc + 1]
        return _ml_step(k_ref[0, rows(j), h * d:(h + 1) * d], qt_ref[0, h, j], vt_aug, a_c, b_r,
                        state, m, forward=forward)

    def finalize(j, h, hsum_t):
        cols = slice(h * d, (h + 1) * d)
        hc = hsum_t - jnp.mean(hsum_t, axis=0, keepdims=True)
        hn = (hc * lax.rsqrt(jnp.mean(hc * hc, axis=0, keepdims=True) + LN_EPS)).T
        gate = _sigmoid(o_ref[0, rows(j), cols].astype(F32))
        out_ref[0, rows(j), cols] = (gate * hn * nw_ref[:, cols]).astype(out_ref.dtype)

    zero_state = jnp.zeros((2 * d, d), F32)
    m0 = jnp.full((1, 1), ML_M_INIT, F32)
    carry = []
    for h in range(heads):
        sf, mf = zero_state, m0
        hc_f = []
        for j in range(nc_lat, nc_all):
            out, sf, mf = run(j, h, sf, mf, True)
            hc_f.append(out)
        sb, mb = zero_state, m0
        hc_b = {}
        for j in range(nc_all - 1, nc_lat - 1, -1):
            out, sb, mb = run(j, h, sb, mb, False)
            hc_b[j] = out
        for idx, j in enumerate(range(nc_lat, nc_all)):
            finalize(j, h, hc_f[idx] + hc_b[j])
        carry += [sf, mf, sb, mb]

    half = nc_lat // 2

    def first_half(i, carry):
        jb = nc_lat - 1 - i
        new = []
        for h in range(heads):
            sf, mf, sb, mb = carry[4 * h:4 * h + 4]
            hf, sf, mf = run(i, h, sf, mf, True)
            hb, sb, mb = run(jb, h, sb, mb, False)
            hsum_ref[h, i] = hf
            hsum_ref[h, jb] = hb
            new += [sf, mf, sb, mb]
        return tuple(new)

    carry = lax.fori_loop(0, half, first_half, tuple(carry))

    def second_half(i, carry):
        jb = nc_lat - 1 - i
        new = []
        for h in range(heads):
            sf, mf, sb, mb = carry[4 * h:4 * h + 4]
            hf, sf, mf = run(i, h, sf, mf, True)
            hb, sb, mb = run(jb, h, sb, mb, False)
            finalize(i, h, hf + hsum_ref[h, i])
            finalize(jb, h, hb + hsum_ref[h, jb])
            new += [sf, mf, sb, mb]
        return tuple(new)

    lax.fori_loop(half, nc_lat, second_half, carry)


def _ml_scan(k, qt, vt, gcol, grow, mlo, norm_w, n_lat):
    b, s, w = k.shape
    hd = ML_HEAD_DIM
    nc = s // ML_CHUNK
    tok = pl.BlockSpec((1, s, w), lambda i: (i, 0, 0))
    chunks = pl.BlockSpec((1, ML_HEADS, nc, hd, ML_CHUNK), lambda i: (i, 0, 0, 0, 0))
    return pl.pallas_call(
        functools.partial(_ml_scan_body, n_lat=n_lat),
        grid=(b,),
        in_specs=[tok, chunks, chunks,
                  pl.BlockSpec((1, s, LANES), lambda i: (i, 0, 0)),
                  pl.BlockSpec((1,) + grow.shape[1:], lambda i: (i, 0, 0, 0)),
                  tok,
                  pl.BlockSpec((1, w), lambda i: (0, 0))],
        out_specs=tok,
        out_shape=jax.ShapeDtypeStruct((b, s, w), BF16),
        scratch_shapes=[pltpu.VMEM((ML_HEADS, n_lat // ML_CHUNK, hd, ML_CHUNK), F32)],
        compiler_params=_cparams(("arbitrary",)),
        name="mlstm_scan",
    )(k, qt, vt, gcol, grow, mlo, norm_w)


def _top2(logits, n_valid):
    lane = lax.broadcasted_iota(jnp.int32, logits.shape, 1).astype(F32)
    neg = jnp.float32(-jnp.inf)
    l = jnp.where(lane < n_valid, logits, neg)
    m1 = jnp.max(l, axis=-1, keepdims=True)
    e1 = jnp.min(jnp.where(l == m1, lane, float(LANES)), axis=-1, keepdims=True)
    l2 = jnp.where(lane == e1, neg, l)
    m2 = jnp.max(l2, axis=-1, keepdims=True)
    e2 = jnp.min(jnp.where(l2 == m2, lane, float(LANES)), axis=-1, keepdims=True)
    x2 = jnp.exp(m2 - m1)
    p1 = 1.0 / (1.0 + x2)
    p2 = x2 / (1.0 + x2)
    return jnp.where(lane == 0, e1, jnp.where(lane == 1, e2, jnp.where(lane == 2, p1, jnp.where(lane == 3, p2, 0.0))))


def _route_ranks(route, valid, cnt_ref):
    tm = route.shape[0]
    lane = lax.broadcasted_iota(jnp.int32, route.shape, 1).astype(F32)
    pick1 = jnp.where(valid, (lane == route[:, 0:1]).astype(F32), 0.0)
    pick2 = jnp.where(valid, (lane == route[:, 1:2]).astype(F32), 0.0)
    both = pick1 + pick2
    before = (lax.broadcasted_iota(jnp.int32, (tm, tm), 0) > lax.broadcasted_iota(jnp.int32, (tm, tm), 1))
    prefix = jnp.dot(before.astype(BF16), both.astype(BF16), preferred_element_type=F32) + cnt_ref[...]
    rank1 = jnp.sum(prefix * pick1, axis=-1, keepdims=True)
    rank2 = jnp.sum(prefix * pick2, axis=-1, keepdims=True)
    cnt_ref[...] += jnp.sum(both, axis=0, keepdims=True)
    return jnp.where(lane == 4, rank1, jnp.where(lane == 5, rank2, route))


def _merge_body(*refs, n_lat, alpha, moe, n_rows_tok):
    (a_ref, n_ref, m_ref, u_ref, x_ref, wbp_ref, wbn_ref, wbm_ref, wg_ref, wo_ref,
     g1l_ref, g1c_ref, shl_ref, shc_ref, scl_ref, scc_ref, lng_ref, lnb_ref) = refs[:18]
    if moe:
        rw_ref, rb_ref, x_out, tok_out, route_out, cnt_out, cnt_ref = refs[18:]
    else:
        x_out, tok_out = refs[18:]
    tm = x_ref.shape[1]
    d = x_ref.shape[2]
    is_ctx = _is_ctx_rows(pl.program_id(1), tm, n_lat)
    u = u_ref[0]
    y = jnp.zeros((tm, d), F32)
    for k, (br_ref, w_ref) in enumerate(((a_ref, wbp_ref), (n_ref, wbn_ref), (m_ref, wbm_ref))):
        t = jnp.tanh(jnp.dot(u, wg_ref[:, k * d:(k + 1) * d], preferred_element_type=F32))
        half_v = jnp.dot(br_ref[0], w_ref[...], preferred_element_type=F32)
        y = y + (half_v + half_v * t)
    y = jnp.dot(y.astype(BF16), wo_ref[...], preferred_element_type=F32)
    z = alpha * x_ref[0] + _tile_mod(g1l_ref, g1c_ref, is_ctx) * y
    xn = _ln_plain(z) * lng_ref[...] + lnb_ref[...]
    x_out[0] = xn
    tok = _ln_plain(xn) * (1.0 + _tile_mod(scl_ref, scc_ref, is_ctx)) + _tile_mod(shl_ref, shc_ref, is_ctx)
    tok_out[0] = tok.astype(tok_out.dtype)
    if moe:
        @pl.when((pl.program_id(0) == 0) & (pl.program_id(1) == 0))
        def _():
            cnt_ref[...] = jnp.zeros_like(cnt_ref)

        tok_hi = tok.astype(BF16)
        tok_lo = (tok - tok_hi.astype(F32)).astype(BF16)
        both = jnp.dot(tok_hi, rw_ref[...], preferred_element_type=F32)
        logits = (both[:, :LANES] + both[:, LANES:]
                  + jnp.dot(tok_lo, rw_ref[:, :LANES], preferred_element_type=F32))
        valid = (pl.program_id(1) * tm + lax.broadcasted_iota(jnp.int32, (tm, 1), 0)) < n_rows_tok
        route_out[0] = _route_ranks(_top2(logits + rb_ref[...], N_EXPERTS), valid, cnt_ref)
        cnt_out[...] = cnt_ref[...]


def _merge(a, n, m, u, x_all, w_bp, w_bn, w_bm, w_gate, w_o, g1, sh2, sc2, ln_g, ln_b, n_lat, alpha,
           router=None, n_rows_tok=0):
    b, s, d = x_all.shape
    tm = TM_MERGE
    moe = router is not None

    def tok(w):
        return pl.BlockSpec((1, tm, w), lambda i, j: (i, j, 0))

    def full(arr):
        return pl.BlockSpec(arr.shape, lambda i, j: (0,) * arr.ndim, pipeline_mode=pl.Buffered(1))

    in_specs = ([tok(a.shape[2]), tok(n.shape[2]), tok(m.shape[2]), tok(d), tok(d),
                 full(w_bp), full(w_bn), full(w_bm), full(w_gate), full(w_o)]
                + _row_specs(d, b) * 3 + [full(ln_g), full(ln_b)])
    args = [a, n, m, u, x_all, w_bp, w_bn, w_bm, w_gate, w_o, g1, g1, sh2, sh2, sc2, sc2, ln_g, ln_b]
    out_specs = [tok(d), tok(d)]
    out_shape = [jax.ShapeDtypeStruct((b, s, d), F32), jax.ShapeDtypeStruct((b, s, d), F32 if moe else BF16)]
    scratch = []
    if moe:
        in_specs += [full(router[0]), full(router[1])]
        args += list(router)
        out_specs += [tok(LANES), pl.BlockSpec((1, LANES), lambda i, j: (0, 0))]
        out_shape += [jax.ShapeDtypeStruct((b, s, LANES), F32), jax.ShapeDtypeStruct((1, LANES), F32)]
        scratch = [pltpu.VMEM((1, LANES), F32)]
    return pl.pallas_call(
        functools.partial(_merge_body, n_lat=n_lat, alpha=alpha, moe=moe, n_rows_tok=n_rows_tok),
        grid=(b, s // tm),
        in_specs=in_specs,
        out_specs=out_specs,
        out_shape=out_shape,
        scratch_shapes=scratch,
        compiler_params=_cparams(("arbitrary", "arbitrary")),
        name="merge",
    )(*args)


def _finish_rows(f, x, g2, lng, lnb, alpha):
    return _ln_plain(alpha * x + g2 * f) * lng + lnb


def _ffn_body(*refs, n_lat, alpha, with_next):
    (t_ref, x_ref, w1_ref, w3_ref, w2_ref, g2l_ref, g2c_ref, lng_ref, lnb_ref) = refs[:9]
    if with_next:
        shl_ref, shc_ref, scl_ref, scc_ref, x_out, u_out = refs[9:]
    else:
        (x_out,) = refs[9:]
    t = t_ref[0]
    h1 = jnp.dot(t, w1_ref[...], preferred_element_type=F32)
    h3 = jnp.dot(t, w3_ref[...], preferred_element_type=F32)
    f = jnp.dot((_silu(h1) * h3).astype(BF16), w2_ref[...], preferred_element_type=F32)
    tm = x_ref.shape[1]
    is_ctx = _is_ctx_rows(pl.program_id(1), tm, n_lat)
    xn = _finish_rows(f, x_ref[0], _tile_mod(g2l_ref, g2c_ref, is_ctx), lng_ref[...], lnb_ref[...], alpha)
    x_out[0] = xn
    if with_next:
        u = _ln_plain(xn) * (1.0 + _tile_mod(scl_ref, scc_ref, is_ctx)) + _tile_mod(shl_ref, shc_ref, is_ctx)
        u_out[0] = u.astype(u_out.dtype)


def _ffn(tok, x_all, w1, w3, w2, g2, ln_g, ln_b, n_lat, alpha, nxt=None):
    b, s, d = x_all.shape
    tm = TM_FFN
    with_next = nxt is not None

    def tokspec():
        return pl.BlockSpec((1, tm, d), lambda i, j: (i, j, 0))

    def resident(arr):
        return pl.BlockSpec(arr.shape, lambda i, j: (0, 0), pipeline_mode=pl.Buffered(1))

    vec = pl.BlockSpec((1, d), lambda i, j: (0, 0))
    in_specs = [tokspec(), tokspec(), resident(w1), resident(w3), resident(w2)] + _row_specs(d, b) + [vec, vec]
    args = [tok, x_all, w1, w3, w2, g2, g2, ln_g, ln_b]
    out_specs = [tokspec()]
    out_shape = [jax.ShapeDtypeStruct((b, s, d), F32)]
    if with_next:
        in_specs += _row_specs(d, b) * 2
        args += [nxt[0], nxt[0], nxt[1], nxt[1]]
        out_specs.append(tokspec())
        out_shape.append(jax.ShapeDtypeStruct((b, s, d), BF16))
    res = pl.pallas_call(
        functools.partial(_ffn_body, n_lat=n_lat, alpha=alpha, with_next=with_next),
        grid=(b, s // tm),
        in_specs=in_specs,
        out_specs=out_specs,
        out_shape=out_shape,
        compiler_params=_cparams(("arbitrary", "arbitrary")),
        name="ffn_dense",
    )(*args)
    return res if with_next else (res[0], None)


def _moe_tables(route, counts, n_rows_tok):
    r = MOE_ROWS
    tm = TM_MOE
    b = route.shape[0]
    cnt = counts[0, :N_EXPERTS].astype(jnp.int32)
    nblk = (cnt + r - 1) // r
    bend = jnp.cumsum(nblk)
    bstart = bend - nblk
    total = bend[-1]
    n_blocks = -(-(b * n_rows_tok * TOP_K) // r) + N_EXPERTS
    part = route[:, :n_rows_tok]
    experts = jnp.arange(N_EXPERTS, dtype=F32)
    seg_row0 = jnp.sum((part[:, :, 0:TOP_K, None] == experts).astype(jnp.int32) * (bstart * r), axis=-1)
    dest = seg_row0 + part[:, :, 4:4 + TOP_K].astype(jnp.int32)
    nt = n_rows_tok // tm
    dest_tiles = dest.reshape(b, nt, tm, TOP_K).transpose(0, 1, 3, 2).reshape(b * nt, 1, TOP_K * tm)
    blk = jnp.arange(n_blocks, dtype=jnp.int32)
    e_of = jnp.sum((blk[:, None] >= bend[None, :]).astype(jnp.int32), axis=1)
    e_last = jnp.sum((total - 1 >= bend).astype(jnp.int32))
    active = (blk < total).astype(jnp.int32)
    block_e = jnp.minimum(e_of, e_last)
    spare = total + jnp.arange(N_EXPERTS, dtype=jnp.int32)
    zero_blk = jnp.concatenate([bend - 1, spare])
    zero_on = jnp.concatenate([nblk > 0, spare < n_blocks]).astype(jnp.int32)
    zero_blk = jnp.clip(zero_blk, 0, n_blocks - 1)
    return dest_tiles, zero_blk, zero_on, block_e, active


def _dispatch_body(zblk_ref, zon_ref, dest_ref, tok_ref, xs_hbm, zbuf, zsem, dsem):
    i = pl.program_id(0)
    tm = tok_ref.shape[1]
    r = zbuf.shape[0]

    def zero_copy(z):
        start = pl.multiple_of(zblk_ref[z] * r, r)
        return pltpu.make_async_copy(zbuf, xs_hbm.at[pl.ds(start, r)], zsem)

    @pl.when(i == 0)
    def _():
        zbuf[...] = jnp.zeros_like(zbuf)
        for z in range(2 * N_EXPERTS):
            @pl.when(zon_ref[z] > 0)
            def _():
                zero_copy(z).start()
        for z in range(2 * N_EXPERTS):
            @pl.when(zon_ref[z] > 0)
            def _():
                zero_copy(z).wait()

    def row_copy(j, k):
        return pltpu.make_async_copy(tok_ref.at[0, pl.ds(j, 1)],
                                     xs_hbm.at[pl.ds(dest_ref[0, 0, k * tm + j], 1)], dsem)

    def issue(j, c):
        for k in range(TOP_K):
            row_copy(j, k).start(priority=k % 2)
        return c

    lax.fori_loop(0, tm, issue, 0, unroll=8)
    for _ in range(TOP_K):
        pltpu.make_async_copy(tok_ref.at[0], xs_hbm.at[pl.ds(0, tm)], dsem).wait()


def _moe_dispatch(tok, zero_blk, zero_on, dest_tiles, n_blocks, n_rows_tok):
    d = tok.shape[2]
    tm = TM_MOE
    nt = n_rows_tok // tm
    grid_spec = pltpu.PrefetchScalarGridSpec(
        num_scalar_prefetch=2,
        grid=(dest_tiles.shape[0],),
        in_specs=[pl.BlockSpec((1, 1, TOP_K * tm), lambda i, zb, zo: (i, 0, 0), memory_space=pltpu.SMEM),
                  pl.BlockSpec((1, tm, d), lambda i, zb, zo: (i // nt, i % nt, 0))],
        out_specs=pl.BlockSpec(memory_space=pl.ANY),
        scratch_shapes=[pltpu.VMEM((MOE_ROWS, d), F32),
                        pltpu.SemaphoreType.DMA(()), pltpu.SemaphoreType.DMA(())],
    )
    return pl.pallas_call(
        _dispatch_body,
        grid_spec=grid_spec,
        out_shape=jax.ShapeDtypeStruct((n_blocks * MOE_ROWS, d), F32),
        compiler_params=_cparams(("arbitrary",)),
        name="moe_dispatch",
    )(zero_blk, zero_on, dest_tiles, tok)


def _experts_body(be_ref, act_ref, x_ref, w1_ref, w3_ref, w2_ref, y_ref, xb_ref):
    i = pl.program_id(0)
    f = pl.program_id(1)

    @pl.when((act_ref[i] == 0) & (f == 0))
    def _():
        y_ref[...] = jnp.zeros_like(y_ref)

    @pl.when(act_ref[i] > 0)
    def _():
        @pl.when(f == 0)
        def _():
            xb_ref[...] = x_ref[...].astype(BF16)

        x = xb_ref[...]
        h1 = jnp.dot(x, w1_ref[0, 0].astype(BF16), preferred_element_type=F32)
        h3 = jnp.dot(x, w3_ref[0, 0].astype(BF16), preferred_element_type=F32)
        contrib = jnp.dot((_silu(h1) * h3).astype(BF16), w2_ref[0, 0].astype(BF16), preferred_element_type=F32)

        @pl.when(f == 0)
        def _():
            y_ref[...] = contrib

        @pl.when(f > 0)
        def _():
            y_ref[...] += contrib


def _moe_experts(xs, block_e, active, w1, w3, w2, li):
    d = xs.shape[1]
    dff = w1.shape[3]
    r, tf = MOE_ROWS, TF_MOE
    nf = dff // tf
    n_blocks = block_e.shape[0]

    def f_eff(i, f, act):
        return jnp.where(act[i] > 0, f, nf - 1)

    grid_spec = pltpu.PrefetchScalarGridSpec(
        num_scalar_prefetch=2,
        grid=(n_blocks, nf),
        in_specs=[pl.BlockSpec((r, d), lambda i, f, be, act: (i, 0)),
                  pl.BlockSpec((1, 1, d, tf), lambda i, f, be, act: (li, be[i], 0, f_eff(i, f, act))),
                  pl.BlockSpec((1, 1, d, tf), lambda i, f, be, act: (li, be[i], 0, f_eff(i, f, act))),
                  pl.BlockSpec((1, 1, tf, d), lambda i, f, be, act: (li, be[i], f_eff(i, f, act), 0))],
        out_specs=pl.BlockSpec((r, d), lambda i, f, be, act: (i, 0)),
        scratch_shapes=[pltpu.VMEM((r, d), BF16)],
    )
    return pl.pallas_call(
        _experts_body,
        grid_spec=grid_spec,
        out_shape=jax.ShapeDtypeStruct(xs.shape, F32),
        compiler_params=_cparams(("arbitrary", "arbitrary")),
        name="moe_experts",
    )(block_e, active, xs, w1, w3, w2)


def _combine_body(*refs, n_lat, alpha, with_next):
    (dest_ref, dnext_ref, ys_hbm, p_ref, x_ref, g2l_ref, g2c_ref, lng_ref, lnb_ref) = refs[:9]
    if with_next:
        shl_ref, shc_ref, scl_ref, scc_ref, x_out, u_out, ybuf, sem = refs[9:]
    else:
        x_out, ybuf, sem = refs[9:]
    tm = x_ref.shape[1]
    t = pl.program_id(0) * pl.num_programs(1) + pl.program_id(1)
    n_tiles = pl.num_programs(0) * pl.num_programs(1)
    slot = t % 2

    def gather_tile(table_ref, to_slot):
        def issue(j, c):
            for k in range(TOP_K):
                pltpu.make_async_copy(ys_hbm.at[pl.ds(table_ref[0, 0, k * tm + j], 1)],
                                      ybuf.at[to_slot, k, pl.ds(j, 1)], sem.at[to_slot]).start(priority=k % 2)
            return c

        lax.fori_loop(0, tm, issue, 0, unroll=8)

    @pl.when(t == 0)
    def _():
        gather_tile(dest_ref, 0)

    @pl.when(t + 1 < n_tiles)
    def _():
        gather_tile(dnext_ref, 1 - slot)

    for k in range(TOP_K):
        pltpu.make_async_copy(ys_hbm.at[pl.ds(0, tm)], ybuf.at[slot, k], sem.at[slot]).wait()

    is_ctx = _is_ctx_rows(pl.program_id(1), tm, n_lat)
    p = p_ref[0]
    f = p[:, 2:3] * ybuf[slot, 0] + p[:, 3:4] * ybuf[slot, 1]
    xn = _finish_rows(f, x_ref[0], _tile_mod(g2l_ref, g2c_ref, is_ctx), lng_ref[...], lnb_ref[...], alpha)
    x_out[0] = xn
    if with_next:
        u = _ln_plain(xn) * (1.0 + _tile_mod(scl_ref, scc_ref, is_ctx)) + _tile_mod(shl_ref, shc_ref, is_ctx)
        u_out[0] = u.astype(u_out.dtype)


def _combine(ys, dest_tiles, route, x_all, g2, ln_g, ln_b, n_lat, alpha, n_tok_rows, nxt=None):
    b, s, d = x_all.shape
    tm = TM_MOE
    nt = n_tok_rows // tm
    with_next = nxt is not None

    def tok(w):
        return pl.BlockSpec((1, tm, w), lambda i, j: (i, j, 0))

    vec = pl.BlockSpec((1, d), lambda i, j: (0, 0))
    last_tile = b * nt - 1
    in_specs = [pl.BlockSpec((1, 1, TOP_K * tm), lambda i, j: (i * nt + j, 0, 0), memory_space=pltpu.SMEM),
                pl.BlockSpec((1, 1, TOP_K * tm), lambda i, j: (jnp.minimum(i * nt + j + 1, last_tile), 0, 0),
                             memory_space=pltpu.SMEM),
                pl.BlockSpec(memory_space=pl.ANY), tok(LANES), tok(d)] + _row_specs(d, b) + [vec, vec]
    args = [dest_tiles, dest_tiles, ys, route, x_all, g2, g2, ln_g, ln_b]
    out_specs = [tok(d)]
    out_shape = [jax.ShapeDtypeStruct((b, n_tok_rows, d), F32)]
    if with_next:
        in_specs += _row_specs(d, b) * 2
        args += [nxt[0], nxt[0], nxt[1], nxt[1]]
        out_specs.append(tok(d))
        out_shape.append(jax.ShapeDtypeStruct((b, n_tok_rows, d), BF16))
    res = pl.pallas_call(
        functools.partial(_combine_body, n_lat=n_lat, alpha=alpha, with_next=with_next),
        grid=(b, nt),
        in_specs=in_specs,
        out_specs=out_specs,
        out_shape=out_shape,
        scratch_shapes=[pltpu.VMEM((2, TOP_K, tm, d), F32), pltpu.SemaphoreType.DMA((2,))],
        compiler_params=_cparams(("arbitrary", "arbitrary")),
        name="moe_combine",
    )(*args)
    return res if with_next else (res[0], None)


def _block_diag(pool_w):
    g, c, _ = pool_w.shape
    out = jnp.zeros((g * c, g * c), pool_w.dtype)
    for i in range(g):
        out = out.at[i * c:(i + 1) * c, i * c:(i + 1) * c].set(pool_w[i])
    return out


def kernel(x, c, ctx, c_ctx, w_mod, b_mod, w_in, ml_gate_bias, ml_conv_w, ml_norm_w, pool_w, pool_scale,
           na_rpb, w_branch_pool, w_branch_na, w_branch_ml, w_out, ln1_g, ln1_b, ln2_g, ln2_b,
           ffn_w1, ffn_w3, ffn_w2, moe_router_w, moe_router_b, moe_w1, moe_w3, moe_w2):
    b, n_lat, d = x.shape
    n_ctx = ctx.shape[1]
    s = n_lat + n_ctx
    depth = w_in.shape[0]
    alpha = (2 * depth) ** 0.25
    off_g = POOL_WIDTH + 3 * NA_WIDTH + 4 * ML_WIDTH
    n_gate_cols = 4 * ML_HEADS

    x_all = jnp.concatenate([x, ctx], axis=1)
    mod = _mod_vectors(c, c_ctx, w_mod, b_mod)

    def mod_part(layer, k):
        return mod[layer, :, k * d:(k + 1) * d].reshape(MOD_ROWS, 1, d)

    rope = _rope_tables(n_lat)
    u = _ln_mod(x_all, mod_part(0, 0), mod_part(0, 1), n_lat)

    for layer in range(depth):
        last = layer == depth - 1
        w_l = w_in[layer]
        w_main = w_l[:, :off_g].astype(BF16)
        w_g = jnp.zeros((d, LANES), BF16).at[:, :n_gate_cols].set(w_l[:, off_g:off_g + n_gate_cols].astype(BF16))
        w_gate = (0.5 * w_l[:, off_g + n_gate_cols:]).astype(BF16)
        pool_in, na_qkv, mlqk, mlv, mlo, mlg = _proj_main(u, w_main, w_g)

        a = _pool(pool_in, _block_diag(pool_w[layer]).astype(BF16), pool_scale[layer].reshape(1, -1), n_lat)
        n = _na(na_qkv, _na_bias_classes(na_rpb[layer]), n_lat)
        k_ml, qt_ml, vt_ml = _ml_prep(mlqk, mlv, ml_conv_w[layer], rope, n_lat)
        gcol, grow = _ml_gates(mlg, ml_gate_bias[layer])
        m = _ml_scan(k_ml, qt_ml, vt_ml, gcol, grow, mlo, ml_norm_w[layer].reshape(1, -1), n_lat)

        is_moe = layer % 2 == 1
        i = layer // 2
        router = None
        if is_moe:
            rw_f = jnp.zeros((d, LANES), F32).at[:, :N_EXPERTS].set(moe_router_w[i])
            rw_hi = rw_f.astype(BF16)
            rw = jnp.concatenate([rw_hi, (rw_f - rw_hi.astype(F32)).astype(BF16)], axis=1)
            rb = jnp.zeros((1, LANES), F32).at[0, :N_EXPERTS].set(moe_router_b[i])
            router = (rw, rb)
        n_rows_tok = n_lat if last else s
        merged = _merge(a, n, m, u, x_all,
                        (0.5 * w_branch_pool[layer]).astype(BF16), (0.5 * w_branch_na[layer]).astype(BF16),
                        (0.5 * w_branch_ml[layer]).astype(BF16), w_gate, w_out[layer].astype(BF16),
                        mod_part(layer, 2), mod_part(layer, 3), mod_part(layer, 4),
                        ln1_g[layer].reshape(1, d), ln1_b[layer].reshape(1, d), n_lat, alpha, router, n_rows_tok)
        nxt = None if last else (mod_part(layer + 1, 0), mod_part(layer + 1, 1))
        g2 = mod_part(layer, 5)
        lng, lnb = ln2_g[layer].reshape(1, d), ln2_b[layer].reshape(1, d)
        if not is_moe:
            x_mid, tok = merged
            x_all, u = _ffn(tok, x_mid, ffn_w1[i].astype(BF16), ffn_w3[i].astype(BF16), ffn_w2[i].astype(BF16),
                            g2, lng, lnb, n_lat, alpha, nxt)
        else:
            x_mid, tok, route, counts = merged
            x_all, u = _moe_layer(tok, route, counts, x_mid, moe_w1, moe_w3, moe_w2, i, g2, lng, lnb,
                                  n_lat, alpha, n_rows_tok, nxt)
    return x_all[:, :n_lat]


def _moe_layer(tok, route, counts, x_mid, w1, w3, w2, li, g2, lng, lnb, n_lat, alpha, n_rows_tok, nxt):
    dest_tiles, zero_blk, zero_on, block_e, active = _moe_tables(route, counts, n_rows_tok)
    xs = _moe_dispatch(tok, zero_blk, zero_on, dest_tiles, block_e.shape[0], n_rows_tok)
    ys = _moe_experts(xs, block_e, active, w1, w3, w2, li)
    return _combine(ys, dest_tiles, route, x_mid, g2, lng, lnb, n_lat, alpha, n_rows_tok, nxt)
```

```python
import functools

import numpy as np
import jax
import jax.numpy as jnp
from jax import lax
from jax.experimental import pallas as pl
from jax.experimental.pallas import tpu as pltpu

F32 = jnp.float32
BF16 = jnp.bfloat16
HIGHEST = lax.Precision.HIGHEST

GRID_W = 64
POOL_GROUPS = 4
POOL_GROUP_DIM = 64
POOL_WIDTH = POOL_GROUPS * POOL_GROUP_DIM
POOL_HALF_WINDOWS = (1, 2, 4, 8)
NA_HEADS = 4
NA_HEAD_DIM = 64
NA_WIDTH = NA_HEADS * NA_HEAD_DIM
NA_WIN_ROWS = 8
NA_WIN_COLS = 16
ML_HEADS = 4
ML_HEAD_DIM = 128
ML_WIDTH = ML_HEADS * ML_HEAD_DIM
ML_CHUNK = 128
ML_M_INIT = -1e30
ROPE_THETA = 10000.0
N_EXPERTS = 8
TOP_K = 2
N_BRANCHES = 3
LN_EPS = 1e-6

LANES = 128
MOD_ROWS = 16
VMEM_LIMIT = 56 * 1024 * 1024

TM_TOKEN = 768
TM_MERGE = 768
TM_FFN = 384
TF_MOE = 512
MOE_ROWS = 1024
TM_MOE = 256


def _cparams(sem):
    return pltpu.CompilerParams(dimension_semantics=sem, vmem_limit_bytes=VMEM_LIMIT)


def _sigmoid(x):
    return 0.5 + 0.5 * jnp.tanh(0.5 * x)


def _silu(x):
    t = 0.5 * x
    return t + t * jnp.tanh(t)


def _ln_plain(x):
    xc = x - jnp.mean(x, axis=-1, keepdims=True)
    return xc * lax.rsqrt(jnp.mean(xc * xc, axis=-1, keepdims=True) + LN_EPS)


def _is_ctx_rows(tile_idx, tm, n_lat):
    row = tile_idx * tm + lax.broadcasted_iota(jnp.int32, (tm, 1), 0)
    return row >= n_lat


def _tile_mod(lat_ref, ctx_ref, is_ctx):
    return jnp.where(is_ctx, ctx_ref[0], lat_ref[0])


def _mod_body(s_ref, w_ref, b_ref, o_ref):
    s = _silu(s_ref[...])
    o_ref[0] = jnp.dot(s, w_ref[0], precision=HIGHEST, preferred_element_type=F32) + b_ref[0]


def _mod_vectors(c, c_ctx, w_mod, b_mod):
    depth, d, d6 = w_mod.shape
    b = c.shape[0]
    s = jnp.zeros((MOD_ROWS, d), F32).at[:b].set(c).at[b].set(c_ctx)
    tn = 1536
    return pl.pallas_call(
        _mod_body,
        grid=(depth, d6 // tn),
        in_specs=[pl.BlockSpec((MOD_ROWS, d), lambda l, j: (0, 0)),
                  pl.BlockSpec((1, d, tn), lambda l, j: (l, 0, j)),
                  pl.BlockSpec((1, 1, tn), lambda l, j: (l, 0, j))],
        out_specs=pl.BlockSpec((1, MOD_ROWS, tn), lambda l, j: (l, 0, j)),
        out_shape=jax.ShapeDtypeStruct((depth, MOD_ROWS, d6), F32),
        compiler_params=_cparams(("arbitrary", "arbitrary")),
        name="mod_vectors",
    )(s, w_mod, b_mod.reshape(depth, 1, d6))


def _ln_mod_body(x_ref, shl_ref, shc_ref, scl_ref, scc_ref, u_ref, *, n_lat):
    tm = x_ref.shape[1]
    is_ctx = _is_ctx_rows(pl.program_id(1), tm, n_lat)
    sh = _tile_mod(shl_ref, shc_ref, is_ctx)
    sc = _tile_mod(scl_ref, scc_ref, is_ctx)
    u_ref[0] = (_ln_plain(x_ref[0]) * (1.0 + sc) + sh).astype(u_ref.dtype)


def _row_specs(d, nb):
    return [pl.BlockSpec((1, 1, d), lambda b, j: (b, 0, 0)),
            pl.BlockSpec((1, 1, d), lambda b, j: (nb, 0, 0))]


def _ln_mod(x_all, shift, scale, n_lat):
    b, s, d = x_all.shape
    tm = TM_TOKEN
    return pl.pallas_call(
        functools.partial(_ln_mod_body, n_lat=n_lat),
        grid=(b, s // tm),
        in_specs=[pl.BlockSpec((1, tm, d), lambda i, j: (i, j, 0))] + _row_specs(d, b) + _row_specs(d, b),
        out_specs=pl.BlockSpec((1, tm, d), lambda i, j: (i, j, 0)),
        out_shape=jax.ShapeDtypeStruct((b, s, d), BF16),
        compiler_params=_cparams(("arbitrary", "arbitrary")),
        name="ln_mod",
    )(x_all, shift, shift, scale, scale)


def _proj_main_body(u_ref, w_ref, wg_ref, pool_ref, na_ref, mlqk_ref, mlv_ref, mlo_ref, mlg_ref):
    u = u_ref[0]

    def seg(lo, hi):
        return jnp.dot(u, w_ref[:, lo:hi], preferred_element_type=F32)

    off_na = POOL_WIDTH
    off_ml = off_na + 3 * NA_WIDTH
    off_mlv = off_ml + 2 * ML_WIDTH
    off_mlo = off_ml + 3 * ML_WIDTH
    pool_ref[0] = seg(0, off_na)
    na_ref[0, :, :NA_WIDTH] = (seg(off_na, off_na + NA_WIDTH) * NA_HEAD_DIM ** -0.5).astype(na_ref.dtype)
    na_ref[0, :, NA_WIDTH:] = seg(off_na + NA_WIDTH, off_ml).astype(na_ref.dtype)
    mlqk_ref[0] = seg(off_ml, off_mlv)
    mlv_ref[0] = seg(off_mlv, off_mlo).astype(mlv_ref.dtype)
    mlo_ref[0] = seg(off_mlo, off_mlo + ML_WIDTH).astype(mlo_ref.dtype)
    mlg_ref[0] = jnp.dot(u, wg_ref[...], preferred_element_type=F32)


def _proj_main(u, w_main, w_g):
    b, s, d = u.shape
    tm = TM_TOKEN
    widths = (POOL_WIDTH, 3 * NA_WIDTH, 2 * ML_WIDTH, ML_WIDTH, ML_WIDTH, LANES)
    dtypes = (F32, BF16, F32, BF16, BF16, F32)
    return pl.pallas_call(
        _proj_main_body,
        grid=(b, s // tm),
        in_specs=[pl.BlockSpec((1, tm, d), lambda i, j: (i, j, 0)),
                  pl.BlockSpec(w_main.shape, lambda i, j: (0, 0)),
                  pl.BlockSpec(w_g.shape, lambda i, j: (0, 0))],
        out_specs=[pl.BlockSpec((1, tm, w), lambda i, j: (i, j, 0)) for w in widths],
        out_shape=[jax.ShapeDtypeStruct((b, s, w), dt) for w, dt in zip(widths, dtypes)],
        compiler_params=_cparams(("arbitrary", "arbitrary")),
        name="proj_main",
    )(u, w_main, w_g)


def _pool_segment(a, w_bd, scale):
    n = a.shape[0]
    t = lax.broadcasted_iota(jnp.int32, a.shape, 0)
    grp = lax.broadcasted_iota(jnp.int32, a.shape, 1) // POOL_GROUP_DIM

    def up(x, k):
        return jnp.where(t < n - k, pltpu.roll(x, n - k, 0), 0.0)

    def down(x, k):
        return jnp.where(t >= k, pltpu.roll(x, k, 0), 0.0)

    ahead = a
    behind = down(a, 1)
    wsum = jnp.zeros_like(a)
    half = jnp.zeros_like(t)
    for g, h in enumerate(POOL_HALF_WINDOWS):
        if g > 0:
            ahead = ahead + up(ahead, h // 2)
            behind = behind + down(behind, h // 2)
        wsum = jnp.where(grp == g, ahead + behind, wsum)
        half = jnp.where(grp == g, h, half)
    cnt = jnp.minimum(t + half, n) - jnp.maximum(t - half, 0)
    mean = wsum / cnt.astype(F32)
    out = jnp.dot((mean - a).astype(BF16), w_bd, preferred_element_type=F32)
    return out * scale


def _pool_body(a_ref, w_ref, sc_ref, o_ref, *, n_lat):
    w_bd = w_ref[...]
    scale = sc_ref[...]
    s = a_ref.shape[1]
    o_ref[0, :n_lat] = _pool_segment(a_ref[0, :n_lat], w_bd, scale).astype(o_ref.dtype)
    o_ref[0, n_lat:] = _pool_segment(a_ref[0, n_lat:s], w_bd, scale).astype(o_ref.dtype)


def _pool(pool_in, w_bd, scale, n_lat):
    b, s, w = pool_in.shape
    return pl.pallas_call(
        functools.partial(_pool_body, n_lat=n_lat),
        grid=(b,),
        in_specs=[pl.BlockSpec((1, s, w), lambda i: (i, 0, 0)),
                  pl.BlockSpec((w, w), lambda i: (0, 0)),
                  pl.BlockSpec((1, w), lambda i: (0, 0))],
        out_specs=pl.BlockSpec((1, s, w), lambda i: (i, 0, 0)),
        out_shape=jax.ShapeDtypeStruct((b, s, w), BF16),
        compiler_params=_cparams(("arbitrary",)),
        name="pool",
    )(pool_in, w_bd, scale)


NA_ROWS_PER_STEP = 4


def _na_body(q_ref, k_ref, v_ref, bias_ref, o_ref, *, n_lat, n_rows):
    step = pl.program_id(1)
    s_tot = k_ref.shape[1]
    kc = k_ref[0, n_lat:s_tot, :]
    vc = v_ref[0, n_lat:s_tot, :]
    nt = (((1,), (1,)), ((), ()))
    head = lax.broadcasted_iota(jnp.int32, (GRID_W, NA_WIDTH), 1) // NA_HEAD_DIM
    wr = NA_WIN_ROWS

    def one_row(g, local):
        rows = slice(g * GRID_W, (g + 1) * GRID_W)
        q = q_ref[0, rows, :]
        zero = jnp.zeros_like(q)
        q4 = jnp.concatenate([jnp.where(head == h, q, zero) for h in range(NA_HEADS)], axis=0)
        s_ctx = lax.dot_general(q4, kc, nt, preferred_element_type=F32)
        m = jnp.max(s_ctx, axis=-1, keepdims=True)
        if local:
            r = step * NA_ROWS_PER_STEP + g
            rs = jnp.clip(r - wr // 2, 0, n_rows - wr)
            start = pl.multiple_of(rs * GRID_W, GRID_W)
            kw = k_ref[0, pl.ds(start, wr * GRID_W), :]
            vw = v_ref[0, pl.ds(start, wr * GRID_W), :]
            s_loc = lax.dot_general(q4, kw, nt, preferred_element_type=F32) + bias_ref[rs - r + wr - 1]
            m = jnp.maximum(m, jnp.max(s_loc, axis=-1, keepdims=True))
            p_loc = jnp.exp(s_loc - m)
        p_ctx = jnp.exp(s_ctx - m)
        denom = jnp.sum(p_ctx, axis=-1, keepdims=True)
        o4 = jnp.dot(p_ctx.astype(BF16), vc, preferred_element_type=F32)
        if local:
            denom = denom + jnp.sum(p_loc, axis=-1, keepdims=True)
            o4 = o4 + jnp.dot(p_loc.astype(BF16), vw, preferred_element_type=F32)
        o4 = o4 / denom
        out = jnp.zeros((GRID_W, NA_WIDTH), F32)
        for h in range(NA_HEADS):
            out = out + jnp.where(head == h, o4[h * GRID_W:(h + 1) * GRID_W], 0.0)
        o_ref[0, rows, :] = out.astype(o_ref.dtype)

    @pl.when(step * NA_ROWS_PER_STEP < n_rows)
    def _():
        for g in range(NA_ROWS_PER_STEP):
            one_row(g, True)

    @pl.when(step * NA_ROWS_PER_STEP >= n_rows)
    def _():
        for g in range(NA_ROWS_PER_STEP):
            one_row(g, False)


def _na_bias_classes(rpb):
    col = np.arange(GRID_W)
    col_start = np.clip(col - NA_WIN_COLS // 2, 0, GRID_W - NA_WIN_COLS)
    in_window = (col[None, :] >= col_start[:, None]) & (col[None, :] < col_start[:, None] + NA_WIN_COLS)
    col_off = np.clip(col[None, :] - col[:, None], -(NA_WIN_COLS - 1), NA_WIN_COLS - 1) + NA_WIN_COLS - 1
    n_off = 2 * NA_WIN_COLS - 1
    pick = (col_off[None, :, :] == np.arange(n_off)[:, None, None]).astype(np.float32)
    by_col = jnp.einsum('hro,oqk->hrqk', rpb.astype(F32), jnp.asarray(pick), precision=HIGHEST)
    by_col = jnp.where(in_window[None, None], by_col, -jnp.inf)
    b = jnp.stack([by_col[:, o:o + NA_WIN_ROWS] for o in range(NA_WIN_ROWS)], axis=0)
    b = b.transpose(0, 1, 3, 2, 4)
    return b.reshape(NA_WIN_ROWS, NA_HEADS * GRID_W, NA_WIN_ROWS * GRID_W)


def _na(na_qkv, bias_cls, n_lat):
    b, s, _ = na_qkv.shape
    n_rows = n_lat // GRID_W
    tq = NA_ROWS_PER_STEP * GRID_W

    return pl.pallas_call(
        functools.partial(_na_body, n_lat=n_lat, n_rows=n_rows),
        grid=(b, s // tq),
        in_specs=[pl.BlockSpec((1, tq, NA_WIDTH), lambda i, r: (i, r, 0)),
                  pl.BlockSpec((1, s, NA_WIDTH), lambda i, r: (i, 0, 1)),
                  pl.BlockSpec((1, s, NA_WIDTH), lambda i, r: (i, 0, 2)),
                  pl.BlockSpec(bias_cls.shape, lambda i, r: (0, 0, 0))],
        out_specs=pl.BlockSpec((1, tq, NA_WIDTH), lambda i, r: (i, r, 0)),
        out_shape=jax.ShapeDtypeStruct((b, s, NA_WIDTH), BF16),
        compiler_params=_cparams(("arbitrary", "arbitrary")),
        name="nbr_attention",
    )(na_qkv, na_qkv, na_qkv, bias_cls)


def _rope_tables(n_lat):
    half = ML_HEAD_DIM // 2
    nf = half // 2
    inv = ROPE_THETA ** (-np.arange(nf, dtype=np.float64) / nf)
    pos = np.arange(n_lat)
    ang = np.concatenate([(pos // GRID_W)[:, None] * inv[None, :]] * 2
                         + [(pos % GRID_W)[:, None] * inv[None, :]] * 2, axis=1)
    lane = np.arange(ML_HEAD_DIM)
    second = (lane % half) >= nf
    cos = np.cos(ang)
    sin = np.sin(ang)
    sin_from_lo = np.where(second[None, :], sin, 0.0)
    sin_from_hi = np.where(second[None, :], 0.0, -sin)
    return (jnp.asarray(cos, F32), jnp.asarray(sin_from_lo, F32), jnp.asarray(sin_from_hi, F32))


def _conv_silu(x, w):
    n = x.shape[0]
    t = lax.broadcasted_iota(jnp.int32, x.shape, 0)
    prev = jnp.where(t >= 1, pltpu.roll(x, 1, 0), 0.0)
    nxt = jnp.where(t < n - 1, pltpu.roll(x, n - 1, 0), 0.0)
    return _silu(w[0:1] * prev + w[1:2] * x + w[2:3] * nxt)


def _rope(x, cos, sin_lo, sin_hi):
    nf = ML_HEAD_DIM // 4
    return x * cos + pltpu.roll(x, nf, 1) * sin_lo + pltpu.roll(x, ML_HEAD_DIM - nf, 1) * sin_hi


def _ml_prep_body(xq_ref, xk_ref, v_ref, wq_ref, wk_ref, cos_ref, slo_ref, shi_ref, k_ref, qt_ref, vt_ref, *, n_lat):
    s = xq_ref.shape[1]
    wq = wq_ref[...]
    wk = wk_ref[...]
    cos, slo, shi = cos_ref[...], slo_ref[...], shi_ref[...]
    kscale = ML_HEAD_DIM ** -0.5
    for lo, hi, rotary in ((0, n_lat, True), (n_lat, s, False)):
        q = _conv_silu(xq_ref[0, lo:hi], wq)
        k = _conv_silu(xk_ref[0, lo:hi], wk) * kscale
        if rotary:
            q = _rope(q, cos, slo, shi)
            k = _rope(k, cos, slo, shi)
        k_ref[0, lo:hi] = k.astype(k_ref.dtype)
        v = v_ref[0, lo:hi].astype(F32)
        for j in range((hi - lo) // ML_CHUNK):
            rows = slice(j * ML_CHUNK, (j + 1) * ML_CHUNK)
            qt_ref[0, 0, lo // ML_CHUNK + j] = q[rows].T.astype(qt_ref.dtype)
            vt_ref[0, 0, lo // ML_CHUNK + j] = v[rows].T.astype(vt_ref.dtype)


def _ml_prep(mlqk, mlv, conv_w, rope, n_lat):
    b, s, _ = mlqk.shape
    hd = ML_HEAD_DIM
    nc = s // ML_CHUNK
    tab = pl.BlockSpec((n_lat, hd), lambda i, h: (0, 0))
    chunks = pl.BlockSpec((1, 1, nc, hd, ML_CHUNK), lambda i, h: (i, h, 0, 0, 0))
    chunk_shape = jax.ShapeDtypeStruct((b, ML_HEADS, nc, hd, ML_CHUNK), BF16)
    return pl.pallas_call(
        functools.partial(_ml_prep_body, n_lat=n_lat),
        grid=(b, ML_HEADS),
        in_specs=[pl.BlockSpec((1, s, hd), lambda i, h: (i, 0, h)),
                  pl.BlockSpec((1, s, hd), lambda i, h: (i, 0, ML_HEADS + h)),
                  pl.BlockSpec((1, s, hd), lambda i, h: (i, 0, h)),
                  pl.BlockSpec((3, hd), lambda i, h: (0, h)),
                  pl.BlockSpec((3, hd), lambda i, h: (0, ML_HEADS + h)),
                  tab, tab, tab],
        out_specs=[pl.BlockSpec((1, s, hd), lambda i, h: (i, 0, h)), chunks, chunks],
        out_shape=[jax.ShapeDtypeStruct((b, s, ML_WIDTH), BF16), chunk_shape, chunk_shape],
        compiler_params=_cparams(("arbitrary", "arbitrary")),
        name="mlstm_prep",
    )(mlqk, mlqk, mlv, conv_w, conv_w, *rope)


def _log_sigmoid(x):
    return jnp.minimum(x, 0.0) - jnp.log1p(jnp.exp(-jnp.abs(x)))


def _ml_gates_body(gc_ref, bc_ref, oc_ref, or_ref):
    nc = or_ref.shape[1]
    ng = or_ref.shape[2]
    c = ML_CHUNK
    ii = lax.broadcasted_iota(jnp.int32, (c, c), 0)
    jj = lax.broadcasted_iota(jnp.int32, (c, c), 1)
    lower = (ii >= jj).astype(F32)
    upper = (ii <= jj).astype(F32)
    lane_kind = lax.broadcasted_iota(jnp.int32, (c, LANES), 1) // ML_HEADS
    row_kind = lax.broadcasted_iota(jnp.int32, (ng, c), 0) // ML_HEADS
    for j in range(nc):
        g = gc_ref[0, j * c:(j + 1) * c, :] + bc_ref[...]
        lf = _log_sigmoid(g)
        pre = jnp.dot(lower, lf, precision=HIGHEST, preferred_element_type=F32)
        suf = jnp.dot(upper, lf, precision=HIGHEST, preferred_element_type=F32)
        oc_ref[0, j * c:(j + 1) * c, :] = jnp.where(lane_kind == 1, pre, jnp.where(lane_kind == 3, suf, g))
        gr = g.T[:ng]
        lfr = lf.T[:ng]
        pre_r = jnp.dot(lfr, upper, precision=HIGHEST, preferred_element_type=F32)
        suf_r = jnp.dot(lfr, lower, precision=HIGHEST, preferred_element_type=F32)
        or_ref[0, j] = jnp.where(row_kind == 1, pre_r, jnp.where(row_kind == 3, suf_r, gr))


def _ml_gates(mlg, gate_bias):
    b, s, _ = mlg.shape
    nc = s // ML_CHUNK
    ng = 4 * ML_HEADS
    bias_c = jnp.zeros((1, LANES), F32).at[0, :ng].set(gate_bias)
    return pl.pallas_call(
        _ml_gates_body,
        grid=(b,),
        in_specs=[pl.BlockSpec((1, s, LANES), lambda i: (i, 0, 0)),
                  pl.BlockSpec((1, LANES), lambda i: (0, 0))],
        out_specs=[pl.BlockSpec((1, s, LANES), lambda i: (i, 0, 0)),
                   pl.BlockSpec((1, nc, ng, ML_CHUNK), lambda i: (i, 0, 0, 0))],
        out_shape=[jax.ShapeDtypeStruct((b, s, LANES), F32),
                   jax.ShapeDtypeStruct((b, nc, ng, ML_CHUNK), F32)],
        compiler_params=_cparams(("arbitrary",)),
        name="mlstm_gates",
    )(mlg, bias_c)


def _ml_step(k, qt, vt_aug, a_c, b_r, state_t, m, *, forward):
    c = ML_CHUNK
    d = ML_HEAD_DIM
    si = lax.broadcasted_iota(jnp.int32, (c, c), 0)
    ti = lax.broadcasted_iota(jnp.int32, (c, c), 1)
    allowed = (si <= ti) if forward else (si >= ti)
    log_d = jnp.where(allowed, b_r + a_c, -jnp.inf)
    inter = b_r + m
    m_t = jnp.maximum(jnp.max(log_d, axis=0, keepdims=True), inter)
    kq = jnp.dot(k, qt, preferred_element_type=F32)
    s_t = (kq * jnp.exp(log_d - m_t)).astype(BF16)
    w_inter = jnp.exp(inter - m_t)
    tot = (jnp.dot(vt_aug, s_t, preferred_element_type=F32)
           + w_inter * jnp.dot(state_t.astype(BF16), qt, preferred_element_type=F32))
    num = tot[:d]
    den = tot[d:d + 1]
    h_t = num / jnp.maximum(jnp.abs(den), jnp.exp(-m_t))
    b_end = b_r[:, c - 1:c] if forward else b_r[:, 0:1]
    log_w = b_end + a_c
    m_new = jnp.maximum(b_end + m, jnp.max(log_w, axis=0, keepdims=True))
    w_c = jnp.exp(log_w - m_new)
    decay = jnp.exp(b_end + m - m_new)
    kw = (k.astype(F32) * w_c).astype(BF16)
    state_new = decay * state_t + jnp.dot(vt_aug, kw, preferred_element_type=F32)
    return h_t, state_new, m_new


def _ml_scan_body(k_ref, qt_ref, vt_ref, gc_ref, gr_ref, o_ref, nw_ref, out_ref, hsum_ref, *, n_lat):
    c = ML_CHUNK
    d = ML_HEAD_DIM
    s = k_ref.shape[1]
    heads = qt_ref.shape[1]
    nc_lat = n_lat // c
    nc_all = s // c
    ones_row = (lax.broadcasted_iota(jnp.int32, (d, c), 0) == 0).astype(BF16)

    def rows(j):
        return pl.ds(j * c if isinstance(j, int) else pl.multiple_of(j * c, c), c)

    def run(j, h, state, m, forward):
        vt_aug = jnp.concatenate([vt_ref[0, h, j], ones_row], axis=0)
        ig = (0 if forward else 2) * heads + h
        bc = ig + heads
        gc = gc_ref[0, rows(j), :]
        a_c = gc[:, ig:ig + 1] - gc[:, bc:bc + 1]
        b_r = gr_ref[0, j][bc:bc + 1]
        return _ml_step(k_ref[0, rows(j), h * d:(h + 1) * d], qt_ref[0, h, j], vt_aug, a_c, b_r,
                        state, m, forward=forward)

    def finalize(j, h, hsum_t):
        cols = slice(h * d, (h + 1) * d)
        hc = hsum_t - jnp.mean(hsum_t, axis=0, keepdims=True)
        hn = (hc * lax.rsqrt(jnp.mean(hc * hc, axis=0, keepdims=True) + LN_EPS)).T
        gate = _sigmoid(o_ref[0, rows(j), cols].astype(F32))
        out_ref[0, rows(j), cols] = (gate * hn * nw_ref[:, cols]).astype(out_ref.dtype)

    zero_state = jnp.zeros((2 * d, d), F32)
    m0 = jnp.full((1, 1), ML_M_INIT, F32)
    carry = []
    for h in range(heads):
        sf, mf = zero_state, m0
        hc_f = []
        for j in range(nc_lat, nc_all):
            out, sf, mf = run(j, h, sf, mf, True)
            hc_f.append(out)
        sb, mb = zero_state, m0
        hc_b = {}
        for j in range(nc_all - 1, nc_lat - 1, -1):
            out, sb, mb = run(j, h, sb, mb, False)
            hc_b[j] = out
        for idx, j in enumerate(range(nc_lat, nc_all)):
            finalize(j, h, hc_f[idx] + hc_b[j])
        carry += [sf, mf, sb, mb]

    half = nc_lat // 2

    def first_half(i, carry):
        jb = nc_lat - 1 - i
        new = []
        for h in range(heads):
            sf, mf, sb, mb = carry[4 * h:4 * h + 4]
            hf, sf, mf = run(i, h, sf, mf, True)
            hb, sb, mb = run(jb, h, sb, mb, False)
            hsum_ref[h, i] = hf
            hsum_ref[h, jb] = hb
            new += [sf, mf, sb, mb]
        return tuple(new)

    carry = lax.fori_loop(0, half, first_half, tuple(carry))

    def second_half(i, carry):
        jb = nc_lat - 1 - i
        new = []
        for h in range(heads):
            sf, mf, sb, mb = carry[4 * h:4 * h + 4]
            hf, sf, mf = run(i, h, sf, mf, True)
            hb, sb, mb = run(jb, h, sb, mb, False)
            finalize(i, h, hf + hsum_ref[h, i])
            finalize(jb, h, hb + hsum_ref[h, jb])
            new += [sf, mf, sb, mb]
        return tuple(new)

    lax.fori_loop(half, nc_lat, second_half, carry)


def _ml_scan(k, qt, vt, gcol, grow, mlo, norm_w, n_lat):
    b, s, w = k.shape
    hd = ML_HEAD_DIM
    nc = s // ML_CHUNK
    tok = pl.BlockSpec((1, s, w), lambda i: (i, 0, 0))
    chunks = pl.BlockSpec((1, ML_HEADS, nc, hd, ML_CHUNK), lambda i: (i, 0, 0, 0, 0))
    return pl.pallas_call(
        functools.partial(_ml_scan_body, n_lat=n_lat),
        grid=(b,),
        in_specs=[tok, chunks, chunks,
                  pl.BlockSpec((1, s, LANES), lambda i: (i, 0, 0)),
                  pl.BlockSpec((1,) + grow.shape[1:], lambda i: (i, 0, 0, 0)),
                  tok,
                  pl.BlockSpec((1, w), lambda i: (0, 0))],
        out_specs=tok,
        out_shape=jax.ShapeDtypeStruct((b, s, w), BF16),
        scratch_shapes=[pltpu.VMEM((ML_HEADS, n_lat // ML_CHUNK, hd, ML_CHUNK), F32)],
        compiler_params=_cparams(("arbitrary",)),
        name="mlstm_scan",
    )(k, qt, vt, gcol, grow, mlo, norm_w)


def _top2(logits, n_valid):
    lane = lax.broadcasted_iota(jnp.int32, logits.shape, 1).astype(F32)
    neg = jnp.float32(-jnp.inf)
    l = jnp.where(lane < n_valid, logits, neg)
    m1 = jnp.max(l, axis=-1, keepdims=True)
    e1 = jnp.min(jnp.where(l == m1, lane, float(LANES)), axis=-1, keepdims=True)
    l2 = jnp.where(lane == e1, neg, l)
    m2 = jnp.max(l2, axis=-1, keepdims=True)
    e2 = jnp.min(jnp.where(l2 == m2, lane, float(LANES)), axis=-1, keepdims=True)
    x2 = jnp.exp(m2 - m1)
    p1 = 1.0 / (1.0 + x2)
    p2 = x2 / (1.0 + x2)
    return jnp.where(lane == 0, e1, jnp.where(lane == 1, e2, jnp.where(lane == 2, p1, jnp.where(lane == 3, p2, 0.0))))


def _route_ranks(route, valid, cnt_ref):
    tm = route.shape[0]
    lane = lax.broadcasted_iota(jnp.int32, route.shape, 1).astype(F32)
    pick1 = jnp.where(valid, (lane == route[:, 0:1]).astype(F32), 0.0)
    pick2 = jnp.where(valid, (lane == route[:, 1:2]).astype(F32), 0.0)
    both = pick1 + pick2
    before = (lax.broadcasted_iota(jnp.int32, (tm, tm), 0) > lax.broadcasted_iota(jnp.int32, (tm, tm), 1))
    prefix = jnp.dot(before.astype(BF16), both.astype(BF16), preferred_element_type=F32) + cnt_ref[...]
    rank1 = jnp.sum(prefix * pick1, axis=-1, keepdims=True)
    rank2 = jnp.sum(prefix * pick2, axis=-1, keepdims=True)
    cnt_ref[...] += jnp.sum(both, axis=0, keepdims=True)
    return jnp.where(lane == 4, rank1, jnp.where(lane == 5, rank2, route))


def _merge_body(*refs, n_lat, alpha, moe, n_rows_tok):
    (a_ref, n_ref, m_ref, u_ref, x_ref, wbp_ref, wbn_ref, wbm_ref, wg_ref, wo_ref,
     g1l_ref, g1c_ref, shl_ref, shc_ref, scl_ref, scc_ref, lng_ref, lnb_ref) = refs[:18]
    if moe:
        rw_ref, rb_ref, x_out, tok_out, route_out, cnt_out, cnt_ref = refs[18:]
    else:
        x_out, tok_out = refs[18:]
    tm = x_ref.shape[1]
    d = x_ref.shape[2]
    is_ctx = _is_ctx_rows(pl.program_id(1), tm, n_lat)
    u = u_ref[0]
    y = jnp.zeros((tm, d), F32)
    for k, (br_ref, w_ref) in enumerate(((a_ref, wbp_ref), (n_ref, wbn_ref), (m_ref, wbm_ref))):
        t = jnp.tanh(jnp.dot(u, wg_ref[:, k * d:(k + 1) * d], preferred_element_type=F32))
        half_v = jnp.dot(br_ref[0], w_ref[...], preferred_element_type=F32)
        y = y + (half_v + half_v * t)
    y = jnp.dot(y.astype(BF16), wo_ref[...], preferred_element_type=F32)
    z = alpha * x_ref[0] + _tile_mod(g1l_ref, g1c_ref, is_ctx) * y
    xn = _ln_plain(z) * lng_ref[...] + lnb_ref[...]
    x_out[0] = xn
    tok = _ln_plain(xn) * (1.0 + _tile_mod(scl_ref, scc_ref, is_ctx)) + _tile_mod(shl_ref, shc_ref, is_ctx)
    tok_out[0] = tok.astype(tok_out.dtype)
    if moe:
        @pl.when((pl.program_id(0) == 0) & (pl.program_id(1) == 0))
        def _():
            cnt_ref[...] = jnp.zeros_like(cnt_ref)

        tok_hi = tok.astype(BF16)
        tok_lo = (tok - tok_hi.astype(F32)).astype(BF16)
        both = jnp.dot(tok_hi, rw_ref[...], preferred_element_type=F32)
        logits = (both[:, :LANES] + both[:, LANES:]
                  + jnp.dot(tok_lo, rw_ref[:, :LANES], preferred_element_type=F32))
        valid = (pl.program_id(1) * tm + lax.broadcasted_iota(jnp.int32, (tm, 1), 0)) < n_rows_tok
        route_out[0] = _route_ranks(_top2(logits + rb_ref[...], N_EXPERTS), valid, cnt_ref)
        cnt_out[...] = cnt_ref[...]


def _merge(a, n, m, u, x_all, w_bp, w_bn, w_bm, w_gate, w_o, g1, sh2, sc2, ln_g, ln_b, n_lat, alpha,
           router=None, n_rows_tok=0):
    b, s, d = x_all.shape
    tm = TM_MERGE
    moe = router is not None

    def tok(w):
        return pl.BlockSpec((1, tm, w), lambda i, j: (i, j, 0))

    def full(arr):
        return pl.BlockSpec(arr.shape, lambda i, j: (0,) * arr.ndim, pipeline_mode=pl.Buffered(1))

    in_specs = ([tok(a.shape[2]), tok(n.shape[2]), tok(m.shape[2]), tok(d), tok(d),
                 full(w_bp), full(w_bn), full(w_bm), full(w_gate), full(w_o)]
                + _row_specs(d, b) * 3 + [full(ln_g), full(ln_b)])
    args = [a, n, m, u, x_all, w_bp, w_bn, w_bm, w_gate, w_o, g1, g1, sh2, sh2, sc2, sc2, ln_g, ln_b]
    out_specs = [tok(d), tok(d)]
    out_shape = [jax.ShapeDtypeStruct((b, s, d), F32), jax.ShapeDtypeStruct((b, s, d), F32 if moe else BF16)]
    scratch = []
    if moe:
        in_specs += [full(router[0]), full(router[1])]
        args += list(router)
        out_specs += [tok(LANES), pl.BlockSpec((1, LANES), lambda i, j: (0, 0))]
        out_shape += [jax.ShapeDtypeStruct((b, s, LANES), F32), jax.ShapeDtypeStruct((1, LANES), F32)]
        scratch = [pltpu.VMEM((1, LANES), F32)]
    return pl.pallas_call(
        functools.partial(_merge_body, n_lat=n_lat, alpha=alpha, moe=moe, n_rows_tok=n_rows_tok),
        grid=(b, s // tm),
        in_specs=in_specs,
        out_specs=out_specs,
        out_shape=out_shape,
        scratch_shapes=scratch,
        compiler_params=_cparams(("arbitrary", "arbitrary")),
        name="merge",
    )(*args)


def _finish_rows(f, x, g2, lng, lnb, alpha):
    return _ln_plain(alpha * x + g2 * f) * lng + lnb


def _ffn_body(*refs, n_lat, alpha, with_next):
    (t_ref, x_ref, w1_ref, w3_ref, w2_ref, g2l_ref, g2c_ref, lng_ref, lnb_ref) = refs[:9]
    if with_next:
        shl_ref, shc_ref, scl_ref, scc_ref, x_out, u_out = refs[9:]
    else:
        (x_out,) = refs[9:]
    t = t_ref[0]
    h1 = jnp.dot(t, w1_ref[...], preferred_element_type=F32)
    h3 = jnp.dot(t, w3_ref[...], preferred_element_type=F32)
    f = jnp.dot((_silu(h1) * h3).astype(BF16), w2_ref[...], preferred_element_type=F32)
    tm = x_ref.shape[1]
    is_ctx = _is_ctx_rows(pl.program_id(1), tm, n_lat)
    xn = _finish_rows(f, x_ref[0], _tile_mod(g2l_ref, g2c_ref, is_ctx), lng_ref[...], lnb_ref[...], alpha)
    x_out[0] = xn
    if with_next:
        u = _ln_plain(xn) * (1.0 + _tile_mod(scl_ref, scc_ref, is_ctx)) + _tile_mod(shl_ref, shc_ref, is_ctx)
        u_out[0] = u.astype(u_out.dtype)


def _ffn(tok, x_all, w1, w3, w2, g2, ln_g, ln_b, n_lat, alpha, nxt=None):
    b, s, d = x_all.shape
    tm = TM_FFN
    with_next = nxt is not None

    def tokspec():
        return pl.BlockSpec((1, tm, d), lambda i, j: (i, j, 0))

    def resident(arr):
        return pl.BlockSpec(arr.shape, lambda i, j: (0, 0), pipeline_mode=pl.Buffered(1))

    vec = pl.BlockSpec((1, d), lambda i, j: (0, 0))
    in_specs = [tokspec(), tokspec(), resident(w1), resident(w3), resident(w2)] + _row_specs(d, b) + [vec, vec]
    args = [tok, x_all, w1, w3, w2, g2, g2, ln_g, ln_b]
    out_specs = [tokspec()]
    out_shape = [jax.ShapeDtypeStruct((b, s, d), F32)]
    if with_next:
        in_specs += _row_specs(d, b) * 2
        args += [nxt[0], nxt[0], nxt[1], nxt[1]]
        out_specs.append(tokspec())
        out_shape.append(jax.ShapeDtypeStruct((b, s, d), BF16))
    res = pl.pallas_call(
        functools.partial(_ffn_body, n_lat=n_lat, alpha=alpha, with_next=with_next),
        grid=(b, s // tm),
        in_specs=in_specs,
        out_specs=out_specs,
        out_shape=out_shape,
        compiler_params=_cparams(("arbitrary", "arbitrary")),
        name="ffn_dense",
    )(*args)
    return res if with_next else (res[0], None)


def _moe_tables(route, counts, n_rows_tok):
    r = MOE_ROWS
    tm = TM_MOE
    b = route.shape[0]
    cnt = counts[0, :N_EXPERTS].astype(jnp.int32)
    nblk = (cnt + r - 1) // r
    bend = jnp.cumsum(nblk)
    bstart = bend - nblk
    total = bend[-1]
    n_blocks = -(-(b * n_rows_tok * TOP_K) // r) + N_EXPERTS
    part = route[:, :n_rows_tok]
    experts = jnp.arange(N_EXPERTS, dtype=F32)
    seg_row0 = jnp.sum((part[:, :, 0:TOP_K, None] == experts).astype(jnp.int32) * (bstart * r), axis=-1)
    dest = seg_row0 + part[:, :, 4:4 + TOP_K].astype(jnp.int32)
    nt = n_rows_tok // tm
    dest_tiles = dest.reshape(b, nt, tm, TOP_K).transpose(0, 1, 3, 2).reshape(b * nt, 1, TOP_K * tm)
    blk = jnp.arange(n_blocks, dtype=jnp.int32)
    e_of = jnp.sum((blk[:, None] >= bend[None, :]).astype(jnp.int32), axis=1)
    e_last = jnp.sum((total - 1 >= bend).astype(jnp.int32))
    block_e = jnp.minimum(e_of, e_last)
    onehot = (block_e[:, None] == jnp.arange(N_EXPERTS, dtype=jnp.int32)[None, :]).astype(jnp.int32)
    left = jnp.sum(onehot * cnt[None, :], axis=1) - (blk - jnp.sum(onehot * bstart[None, :], axis=1)) * r
    active = jnp.where(blk < total, jnp.clip(left, 0, r), 0).astype(jnp.int32)
    spare = total + jnp.arange(N_EXPERTS, dtype=jnp.int32)
    zero_blk = jnp.concatenate([bend - 1, spare])
    zero_on = jnp.concatenate([nblk > 0, spare < n_blocks]).astype(jnp.int32)
    zero_blk = jnp.clip(zero_blk, 0, n_blocks - 1)
    return dest_tiles, zero_blk, zero_on, block_e, active


def _dispatch_body(zblk_ref, zon_ref, dest_ref, tok_ref, xs_hbm, zbuf, zsem, dsem):
    i = pl.program_id(0)
    tm = tok_ref.shape[1]
    r = zbuf.shape[0]

    def zero_copy(z):
        start = pl.multiple_of(zblk_ref[z] * r, r)
        return pltpu.make_async_copy(zbuf, xs_hbm.at[pl.ds(start, r)], zsem)

    @pl.when(i == 0)
    def _():
        zbuf[...] = jnp.zeros_like(zbuf)
        for z in range(2 * N_EXPERTS):
            @pl.when(zon_ref[z] > 0)
            def _():
                zero_copy(z).start()
        for z in range(2 * N_EXPERTS):
            @pl.when(zon_ref[z] > 0)
            def _():
                zero_copy(z).wait()

    def row_copy(j, k):
        return pltpu.make_async_copy(tok_ref.at[0, pl.ds(j, 1)],
                                     xs_hbm.at[pl.ds(dest_ref[0, 0, k * tm + j], 1)], dsem)

    def issue(j, c):
        for k in range(TOP_K):
            row_copy(j, k).start(priority=k % 2)
        return c

    lax.fori_loop(0, tm, issue, 0, unroll=8)
    for _ in range(TOP_K):
        pltpu.make_async_copy(tok_ref.at[0], xs_hbm.at[pl.ds(0, tm)], dsem).wait()


def _moe_dispatch(tok, zero_blk, zero_on, dest_tiles, n_blocks, n_rows_tok):
    d = tok.shape[2]
    tm = TM_MOE
    nt = n_rows_tok // tm
    grid_spec = pltpu.PrefetchScalarGridSpec(
        num_scalar_prefetch=2,
        grid=(dest_tiles.shape[0],),
        in_specs=[pl.BlockSpec((1, 1, TOP_K * tm), lambda i, zb, zo: (i, 0, 0), memory_space=pltpu.SMEM),
                  pl.BlockSpec((1, tm, d), lambda i, zb, zo: (i // nt, i % nt, 0))],
        out_specs=pl.BlockSpec(memory_space=pl.ANY),
        scratch_shapes=[pltpu.VMEM((MOE_ROWS, d), F32),
                        pltpu.SemaphoreType.DMA(()), pltpu.SemaphoreType.DMA(())],
    )
    return pl.pallas_call(
        _dispatch_body,
        grid_spec=grid_spec,
        out_shape=jax.ShapeDtypeStruct((n_blocks * MOE_ROWS, d), F32),
        compiler_params=_cparams(("arbitrary",)),
        name="moe_dispatch",
    )(zero_blk, zero_on, dest_tiles, tok)


def _experts_body(be_ref, act_ref, x_ref, w1_ref, w3_ref, w2_ref, y_ref, xb_ref):
    i = pl.program_id(0)
    f = pl.program_id(1)

    n_used = act_ref[i]
    half = x_ref.shape[0] // 2

    @pl.when((n_used == 0) & (f == 0))
    def _():
        y_ref[...] = jnp.zeros_like(y_ref)

    def compute(rows):
        @pl.when(f == 0)
        def _():
            xb_ref[rows] = x_ref[rows].astype(BF16)

        x = xb_ref[rows]
        h1 = jnp.dot(x, w1_ref[0, 0].astype(BF16), preferred_element_type=F32)
        h3 = jnp.dot(x, w3_ref[0, 0].astype(BF16), preferred_element_type=F32)
        contrib = jnp.dot((_silu(h1) * h3).astype(BF16), w2_ref[0, 0].astype(BF16), preferred_element_type=F32)

        @pl.when(f == 0)
        def _():
            y_ref[rows] = contrib

        @pl.when(f > 0)
        def _():
            y_ref[rows] += contrib

    @pl.when(n_used > half)
    def _():
        compute(slice(None))

    @pl.when((n_used > 0) & (n_used <= half))
    def _():
        compute(slice(0, half))

        @pl.when(f == 0)
        def _():
            y_ref[half:] = jnp.zeros((half, y_ref.shape[1]), y_ref.dtype)


def _moe_experts(xs, block_e, active, w1, w3, w2, li):
    d = xs.shape[1]
    dff = w1.shape[3]
    r, tf = MOE_ROWS, TF_MOE
    nf = dff // tf
    n_blocks = block_e.shape[0]

    def f_eff(i, f, act):
        return jnp.where(act[i] > 0, f, nf - 1)

    grid_spec = pltpu.PrefetchScalarGridSpec(
        num_scalar_prefetch=2,
        grid=(n_blocks, nf),
        in_specs=[pl.BlockSpec((r, d), lambda i, f, be, act: (i, 0)),
                  pl.BlockSpec((1, 1, d, tf), lambda i, f, be, act: (li, be[i], 0, f_eff(i, f, act))),
                  pl.BlockSpec((1, 1, d, tf), lambda i, f, be, act: (li, be[i], 0, f_eff(i, f, act))),
                  pl.BlockSpec((1, 1, tf, d), lambda i, f, be, act: (li, be[i], f_eff(i, f, act), 0))],
        out_specs=pl.BlockSpec((r, d), lambda i, f, be, act: (i, 0)),
        scratch_shapes=[pltpu.VMEM((r, d), BF16)],
    )
    return pl.pallas_call(
        _experts_body,
        grid_spec=grid_spec,
        out_shape=jax.ShapeDtypeStruct(xs.shape, F32),
        compiler_params=_cparams(("arbitrary", "arbitrary")),
        name="moe_experts",
    )(block_e, active, xs, w1, w3, w2)


def _combine_body(*refs, n_lat, alpha, with_next):
    (dest_ref, dnext_ref, ys_hbm, p_ref, x_ref, g2l_ref, g2c_ref, lng_ref, lnb_ref) = refs[:9]
    if with_next:
        shl_ref, shc_ref, scl_ref, scc_ref, x_out, u_out, ybuf, sem = refs[9:]
    else:
        x_out, ybuf, sem = refs[9:]
    tm = x_ref.shape[1]
    t = pl.program_id(0) * pl.num_programs(1) + pl.program_id(1)
    n_tiles = pl.num_programs(0) * pl.num_programs(1)
    slot = t % 2

    def gather_tile(table_ref, to_slot):
        def issue(j, c):
            for k in range(TOP_K):
                pltpu.make_async_copy(ys_hbm.at[pl.ds(table_ref[0, 0, k * tm + j], 1)],
                                      ybuf.at[to_slot, k, pl.ds(j, 1)], sem.at[to_slot]).start(priority=k % 2)
            return c

        lax.fori_loop(0, tm, issue, 0, unroll=8)

    @pl.when(t == 0)
    def _():
        gather_tile(dest_ref, 0)

    @pl.when(t + 1 < n_tiles)
    def _():
        gather_tile(dnext_ref, 1 - slot)

    for k in range(TOP_K):
        pltpu.make_async_copy(ys_hbm.at[pl.ds(0, tm)], ybuf.at[slot, k], sem.at[slot]).wait()

    is_ctx = _is_ctx_rows(pl.program_id(1), tm, n_lat)
    p = p_ref[0]
    f = p[:, 2:3] * ybuf[slot, 0] + p[:, 3:4] * ybuf[slot, 1]
    xn = _finish_rows(f, x_ref[0], _tile_mod(g2l_ref, g2c_ref, is_ctx), lng_ref[...], lnb_ref[...], alpha)
    x_out[0] = xn
    if with_next:
        u = _ln_plain(xn) * (1.0 + _tile_mod(scl_ref, scc_ref, is_ctx)) + _tile_mod(shl_ref, shc_ref, is_ctx)
        u_out[0] = u.astype(u_out.dtype)


def _combine(ys, dest_tiles, route, x_all, g2, ln_g, ln_b, n_lat, alpha, n_tok_rows, nxt=None):
    b, s, d = x_all.shape
    tm = TM_MOE
    nt = n_tok_rows // tm
    with_next = nxt is not None

    def tok(w):
        return pl.BlockSpec((1, tm, w), lambda i, j: (i, j, 0))

    vec = pl.BlockSpec((1, d), lambda i, j: (0, 0))
    last_tile = b * nt - 1
    in_specs = [pl.BlockSpec((1, 1, TOP_K * tm), lambda i, j: (i * nt + j, 0, 0), memory_space=pltpu.SMEM),
                pl.BlockSpec((1, 1, TOP_K * tm), lambda i, j: (jnp.minimum(i * nt + j + 1, last_tile), 0, 0),
                             memory_space=pltpu.SMEM),
                pl.BlockSpec(memory_space=pl.ANY), tok(LANES), tok(d)] + _row_specs(d, b) + [vec, vec]
    args = [dest_tiles, dest_tiles, ys, route, x_all, g2, g2, ln_g, ln_b]
    out_specs = [tok(d)]
    out_shape = [jax.ShapeDtypeStruct((b, n_tok_rows, d), F32)]
    if with_next:
        in_specs += _row_specs(d, b) * 2
        args += [nxt[0], nxt[0], nxt[1], nxt[1]]
        out_specs.append(tok(d))
        out_shape.append(jax.ShapeDtypeStruct((b, n_tok_rows, d), BF16))
    res = pl.pallas_call(
        functools.partial(_combine_body, n_lat=n_lat, alpha=alpha, with_next=with_next),
        grid=(b, nt),
        in_specs=in_specs,
        out_specs=out_specs,
        out_shape=out_shape,
        scratch_shapes=[pltpu.VMEM((2, TOP_K, tm, d), F32), pltpu.SemaphoreType.DMA((2,))],
        compiler_params=_cparams(("arbitrary", "arbitrary")),
        name="moe_combine",
    )(*args)
    return res if with_next else (res[0], None)


def _block_diag(pool_w):
    g, c, _ = pool_w.shape
    out = jnp.zeros((g * c, g * c), pool_w.dtype)
    for i in range(g):
        out = out.at[i * c:(i + 1) * c, i * c:(i + 1) * c].set(pool_w[i])
    return out


def kernel(x, c, ctx, c_ctx, w_mod, b_mod, w_in, ml_gate_bias, ml_conv_w, ml_norm_w, pool_w, pool_scale,
           na_rpb, w_branch_pool, w_branch_na, w_branch_ml, w_out, ln1_g, ln1_b, ln2_g, ln2_b,
           ffn_w1, ffn_w3, ffn_w2, moe_router_w, moe_router_b, moe_w1, moe_w3, moe_w2):
    b, n_lat, d = x.shape
    n_ctx = ctx.shape[1]
    s = n_lat + n_ctx
    depth = w_in.shape[0]
    alpha = (2 * depth) ** 0.25
    off_g = POOL_WIDTH + 3 * NA_WIDTH + 4 * ML_WIDTH
    n_gate_cols = 4 * ML_HEADS

    x_all = jnp.concatenate([x, ctx], axis=1)
    mod = _mod_vectors(c, c_ctx, w_mod, b_mod)

    def mod_part(layer, k):
        return mod[layer, :, k * d:(k + 1) * d].reshape(MOD_ROWS, 1, d)

    rope = _rope_tables(n_lat)
    u = _ln_mod(x_all, mod_part(0, 0), mod_part(0, 1), n_lat)

    for layer in range(depth):
        last = layer == depth - 1
        w_l = w_in[layer]
        w_main = w_l[:, :off_g].astype(BF16)
        w_g = jnp.zeros((d, LANES), BF16).at[:, :n_gate_cols].set(w_l[:, off_g:off_g + n_gate_cols].astype(BF16))
        w_gate = (0.5 * w_l[:, off_g + n_gate_cols:]).astype(BF16)
        pool_in, na_qkv, mlqk, mlv, mlo, mlg = _proj_main(u, w_main, w_g)

        a = _pool(pool_in, _block_diag(pool_w[layer]).astype(BF16), pool_scale[layer].reshape(1, -1), n_lat)
        n = _na(na_qkv, _na_bias_classes(na_rpb[layer]), n_lat)
        k_ml, qt_ml, vt_ml = _ml_prep(mlqk, mlv, ml_conv_w[layer], rope, n_lat)
        gcol, grow = _ml_gates(mlg, ml_gate_bias[layer])
        m = _ml_scan(k_ml, qt_ml, vt_ml, gcol, grow, mlo, ml_norm_w[layer].reshape(1, -1), n_lat)

        is_moe = layer % 2 == 1
        i = layer // 2
        router = None
        if is_moe:
            rw_f = jnp.zeros((d, LANES), F32).at[:, :N_EXPERTS].set(moe_router_w[i])
            rw_hi = rw_f.astype(BF16)
            rw = jnp.concatenate([rw_hi, (rw_f - rw_hi.astype(F32)).astype(BF16)], axis=1)
            rb = jnp.zeros((1, LANES), F32).at[0, :N_EXPERTS].set(moe_router_b[i])
            router = (rw, rb)
        n_rows_tok = n_lat if last else s
        merged = _merge(a, n, m, u, x_all,
                        (0.5 * w_branch_pool[layer]).astype(BF16), (0.5 * w_branch_na[layer]).astype(BF16),
                        (0.5 * w_branch_ml[layer]).astype(BF16), w_gate, w_out[layer].astype(BF16),
                        mod_part(layer, 2), mod_part(layer, 3), mod_part(layer, 4),
                        ln1_g[layer].reshape(1, d), ln1_b[layer].reshape(1, d), n_lat, alpha, router, n_rows_tok)
        nxt = None if last else (mod_part(layer + 1, 0), mod_part(layer + 1, 1))
        g2 = mod_part(layer, 5)
        lng, lnb = ln2_g[layer].reshape(1, d), ln2_b[layer].reshape(1, d)
        if not is_moe:
            x_mid, tok = merged
            x_all, u = _ffn(tok, x_mid, ffn_w1[i].astype(BF16), ffn_w3[i].astype(BF16), ffn_w2[i].astype(BF16),
                            g2, lng, lnb, n_lat, alpha, nxt)
        else:
            x_mid, tok, route, counts = merged
            x_all, u = _moe_layer(tok, route, counts, x_mid, moe_w1, moe_w3, moe_w2, i, g2, lng, lnb,
                                  n_lat, alpha, n_rows_tok, nxt)
    return x_all[:, :n_lat]


def _moe_layer(tok, route, counts, x_mid, w1, w3, w2, li, g2, lng, lnb, n_lat, alpha, n_rows_tok, nxt):
    dest_tiles, zero_blk, zero_on, block_e, active = _moe_tables(route, counts, n_rows_tok)
    xs = _moe_dispatch(tok, zero_blk, zero_on, dest_tiles, block_e.shape[0], n_rows_tok)
    ys = _moe_experts(xs, block_e, active, w1, w3, w2, li)
    return _combine(ys, dest_tiles, route, x_mid, g2, lng, lnb, n_lat, alpha, n_rows_tok, nxt)
```
